```python
import math
import jax
import jax.numpy as jnp
from jax import lax
import numpy as np

D_MODEL = 1024
BATCH = 8
SEQ = 2048
DEPTH = 2

GRID_W = 64
CTX_LEN = 256
HEAD_DIM = 64
N_GROUPS = 4
GROUP_HEADS = D_MODEL // (N_GROUPS * HEAD_DIM)
GROUP_WIDTH = GROUP_HEADS * HEAD_DIM
MIX_WIDTH = N_GROUPS * GROUP_WIDTH
A_KV_HEADS = GROUP_HEADS // 2
B_KV_HEADS = GROUP_HEADS // 2
D_SUB = HEAD_DIM // 2
WINDOW = 128
BLOCK = 128
NA_KR = 8
NA_KC = 16
D_FF = 2816
ROPE_BASE = 10000.0
NORM_EPS = 1e-6
N_MOD = 9
NEG_INF = -1e30
IN_SPLITS = (GROUP_WIDTH, A_KV_HEADS * HEAD_DIM, A_KV_HEADS * HEAD_DIM,
             GROUP_WIDTH, B_KV_HEADS * HEAD_DIM, B_KV_HEADS * HEAD_DIM,
             GROUP_WIDTH, GROUP_WIDTH, GROUP_WIDTH,
             GROUP_WIDTH, GROUP_WIDTH, GROUP_WIDTH)
IN_WIDTH = int(sum(IN_SPLITS))
IN_SPLIT_OFFSETS = tuple(int(o) for o in np.cumsum(IN_SPLITS)[:-1])

kernel_name = 'hybrid_parallel_heads_diffusion_block'


def rms_norm(x, g=None):
    xf = x.astype(jnp.float32)
    y = xf * lax.rsqrt(jnp.mean(xf * xf, axis=-1, keepdims=True) + NORM_EPS)
    if g is not None:
        y = y * g.astype(jnp.float32)
    return y.astype(x.dtype)


def modulate(x, shift, scale):
    return rms_norm(x) * (1.0 + scale) + shift


def swiglu(h, w_gate, w_up, w_down):
    return (jax.nn.silu(h @ w_gate) * (h @ w_up)) @ w_down


def axial_rope(n_tokens, dim):
    t = jnp.arange(n_tokens, dtype=jnp.int32)
    row = (t // GRID_W).astype(jnp.float32)
    col = (t % GRID_W).astype(jnp.float32)
    half = dim // 2
    freqs = ROPE_BASE ** (-jnp.arange(0, half, 2, dtype=jnp.float32) / half)
    ang_r = row[:, None] * freqs[None, :]
    ang_c = col[:, None] * freqs[None, :]
    ang = jnp.concatenate([ang_r, ang_r, ang_c, ang_c], axis=-1)
    return jnp.cos(ang), jnp.sin(ang)


def apply_rope(x, cos, sin):
    d = x.shape[-1]
    h = d // 2
    qd = h // 2

    def rot(u):
        return jnp.concatenate([-u[..., qd:], u[..., :qd]], axis=-1)

    rotated = jnp.concatenate([rot(x[..., :h]), rot(x[..., h:])], axis=-1)
    shape = (x.shape[1],) + (1,) * (x.ndim - 3) + (d,)
    out = x.astype(jnp.float32) * cos.reshape(shape) + rotated.astype(jnp.float32) * sin.reshape(shape)
    return out.astype(x.dtype)


def window_attention(q, k, v, k_ctx, v_ctx, sink):
    B, S, H, dh = q.shape
    hkv = k.shape[2]
    G = H // hkv
    L = k_ctx.shape[1]
    nb = S // BLOCK
    span = BLOCK + 2 * WINDOW
    pad = ((0, 0), (WINDOW, WINDOW), (0, 0), (0, 0))
    idx = jnp.arange(nb)[:, None] * BLOCK + jnp.arange(span)[None, :]
    kb = jnp.pad(k, pad)[:, idx]
    vb = jnp.pad(v, pad)[:, idx]
    qb = q.reshape(B, nb, BLOCK, hkv, G, dh)
    scale = dh ** -0.5
    s_loc = jnp.einsum('bnqhgd,bnjhd->bnhgqj', qb, kb, preferred_element_type=jnp.float32) * scale
    q_pos = jnp.arange(nb)[:, None] * BLOCK + jnp.arange(BLOCK)[None, :]
    k_pos = idx - WINDOW
    valid = ((jnp.abs(k_pos[:, None, :] - q_pos[:, :, None]) <= WINDOW)
             & (k_pos[:, None, :] >= 0) & (k_pos[:, None, :] < S))
    s_loc = jnp.where(valid[None, :, None, None], s_loc, NEG_INF)
    s_ctx = jnp.einsum('bnqhgd,bjhd->bnhgqj', qb, k_ctx, preferred_element_type=jnp.float32) * scale
    s_sink = jnp.broadcast_to(sink.astype(jnp.float32).reshape(1, 1, hkv, G, 1, 1), s_loc.shape[:-1] + (1,))
    p = jax.nn.softmax(jnp.concatenate([s_loc, s_ctx, s_sink], axis=-1), axis=-1).astype(v.dtype)
    out = (jnp.einsum('bnhgqj,bnjhd->bnqhgd', p[..., :span], vb)
           + jnp.einsum('bnhgqj,bjhd->bnqhgd', p[..., span:span + L], v_ctx))
    return out.reshape(B, S, H, dh)


def context_sink_attention(q, k, v, sink):
    B, L, H, dh = q.shape
    hkv = k.shape[2]
    G = H // hkv
    qg = q.reshape(B, L, hkv, G, dh)
    s = jnp.einsum('bqhgd,bjhd->bhgqj', qg, k, preferred_element_type=jnp.float32) * (dh ** -0.5)
    s_sink = jnp.broadcast_to(sink.astype(jnp.float32).reshape(1, hkv, G, 1, 1), s.shape[:-1] + (1,))
    p = jax.nn.softmax(jnp.concatenate([s, s_sink], axis=-1), axis=-1).astype(v.dtype)
    out = jnp.einsum('bhgqj,bjhd->bqhgd', p[..., :L], v)
    return out.reshape(B, L, H, dh)


def blocked_gqa(q, k, v):
    B, T, H, dh = q.shape
    hkv = k.shape[2]
    G = H // hkv
    nb = T // BLOCK
    scale = dh ** -0.5
    qb = jnp.moveaxis(q.reshape(B, nb, BLOCK, hkv, G, dh), 1, 0)

    def one_block(qblk):
        s = jnp.einsum('bqhgd,bjhd->bhgqj', qblk, k, preferred_element_type=jnp.float32) * scale
        p = jax.nn.softmax(s, axis=-1).astype(v.dtype)
        return jnp.einsum('bhgqj,bjhd->bqhgd', p, v)

    out = lax.map(one_block, qb)
    return jnp.moveaxis(out, 0, 1).reshape(B, T, H, dh)


def neighbourhood_attention(q, k, v, k_ctx, v_ctx, rel_bias, rows):
    B, S, H, dh = q.shape
    kr = min(NA_KR, rows)
    r = jnp.arange(rows)
    r_start = jnp.clip(r - kr // 2, 0, rows - kr)
    key_rows = r_start[:, None] + jnp.arange(kr)[None, :]
    kg = k.reshape(B, rows, GRID_W, H, dh)[:, key_rows].reshape(B, rows, kr * GRID_W, H, dh)
    vg = v.reshape(B, rows, GRID_W, H, dh)[:, key_rows].reshape(B, rows, kr * GRID_W, H, dh)
    qg = q.reshape(B, rows, GRID_W, H, dh)
    scale = dh ** -0.5
    s_loc = jnp.einsum('brqhd,brjhd->brhqj', qg, kg, preferred_element_type=jnp.float32) * scale
    col = jnp.arange(GRID_W)
    c_start = jnp.clip(col - NA_KC // 2, 0, GRID_W - NA_KC)
    col_ok = (col[None, :] >= c_start[:, None]) & (col[None, :] < c_start[:, None] + NA_KC)
    dr_idx = key_rows - r[:, None] + NA_KR - 1
    dc_idx = jnp.clip(col[None, :] - col[:, None] + NA_KC - 1, 0, 2 * NA_KC - 2)
    bias = rel_bias[:, dr_idx[:, None, :, None], dc_idx[None, :, None, :]]
    bias = bias.reshape(H, rows, GRID_W, kr * GRID_W).transpose(1, 0, 2, 3)
    mask = jnp.broadcast_to(col_ok[:, None, :], (GRID_W, kr, GRID_W)).reshape(GRID_W, kr * GRID_W)
    s_loc = jnp.where(mask, s_loc + bias.astype(jnp.float32), NEG_INF)
    s_ctx = jnp.einsum('brqhd,bjhd->brhqj', qg, k_ctx, preferred_element_type=jnp.float32) * scale
    n_loc = kr * GRID_W
    p = jax.nn.softmax(jnp.concatenate([s_loc, s_ctx], axis=-1), axis=-1).astype(v.dtype)
    out = (jnp.einsum('brhqj,brjhd->brqhd', p[..., :n_loc], vg)
           + jnp.einsum('brhqj,bjhd->brqhd', p[..., n_loc:], v_ctx))
    return out.reshape(B, S, H, dh)


def blocked_diff_attention(q, k, v, lam):
    B, T, H = q.shape[:3]
    nb = T // BLOCK
    scale = D_SUB ** -0.5
    qb = jnp.moveaxis(q.reshape(B, nb, BLOCK, H, 2, D_SUB), 1, 0)

    def one_block(qblk):
        s = jnp.einsum('bqhtd,bjhtd->bhtqj', qblk, k, preferred_element_type=jnp.float32) * scale
        p = jax.nn.softmax(s, axis=-1)
        a = (p[:, :, 0] - lam * p[:, :, 1]).astype(v.dtype)
        return jnp.einsum('bhqj,bjhe->bqhe', a, v)

    out = lax.map(one_block, qb)
    return jnp.moveaxis(out, 0, 1).reshape(B, T, H, v.shape[-1])


def hybrid_mixer(h, hc, rows, w_in, w_out, sink, q_norm_g, k_norm_g, rel_bias, lam, lam_init, subln_g,
                 with_ctx_out):
    B, S, _ = h.shape
    L = hc.shape[1]
    cos_h, sin_h = axial_rope(S, HEAD_DIM)
    cos_s, sin_s = axial_rope(S, D_SUB)
    aq, ak, av, bq, bk, bv, cq, ck, cv, dq, dk, dv = jnp.split(h @ w_in, IN_SPLIT_OFFSETS, axis=-1)
    aqc, akc, avc, bqc, bkc, bvc, cqc, ckc, cvc, dqc, dkc, dvc = jnp.split(hc @ w_in, IN_SPLIT_OFFSETS, axis=-1)

    def heads(t, n):
        return t.reshape(t.shape[0], t.shape[1], n, HEAD_DIM)

    def sub_heads(t):
        return t.reshape(t.shape[0], t.shape[1], GROUP_HEADS, 2, D_SUB)

    ka_c, va_c = heads(akc, A_KV_HEADS), heads(avc, A_KV_HEADS)
    y_a = window_attention(apply_rope(heads(aq, GROUP_HEADS), cos_h, sin_h),
                           apply_rope(heads(ak, A_KV_HEADS), cos_h, sin_h),
                           heads(av, A_KV_HEADS), ka_c, va_c, sink)
    kb_c = rms_norm(heads(bkc, B_KV_HEADS), k_norm_g)
    vb_c = heads(bvc, B_KV_HEADS)
    qb = apply_rope(rms_norm(heads(bq, GROUP_HEADS), q_norm_g), cos_h, sin_h)
    kb = apply_rope(rms_norm(heads(bk, B_KV_HEADS), k_norm_g), cos_h, sin_h)
    y_b = blocked_gqa(qb, jnp.concatenate([kb, kb_c], axis=1),
                      jnp.concatenate([heads(bv, B_KV_HEADS), vb_c], axis=1))
    kc_c, vc_c = heads(ckc, GROUP_HEADS), heads(cvc, GROUP_HEADS)
    y_c = neighbourhood_attention(heads(cq, GROUP_HEADS), heads(ck, GROUP_HEADS), heads(cv, GROUP_HEADS),
                                  kc_c, vc_c, rel_bias, rows)
    kd_c, vd_c = sub_heads(dkc), heads(dvc, GROUP_HEADS)
    qd = apply_rope(sub_heads(dq), cos_s, sin_s)
    kd = apply_rope(sub_heads(dk), cos_s, sin_s)
    y_d = blocked_diff_attention(qd, jnp.concatenate([kd, kd_c], axis=1),
                                 jnp.concatenate([heads(dv, GROUP_HEADS), vd_c], axis=1), lam)
    y_d = rms_norm(y_d, subln_g) * (1.0 - lam_init)
    y = jnp.concatenate([t.reshape(B, S, GROUP_WIDTH) for t in (y_a, y_b, y_c, y_d)], axis=-1) @ w_out
    if not with_ctx_out:
        return y, None
    yc_a = context_sink_attention(heads(aqc, GROUP_HEADS), ka_c, va_c, sink)
    yc_b = blocked_gqa(rms_norm(heads(bqc, GROUP_HEADS), q_norm_g), kb_c, vb_c)
    yc_c = blocked_gqa(heads(cqc, GROUP_HEADS), kc_c, vc_c)
    yc_d = rms_norm(blocked_diff_attention(sub_heads(dqc), kd_c, vd_c, lam), subln_g) * (1.0 - lam_init)
    yc = jnp.concatenate([t.reshape(B, L, GROUP_WIDTH) for t in (yc_a, yc_b, yc_c, yc_d)], axis=-1) @ w_out
    return y, yc


def setup_inputs(seed: int = 0) -> dict:
    key = jax.random.key(seed)
    ks = jax.random.split(key, 24)
    f32 = jnp.float32
    nrm = jax.random.normal

    def dense(k, shape, fan_in, gain=1.0):
        return nrm(k, shape, f32) * (gain * fan_in ** -0.5)

    return {
        'x': nrm(ks[0], (BATCH, SEQ, D_MODEL), f32),
        'c': nrm(ks[1], (BATCH, D_MODEL), f32),
        'ctx': nrm(ks[2], (BATCH, CTX_LEN, D_MODEL), f32),
        'c_ctx': nrm(ks[3], (D_MODEL,), f32),
        'w_ada': dense(ks[4], (DEPTH, D_MODEL, N_MOD * D_MODEL), D_MODEL, 0.5),
        'b_ada': 0.02 * nrm(ks[5], (DEPTH, N_MOD * D_MODEL), f32),
        'w_ffn1_gate': dense(ks[6], (DEPTH, D_MODEL, D_FF), D_MODEL),
        'w_ffn1_up': dense(ks[7], (DEPTH, D_MODEL, D_FF), D_MODEL),
        'w_ffn1_down': dense(ks[8], (DEPTH, D_FF, D_MODEL), D_FF),
        'w_in': dense(ks[9], (DEPTH, D_MODEL, IN_WIDTH), D_MODEL),
        'w_out': dense(ks[10], (DEPTH, MIX_WIDTH, D_MODEL), MIX_WIDTH),
        'sink_logit': 0.5 * nrm(ks[11], (DEPTH, GROUP_HEADS), f32),
        'q_norm_g': 1.0 + 0.1 * nrm(ks[12], (DEPTH, HEAD_DIM), f32),
        'k_norm_g': 1.0 + 0.1 * nrm(ks[13], (DEPTH, HEAD_DIM), f32),
        'rel_pos_bias': 0.1 * nrm(ks[14], (DEPTH, GROUP_HEADS, 2 * NA_KR - 1, 2 * NA_KC - 1), f32),
        'lam_q1': 0.1 * nrm(ks[15], (DEPTH, D_SUB), f32),
        'lam_k1': 0.1 * nrm(ks[16], (DEPTH, D_SUB), f32),
        'lam_q2': 0.1 * nrm(ks[17], (DEPTH, D_SUB), f32),
        'lam_k2': 0.1 * nrm(ks[18], (DEPTH, D_SUB), f32),
        'subln_g': 1.0 + 0.1 * nrm(ks[19], (DEPTH, HEAD_DIM), f32),
        'w_ffn2_gate': dense(ks[20], (DEPTH, D_MODEL, D_FF), D_MODEL),
        'w_ffn2_up': dense(ks[21], (DEPTH, D_MODEL, D_FF), D_MODEL),
        'w_ffn2_down': dense(ks[22], (DEPTH, D_FF, D_MODEL), D_FF),
        'final_norm_g': 1.0 + 0.1 * nrm(ks[23], (D_MODEL,), f32),
    }


def reference(x, c, ctx, c_ctx, w_ada, b_ada, w_ffn1_gate, w_ffn1_up, w_ffn1_down, w_in, w_out,
              sink_logit, q_norm_g, k_norm_g, rel_pos_bias, lam_q1, lam_k1, lam_q2, lam_k2, subln_g,
              w_ffn2_gate, w_ffn2_up, w_ffn2_down, final_norm_g):
    B = x.shape[0]
    rows = x.shape[1] // GRID_W
    xc = ctx
    silu_c = jax.nn.silu(c)
    silu_cc = jax.nn.silu(c_ctx)
    for l in range(DEPTH):
        last = l == DEPTH - 1
        mod = (silu_c @ w_ada[l] + b_ada[l]).reshape(B, N_MOD, 1, D_MODEL)
        mod_c = (silu_cc @ w_ada[l] + b_ada[l]).reshape(N_MOD, 1, 1, D_MODEL)
        x = x + 0.5 * mod[:, 2] * swiglu(modulate(x, mod[:, 0], mod[:, 1]),
                                         w_ffn1_gate[l], w_ffn1_up[l], w_ffn1_down[l])
        xc = xc + 0.5 * mod_c[2] * swiglu(modulate(xc, mod_c[0], mod_c[1]),
                                          w_ffn1_gate[l], w_ffn1_up[l], w_ffn1_down[l])
        lam_init = 0.8 - 0.6 * math.exp(-0.3 * l)
        lam = (jnp.exp(jnp.sum(lam_q1[l].astype(jnp.float32) * lam_k1[l].astype(jnp.float32)))
               - jnp.exp(jnp.sum(lam_q2[l].astype(jnp.float32) * lam_k2[l].astype(jnp.float32)))
               + lam_init)
        h = modulate(x, mod[:, 3], mod[:, 4])
        hc = modulate(xc, mod_c[3], mod_c[4])
        y, yc = hybrid_mixer(h, hc, rows, w_in[l], w_out[l], sink_logit[l], q_norm_g[l], k_norm_g[l],
                             rel_pos_bias[l], lam, lam_init, subln_g[l], not last)
        x = x + mod[:, 5] * y
        if not last:
            xc = xc + mod_c[5] * yc
            xc = xc + 0.5 * mod_c[8] * swiglu(modulate(xc, mod_c[6], mod_c[7]),
                                              w_ffn2_gate[l], w_ffn2_up[l], w_ffn2_down[l])
        x = x + 0.5 * mod[:, 8] * swiglu(modulate(x, mod[:, 6], mod[:, 7]),
                                         w_ffn2_gate[l], w_ffn2_up[l], w_ffn2_down[l])
    return rms_norm(x, final_norm_g)
```

```python
import functools
import math

import numpy as np
import jax
import jax.numpy as jnp
from jax import lax
from jax.experimental import pallas as pl
from jax.experimental.pallas import tpu as pltpu

D_MODEL = 1024
GRID_W = 64
HEAD_DIM = 64
N_GROUPS = 4
GROUP_HEADS = D_MODEL // (N_GROUPS * HEAD_DIM)
GROUP_WIDTH = GROUP_HEADS * HEAD_DIM
D_SUB = HEAD_DIM // 2
WINDOW = 128
BLOCK = 128
NA_KR = 8
NA_KC = 16
D_FF = 2816
ROPE_BASE = 10000.0
NORM_EPS = 1e-6
N_MOD = 9
NEG_INF = -1e30
IN_WIDTH = 2560

LANES = 128
MXU_N = 256
TOKEN_TILE = 256
ATTN_TILE = 128
SOFTMAX_ROWS = 16
VMEM_LIMIT = 56 * 1024 * 1024

BF16 = jnp.bfloat16
F32 = jnp.float32

COL_AQ, COL_AK, COL_AV = 0, 2, 3
COL_BQ, COL_BK, COL_BV = 4, 6, 7
COL_CQ, COL_CK, COL_CV = 8, 10, 12
COL_DQ, COL_DK, COL_DV = 14, 16, 18

GQA_HEAD_ORDER = (0, 2, 1, 3)


def _dot(a, b):
    return jnp.dot(a, b, preferred_element_type=F32)


def _dot_nt(a, b):
    return lax.dot_general(a, b, (((1,), (1,)), ((), ())), preferred_element_type=F32)


def _params(semantics):
    return pltpu.CompilerParams(dimension_semantics=semantics, vmem_limit_bytes=VMEM_LIMIT)


def _resident(shape):
    nd = len(shape)
    return pl.BlockSpec(shape, lambda *_: (0,) * nd, pipeline_mode=pl.Buffered(1))


def _mods_kernel(c_ref, w_ref, b_ref, o_ref):
    c = c_ref[...]
    a = c * (1.0 / (1.0 + jnp.exp(-c)))
    o_ref[...] = _dot(a.astype(BF16), w_ref[...].astype(BF16)) + b_ref[...]


def _mods(cc, w_ada, b_ada):
    depth, d, n = w_ada.shape
    rows = cc.shape[0]
    tn = 1152
    return pl.pallas_call(
        _mods_kernel,
        grid=(depth, n // tn),
        in_specs=[
            pl.BlockSpec((rows, d), lambda l, j: (0, 0)),
            pl.BlockSpec((None, d, tn), lambda l, j: (l, 0, j)),
            pl.BlockSpec((None, 1, tn), lambda l, j: (l, 0, j)),
        ],
        out_specs=pl.BlockSpec((None, rows, tn), lambda l, j: (l, 0, j)),
        out_shape=jax.ShapeDtypeStruct((depth, rows, n), F32),
        compiler_params=_params(("arbitrary", "arbitrary")),
        name="adaln_mods",
    )(cc, w_ada, b_ada.reshape(depth, 1, n))


def _modulated(x, mod_ref, k):
    shift = mod_ref[0, k:k + 1, :]
    scale = mod_ref[0, k + 1:k + 2, :]
    ms = jnp.mean(x * x, axis=-1, keepdims=True)
    return (x * lax.rsqrt(ms + NORM_EPS)) * (1.0 + scale) + shift


def _swiglu(h, wg_ref, wu_ref, wd_ref, act_ref):
    hb = h.astype(BF16)
    for j in range(D_FF // MXU_N):
        cols = slice(j * MXU_N, (j + 1) * MXU_N)
        g = _dot(hb, wg_ref[:, cols])
        u = _dot(hb, wu_ref[:, cols])
        act_ref[:, cols] = ((g * (1.0 / (1.0 + jnp.exp(-g)))) * u).astype(BF16)
    return _dot(act_ref[...], wd_ref[...])


def _rope(v, cos, sin_signed, quarter):
    lane = lax.broadcasted_iota(jnp.int32, (1, LANES), 1)
    first = (lane & (2 * quarter - 1)) < quarter
    rot = jnp.where(first, pltpu.roll(v, LANES - quarter, 1), pltpu.roll(v, quarter, 1))
    return v * cos + rot * sin_signed


def _head_rms(v, gain):
    lane = lax.broadcasted_iota(jnp.int32, (1, LANES), 1)
    lo = lane < HEAD_DIM
    sq = v * v
    ms_lo = jnp.sum(jnp.where(lo, sq, 0.0), axis=-1, keepdims=True) * (1.0 / HEAD_DIM)
    ms_hi = jnp.sum(jnp.where(lo, 0.0, sq), axis=-1, keepdims=True) * (1.0 / HEAD_DIM)
    rs = jnp.where(lo, lax.rsqrt(ms_lo + NORM_EPS), lax.rsqrt(ms_hi + NORM_EPS))
    return v * rs * gain


_QK_SCALE = HEAD_DIM ** -0.5
_SUB_SCALE = D_SUB ** -0.5
_PROJ_BLOCKS = (
    (16, -1, _QK_SCALE), (16, -1, _QK_SCALE), (16, -1, 1.0), (0, -1, 1.0),
    (16, 0, _QK_SCALE), (16, 0, _QK_SCALE), (16, 1, 1.0), (0, -1, 1.0),
    (0, -1, _QK_SCALE), (0, -1, _QK_SCALE), (0, -1, 1.0), (0, -1, 1.0),
    (0, -1, 1.0), (0, -1, 1.0),
    (8, -1, _SUB_SCALE), (8, -1, _SUB_SCALE), (8, -1, 1.0), (8, -1, 1.0),
    (0, -1, 1.0), (0, -1, 1.0),
)


def _project(h, w_ref, rope_ref, gain_ref, o_ref):
    hb = h.astype(BF16)
    for j in range(IN_WIDTH // MXU_N):
        r = _dot(hb, w_ref[:, j * MXU_N:(j + 1) * MXU_N])
        for half in range(MXU_N // LANES):
            blk = j * (MXU_N // LANES) + half
            quarter, gain_idx, scale = _PROJ_BLOCKS[blk]
            v = r[:, half * LANES:(half + 1) * LANES]
            if gain_idx >= 0:
                v = _head_rms(v, gain_ref[gain_idx:gain_idx + 1, :])
            if quarter:
                t = 0 if quarter == 16 else 2
                v = _rope(v, rope_ref[t], rope_ref[t + 1], quarter)
            if scale != 1.0:
                v = v * scale
            o_ref[:, blk * LANES:(blk + 1) * LANES] = v.astype(BF16)


def _pre_kernel(x_ref, mod_ref, wg_ref, wu_ref, wd_ref, win_ref, rope_ref, gain_ref,
                xo_ref, qkv_ref, act_ref):
    x = x_ref[...]
    y = _swiglu(_modulated(x, mod_ref, 0), wg_ref, wu_ref, wd_ref, act_ref)
    x = x + (0.5 * mod_ref[0, 2:3, :]) * y
    xo_ref[...] = x
    _project(_modulated(x, mod_ref, 3), win_ref, rope_ref, gain_ref, qkv_ref)


def _post_kernel(n_lat_tiles, final_norm, *refs):
    if n_lat_tiles is None:
        (x_ref, mod_ref, ya_ref, yb_ref, yc_ref, yd_ref, wout_ref, wg_ref, wu_ref, wd_ref,
         fng_ref, xo_ref, y_scr, act_ref) = refs
        y_scr[...] = jnp.concatenate([ya_ref[...], yb_ref[...], yc_ref[...], yd_ref[...]], axis=1)
    else:
        (x_ref, mod_ref, ya_ref, yb_ref, yc_ref, yd_ref, yctx_ref, wout_ref, wg_ref, wu_ref,
         wd_ref, fng_ref, xo_ref, y_scr, act_ref) = refs
        is_latent = pl.program_id(0) < n_lat_tiles

        @pl.when(is_latent)
        def _():
            y_scr[...] = jnp.concatenate(
                [ya_ref[...], yb_ref[...], yc_ref[...], yd_ref[...]], axis=1)

        @pl.when(jnp.logical_not(is_latent))
        def _():
            y_scr[...] = yctx_ref[...]

    x = x_ref[...]
    x = x + mod_ref[0, 5:6, :] * _dot(y_scr[...], wout_ref[...])
    y = _swiglu(_modulated(x, mod_ref, 6), wg_ref, wu_ref, wd_ref, act_ref)
    x = x + (0.5 * mod_ref[0, 8:9, :]) * y
    if final_norm:
        ms = jnp.mean(x * x, axis=-1, keepdims=True)
        x = (x * lax.rsqrt(ms + NORM_EPS)) * fng_ref[...]
    xo_ref[...] = x


def _lane_masks(n, dtype):
    lane = lax.broadcasted_iota(jnp.int32, (1, LANES), 1)
    w = LANES // n
    return [jnp.where((lane >= k * w) & (lane < (k + 1) * w), 1.0, 0.0).astype(dtype)
            for k in range(n)]


def _stack_masked(q, masks):
    return jnp.concatenate([q * m for m in masks], axis=0)


def _softmax_rows(s_ref, p_ref, linv_ref, n_rows, extra_fn=None):
    rc = SOFTMAX_ROWS

    def body(c, carry):
        r0 = pl.multiple_of(c * rc, rc)
        s = s_ref[pl.ds(r0, rc), :]
        m = jnp.max(s, axis=-1, keepdims=True)
        if extra_fn is not None:
            ex = extra_fn(c)
            m = jnp.maximum(m, ex)
        e = jnp.exp(s - m)
        l = jnp.sum(e, axis=-1, keepdims=True)
        if extra_fn is not None:
            l = l + jnp.exp(ex - m)
        p_ref[pl.ds(r0, rc), :] = e.astype(BF16)
        linv_ref[pl.ds(r0, rc), :] = jnp.broadcast_to(1.0 / l, (rc, LANES))
        return carry

    lax.fori_loop(0, n_rows // rc, body, 0)


def _diff_rows(s_ref, p_ref, n_q, lam):
    rc = SOFTMAX_ROWS
    per_head = n_q // rc

    def probs(r0):
        s = s_ref[pl.ds(r0, rc), :]
        e = jnp.exp(s - jnp.max(s, axis=-1, keepdims=True))
        return e, 1.0 / jnp.sum(e, axis=-1, keepdims=True)

    def body(c, carry):
        head = c // per_head
        r = c - head * per_head
        r1 = pl.multiple_of(head * (2 * n_q) + r * rc, rc)
        r2 = pl.multiple_of(head * (2 * n_q) + n_q + r * rc, rc)
        e1, w1 = probs(r1)
        e2, w2 = probs(r2)
        a = e1 * w1 - e2 * (lam * w2)
        p_ref[pl.ds(pl.multiple_of(c * rc, rc), rc), :] = a.astype(BF16)
        return carry

    lax.fori_loop(0, 2 * per_head, body, 0)


def _pair_select(o, n_q):
    lane = lax.broadcasted_iota(jnp.int32, (1, LANES), 1)
    return jnp.where(lane < HEAD_DIM, o[:n_q], o[n_q:2 * n_q])


def _sub_ln(y, gain, post_scale):
    return _head_rms(y, gain) * post_scale


def _lam_value(lamv_ref, lam_init):
    t1 = jnp.sum(lamv_ref[0:1, :] * lamv_ref[1:2, :], axis=-1, keepdims=True)
    t2 = jnp.sum(lamv_ref[2:3, :] * lamv_ref[3:4, :], axis=-1, keepdims=True)
    return jnp.exp(t1) - jnp.exp(t2) + lam_init


def _attn_a_kernel(seq, sink_ref, q_ref, k_ref, v_ref, kc_ref, vc_ref, o_ref,
                   s_ref, p_ref, linv_ref):
    n_q = BLOCK
    span = BLOCK + 2 * WINDOW
    n_ctx = kc_ref.shape[0]
    masks = _lane_masks(2, BF16)
    rows = 4 * n_q
    per_head = n_q // SOFTMAX_ROWS

    def block(n, carry):
        q0 = pl.multiple_of(n * n_q, n_q)
        q = q_ref[pl.ds(q0, n_q), :]
        lhs = jnp.concatenate([_stack_masked(q[:, :LANES], masks),
                               _stack_masked(q[:, LANES:], masks)], axis=0)
        start = pl.multiple_of(jnp.clip(q0 - WINDOW, 0, seq - span), BLOCK)
        s_loc = _dot_nt(lhs, k_ref[pl.ds(start, span), :])
        q_pos = q0 + (lax.broadcasted_iota(jnp.int32, (rows, 1), 0) & (n_q - 1))
        k_pos = start + lax.broadcasted_iota(jnp.int32, (1, span), 1)
        s_ref[:, :span] = jnp.where(jnp.abs(k_pos - q_pos) <= WINDOW, s_loc, NEG_INF)
        s_ref[:, span:] = _dot_nt(lhs, kc_ref[...])
        _softmax_rows(s_ref, p_ref, linv_ref, rows,
                      extra_fn=lambda c: sink_ref[c // per_head])
        o = _dot(p_ref[:, :span], v_ref[pl.ds(start, span), :]) + _dot(p_ref[:, span:], vc_ref[...])
        o = o * linv_ref[...]
        out = jnp.concatenate([_pair_select(o[:2 * n_q], n_q), _pair_select(o[2 * n_q:], n_q)],
                              axis=1)
        o_ref[pl.ds(q0, n_q), :] = out.astype(BF16)
        return carry

    lax.fori_loop(0, seq // n_q, block, 0)


def _attn_b_kernel(q_ref, k_ref, v_ref, kc_ref, vc_ref, o_ref, s_ref, p_ref, linv_ref):
    n_q = q_ref.shape[0]
    n_lat = k_ref.shape[0]
    masks = _lane_masks(2, BF16)
    q = q_ref[...]
    lhs = jnp.concatenate([_stack_masked(q[:, :LANES], masks),
                           _stack_masked(q[:, LANES:], masks)], axis=0)
    s_ref[:, :n_lat] = _dot_nt(lhs, k_ref[...])
    s_ref[:, n_lat:] = _dot_nt(lhs, kc_ref[...])
    _softmax_rows(s_ref, p_ref, linv_ref, 4 * n_q)
    o = _dot(p_ref[:, :n_lat], v_ref[...]) + _dot(p_ref[:, n_lat:], vc_ref[...])
    o = o * linv_ref[...]
    out = jnp.concatenate([_pair_select(o[:2 * n_q], n_q), _pair_select(o[2 * n_q:], n_q)], axis=1)
    o_ref[...] = out.astype(BF16)


def _attn_c_kernel(rows_per_step, n_grid_rows, q_ref, k_ref, v_ref, kc_ref, vc_ref, bias_ref,
                   o_ref, s_ref, p_ref, linv_ref):
    n_q = GRID_W
    n_loc = NA_KR * GRID_W
    masks = _lane_masks(2, BF16)
    row_group = pl.program_id(1)

    def grid_row(i, carry):
        r = row_group * rows_per_step + i
        r_start = jnp.clip(r - NA_KR // 2, 0, n_grid_rows - NA_KR)
        off = r - r_start
        q0 = pl.multiple_of(i * n_q, n_q)
        k0 = pl.multiple_of(r_start * GRID_W, GRID_W)
        q = q_ref[pl.ds(q0, n_q), :]
        for jb in range(2):
            cols = slice(jb * LANES, (jb + 1) * LANES)
            rws = slice(jb * 2 * n_q, (jb + 1) * 2 * n_q)
            lhs = _stack_masked(q[:, cols], masks)
            bias = jnp.concatenate([bias_ref[2 * jb, off], bias_ref[2 * jb + 1, off]], axis=0)
            s_ref[rws, :n_loc] = _dot_nt(lhs, k_ref[pl.ds(k0, n_loc), cols]) + bias
            s_ref[rws, n_loc:] = _dot_nt(lhs, kc_ref[:, cols])
        _softmax_rows(s_ref, p_ref, linv_ref, 4 * n_q)
        outs = []
        for jb in range(2):
            cols = slice(jb * LANES, (jb + 1) * LANES)
            rws = slice(jb * 2 * n_q, (jb + 1) * 2 * n_q)
            o = (_dot(p_ref[rws, :n_loc], v_ref[pl.ds(k0, n_loc), cols])
                 + _dot(p_ref[rws, n_loc:], vc_ref[:, cols]))
            outs.append(_pair_select(o * linv_ref[rws, :], n_q))
        o_ref[pl.ds(q0, n_q), :] = jnp.concatenate(outs, axis=1).astype(BF16)
        return carry

    lax.fori_loop(0, rows_per_step, grid_row, 0)


def _attn_d_kernel(lam_init, lamv_ref, gain_ref, q_ref, k_ref, v_ref, kc_ref, vc_ref, o_ref,
                   s_ref, p_ref):
    n_q = q_ref.shape[0]
    n_lat = k_ref.shape[0]
    masks = _lane_masks(4, BF16)
    lam = _lam_value(lamv_ref, lam_init)
    q = q_ref[...]
    outs = []
    for jb in range(2):
        cols = slice(jb * LANES, (jb + 1) * LANES)
        lhs = _stack_masked(q[:, cols], masks)
        s_ref[:, :n_lat] = _dot_nt(lhs, k_ref[:, cols])
        s_ref[:, n_lat:] = _dot_nt(lhs, kc_ref[:, cols])
        _diff_rows(s_ref, p_ref, n_q, lam)
        o = _dot(p_ref[:, :n_lat], v_ref[:, cols]) + _dot(p_ref[:, n_lat:], vc_ref[:, cols])
        outs.append(_sub_ln(_pair_select(o, n_q), gain_ref[...], 1.0 - lam_init))
    o_ref[...] = jnp.concatenate(outs, axis=1).astype(BF16)


def _attn_ctx_kernel(lam_init, sink_ref, lamv_ref, gain_ref, aq_ref, ak_ref, av_ref, bq_ref, bk_ref,
                     bv_ref, cq_ref, ck_ref, cv_ref, dq_ref, dk_ref, dv_ref, o_ref,
                     s_ref, p_ref, linv_ref):
    n_q = aq_ref.shape[0]
    m2 = _lane_masks(2, BF16)
    m4 = _lane_masks(4, BF16)
    per_head = n_q // SOFTMAX_ROWS

    def gqa(q_ref_, k_ref_, v_ref_, extra_fn):
        q = q_ref_[...]
        lhs = jnp.concatenate([_stack_masked(q[:, :LANES], m2), _stack_masked(q[:, LANES:], m2)],
                              axis=0)
        s_ref[...] = _dot_nt(lhs, k_ref_[...])
        _softmax_rows(s_ref, p_ref, linv_ref, 4 * n_q, extra_fn=extra_fn)
        o = _dot(p_ref[...], v_ref_[...]) * linv_ref[...]
        return jnp.concatenate([_pair_select(o[:2 * n_q], n_q), _pair_select(o[2 * n_q:], n_q)],
                               axis=1)

    o_ref[:, 0:GROUP_WIDTH] = gqa(aq_ref, ak_ref, av_ref,
                                  lambda c: sink_ref[c // per_head]).astype(BF16)
    o_ref[:, GROUP_WIDTH:2 * GROUP_WIDTH] = gqa(bq_ref, bk_ref, bv_ref, None).astype(BF16)

    q = cq_ref[...]
    for jb in range(2):
        cols = slice(jb * LANES, (jb + 1) * LANES)
        rws = slice(jb * 2 * n_q, (jb + 1) * 2 * n_q)
        s_ref[rws, :] = _dot_nt(_stack_masked(q[:, cols], m2), ck_ref[:, cols])
    _softmax_rows(s_ref, p_ref, linv_ref, 4 * n_q)
    for jb in range(2):
        cols = slice(jb * LANES, (jb + 1) * LANES)
        rws = slice(jb * 2 * n_q, (jb + 1) * 2 * n_q)
        o = _dot(p_ref[rws, :], cv_ref[:, cols]) * linv_ref[rws, :]
        o_ref[:, 2 * GROUP_WIDTH + jb * LANES:2 * GROUP_WIDTH + (jb + 1) * LANES] = (
            _pair_select(o, n_q).astype(BF16))

    lam = _lam_value(lamv_ref, lam_init)
    q = dq_ref[...]
    for jb in range(2):
        cols = slice(jb * LANES, (jb + 1) * LANES)
        s_ref[...] = _dot_nt(_stack_masked(q[:, cols], m4), dk_ref[:, cols])
        _diff_rows(s_ref, p_ref, n_q, lam)
        o = _dot(p_ref[0:2 * n_q, :], dv_ref[:, cols])
        y = _sub_ln(_pair_select(o, n_q), gain_ref[...], 1.0 - lam_init)
        o_ref[:, 3 * GROUP_WIDTH + jb * LANES:3 * GROUP_WIDTH + (jb + 1) * LANES] = y.astype(BF16)


def _rope_tables(seq, n_extra):
    t = jnp.arange(seq, dtype=jnp.int32)
    row = (t // GRID_W).astype(F32)
    col = (t % GRID_W).astype(F32)
    tables = []
    for dim in (HEAD_DIM, D_SUB):
        half = dim // 2
        freqs = ROPE_BASE ** (-jnp.arange(0, half, 2, dtype=F32) / half)
        ang_r = row[:, None] * freqs[None, :]
        ang_c = col[:, None] * freqs[None, :]
        ang = jnp.concatenate([ang_r, ang_r, ang_c, ang_c], axis=-1)
        quarter = dim // 4
        sign = jnp.where((jnp.arange(dim) % (2 * quarter)) < quarter, -1.0, 1.0).astype(F32)
        cos = jnp.tile(jnp.cos(ang), (1, LANES // dim))
        sin = jnp.tile(jnp.sin(ang) * sign[None, :], (1, LANES // dim))
        cos = jnp.concatenate([cos, jnp.ones((n_extra, LANES), F32)], axis=0)
        sin = jnp.concatenate([sin, jnp.zeros((n_extra, LANES), F32)], axis=0)
        tables += [cos, sin]
    return jnp.stack(tables)


def _neighbourhood_bias(rel_bias):
    kr = NA_KR
    col = np.arange(GRID_W)
    c_start = np.clip(col - NA_KC // 2, 0, GRID_W - NA_KC)
    col_ok = (col[None, :] >= c_start[:, None]) & (col[None, :] < c_start[:, None] + NA_KC)
    dc_idx = np.clip(col[None, :] - col[:, None] + NA_KC - 1, 0, 2 * NA_KC - 2)
    off = np.arange(kr)
    dr_idx = np.arange(kr)[None, :] - off[:, None] + NA_KR - 1
    bias = rel_bias[:, dr_idx[:, None, :, None], dc_idx[None, :, None, :]]
    bias = jnp.where(col_ok[None, None, :, None, :], bias.astype(F32), NEG_INF)
    return bias.reshape(rel_bias.shape[0], kr, GRID_W, kr * GRID_W)


def _permute_q_heads(w, col0):
    blocks = [w[:, col0 + h * HEAD_DIM:col0 + (h + 1) * HEAD_DIM] for h in GQA_HEAD_ORDER]
    return jnp.concatenate([w[:, :col0]] + blocks + [w[:, col0 + GROUP_WIDTH:]], axis=1)


def _permute_out_heads(w, row0):
    blocks = [w[row0 + h * HEAD_DIM:row0 + (h + 1) * HEAD_DIM] for h in GQA_HEAD_ORDER]
    return jnp.concatenate([w[:row0]] + blocks + [w[row0 + GROUP_WIDTH:]], axis=0)


def kernel(x, c, ctx, c_ctx, w_ada, b_ada, w_ffn1_gate, w_ffn1_up, w_ffn1_down, w_in, w_out,
           sink_logit, q_norm_g, k_norm_g, rel_pos_bias, lam_q1, lam_k1, lam_q2, lam_k2, subln_g,
           w_ffn2_gate, w_ffn2_up, w_ffn2_down, final_norm_g):
    batch, seq, d = x.shape
    n_ctx = ctx.shape[1]
    depth = w_ada.shape[0]
    assert d == D_MODEL and seq % TOKEN_TILE == 0 and n_ctx == TOKEN_TILE
    assert seq % GRID_W == 0 and w_in.shape[-1] == IN_WIDTH
    t_lat, t_ctx = batch * seq, batch * n_ctx
    t_all = t_lat + t_ctx
    tm = TOKEN_TILE
    n_lat_tiles, n_ctx_tiles = t_lat // tm, t_ctx // tm
    tiles_per_seq = seq // tm
    grid_rows = seq // GRID_W

    cc = jnp.concatenate([c, c_ctx[None, :], jnp.zeros((16 - batch - 1, d), F32)], axis=0)
    mods = _mods(cc, w_ada, b_ada).reshape(depth, 16, N_MOD, d)

    def group(i):
        return jnp.minimum(i // tiles_per_seq, batch)

    rope = _rope_tables(seq, tm)

    def rope_block(i):
        return jnp.where(i < n_lat_tiles, i % tiles_per_seq, tiles_per_seq)

    lane_gain = lambda g: jnp.tile(g.astype(F32), LANES // HEAD_DIM)[None, :]

    xs = jnp.concatenate([x.reshape(t_lat, d), ctx.reshape(t_ctx, d)], axis=0)

    tile_spec = pl.BlockSpec((tm, d), lambda i: (i, 0))

    for l in range(depth):
        last = l == depth - 1
        lam_init = 0.8 - 0.6 * math.exp(-0.3 * l)
        mod_spec = pl.BlockSpec((1, N_MOD, d), lambda i: (group(i), 0, 0))

        w_in_l = _permute_q_heads(_permute_q_heads(w_in[l], COL_AQ * LANES), COL_BQ * LANES)
        gains = jnp.concatenate([lane_gain(q_norm_g[l]), lane_gain(k_norm_g[l]),
                                 jnp.zeros((6, LANES), F32)], axis=0)
        xs, qkv = pl.pallas_call(
            _pre_kernel,
            grid=(n_lat_tiles + n_ctx_tiles,),
            in_specs=[
                tile_spec, mod_spec,
                _resident((d, D_FF)), _resident((d, D_FF)), _resident((D_FF, d)),
                _resident((d, IN_WIDTH)),
                pl.BlockSpec((4, tm, LANES), lambda i: (0, rope_block(i), 0)),
                _resident((8, LANES)),
            ],
            out_specs=[tile_spec, pl.BlockSpec((tm, IN_WIDTH), lambda i: (i, 0))],
            out_shape=[jax.ShapeDtypeStruct((t_all, d), F32),
                       jax.ShapeDtypeStruct((t_all, IN_WIDTH), BF16)],
            scratch_shapes=[pltpu.VMEM((tm, D_FF), BF16)],
            compiler_params=_params(("arbitrary",)),
            name=f"pre_l{l}",
        )(xs, mods[l], w_ffn1_gate[l].astype(BF16), w_ffn1_up[l].astype(BF16),
          w_ffn1_down[l].astype(BF16), w_in_l.astype(BF16), rope, gains)

        ctx_row = t_lat // n_ctx
        sink_perm = sink_logit[l].astype(F32)[jnp.array(GQA_HEAD_ORDER)]
        lamv = jnp.zeros((8, LANES), F32)
        lamv = lamv.at[0, :D_SUB].set(lam_q1[l]).at[1, :D_SUB].set(lam_k1[l])
        lamv = lamv.at[2, :D_SUB].set(lam_q2[l]).at[3, :D_SUB].set(lam_k2[l])
        subln = lane_gain(subln_g[l])
        smem = pl.BlockSpec(memory_space=pltpu.SMEM)

        def lat(cols, col):
            return pl.BlockSpec((seq, cols), lambda b, *_: (b, col))

        def ctxb(cols, col):
            return pl.BlockSpec((n_ctx, cols), lambda b, *_: (ctx_row + b, col))

        y_a = pl.pallas_call(
            functools.partial(_attn_a_kernel, seq),
            grid=(batch,),
            in_specs=[smem, lat(2 * LANES, COL_AQ // 2), lat(LANES, COL_AK), lat(LANES, COL_AV),
                      ctxb(LANES, COL_AK), ctxb(LANES, COL_AV)],
            out_specs=pl.BlockSpec((seq, GROUP_WIDTH), lambda b: (b, 0)),
            out_shape=jax.ShapeDtypeStruct((t_lat, GROUP_WIDTH), BF16),
            scratch_shapes=[pltpu.VMEM((4 * BLOCK, BLOCK + 2 * WINDOW + n_ctx), F32),
                            pltpu.VMEM((4 * BLOCK, BLOCK + 2 * WINDOW + n_ctx), BF16),
                            pltpu.VMEM((4 * BLOCK, LANES), F32)],
            compiler_params=_params(("arbitrary",)),
            name=f"attn_a_l{l}",
        )(sink_perm, qkv, qkv, qkv, qkv, qkv)

        tq = ATTN_TILE
        q_tiles = seq // tq
        y_b = pl.pallas_call(
            _attn_b_kernel,
            grid=(batch, q_tiles),
            in_specs=[pl.BlockSpec((tq, 2 * LANES), lambda b, i: (b * q_tiles + i, COL_BQ // 2)),
                      lat(LANES, COL_BK), lat(LANES, COL_BV),
                      ctxb(LANES, COL_BK), ctxb(LANES, COL_BV)],
            out_specs=pl.BlockSpec((tq, GROUP_WIDTH), lambda b, i: (b * q_tiles + i, 0)),
            out_shape=jax.ShapeDtypeStruct((t_lat, GROUP_WIDTH), BF16),
            scratch_shapes=[pltpu.VMEM((4 * tq, seq + n_ctx), F32),
                            pltpu.VMEM((4 * tq, seq + n_ctx), BF16),
                            pltpu.VMEM((4 * tq, LANES), F32)],
            compiler_params=_params(("arbitrary", "arbitrary")),
            name=f"attn_b_l{l}",
        )(qkv, qkv, qkv, qkv, qkv)

        rows_per_step = 8
        tc = rows_per_step * GRID_W
        c_tiles = seq // tc
        bias = _neighbourhood_bias(rel_pos_bias[l])
        y_c = pl.pallas_call(
            functools.partial(_attn_c_kernel, rows_per_step, grid_rows),
            grid=(batch, c_tiles),
            in_specs=[pl.BlockSpec((tc, 2 * LANES), lambda b, i: (b * c_tiles + i, COL_CQ // 2)),
                      lat(2 * LANES, COL_CK // 2), lat(2 * LANES, COL_CV // 2),
                      ctxb(2 * LANES, COL_CK // 2), ctxb(2 * LANES, COL_CV // 2),
                      _resident(bias.shape)],
            out_specs=pl.BlockSpec((tc, GROUP_WIDTH), lambda b, i: (b * c_tiles + i, 0)),
            out_shape=jax.ShapeDtypeStruct((t_lat, GROUP_WIDTH), BF16),
            scratch_shapes=[pltpu.VMEM((4 * GRID_W, NA_KR * GRID_W + n_ctx), F32),
                            pltpu.VMEM((4 * GRID_W, NA_KR * GRID_W + n_ctx), BF16),
                            pltpu.VMEM((4 * GRID_W, LANES), F32)],
            compiler_params=_params(("arbitrary", "arbitrary")),
            name=f"attn_c_l{l}",
        )(qkv, qkv, qkv, qkv, qkv, bias)

        y_d = pl.pallas_call(
            functools.partial(_attn_d_kernel, lam_init),
            grid=(batch, q_tiles),
            in_specs=[_resident((8, LANES)), _resident((1, LANES)),
                      pl.BlockSpec((tq, 2 * LANES), lambda b, i: (b * q_tiles + i, COL_DQ // 2)),
                      lat(2 * LANES, COL_DK // 2), lat(2 * LANES, COL_DV // 2),
                      ctxb(2 * LANES, COL_DK // 2), ctxb(2 * LANES, COL_DV // 2)],
            out_specs=pl.BlockSpec((tq, GROUP_WIDTH), lambda b, i: (b * q_tiles + i, 0)),
            out_shape=jax.ShapeDtypeStruct((t_lat, GROUP_WIDTH), BF16),
            scratch_shapes=[pltpu.VMEM((4 * tq, seq + n_ctx), F32),
                            pltpu.VMEM((2 * tq, seq + n_ctx), BF16)],
            compiler_params=_params(("arbitrary", "arbitrary")),
            name=f"attn_d_l{l}",
        )(lamv, subln, qkv, qkv, qkv, qkv, qkv)

        w_out_l = _permute_out_heads(_permute_out_heads(w_out[l], 0), GROUP_WIDTH).astype(BF16)
        ffn2 = (w_ffn2_gate[l].astype(BF16), w_ffn2_up[l].astype(BF16), w_ffn2_down[l].astype(BF16))
        ffn2_specs = [_resident((d, D_FF)), _resident((d, D_FF)), _resident((D_FF, d))]
        fng = final_norm_g.astype(F32)[None, :]
        y_spec = pl.BlockSpec((tm, GROUP_WIDTH), lambda i: (jnp.minimum(i, n_lat_tiles - 1), 0))
        post_scratch = [pltpu.VMEM((tm, d), BF16), pltpu.VMEM((tm, D_FF), BF16)]

        if not last:
            y_ctx = pl.pallas_call(
                functools.partial(_attn_ctx_kernel, lam_init),
                grid=(batch,),
                in_specs=[smem, _resident((8, LANES)), _resident((1, LANES)),
                          ctxb(2 * LANES, COL_AQ // 2), ctxb(LANES, COL_AK), ctxb(LANES, COL_AV),
                          ctxb(2 * LANES, COL_BQ // 2), ctxb(LANES, COL_BK), ctxb(LANES, COL_BV),
                          ctxb(2 * LANES, COL_CQ // 2), ctxb(2 * LANES, COL_CK // 2),
                          ctxb(2 * LANES, COL_CV // 2),
                          ctxb(2 * LANES, COL_DQ // 2), ctxb(2 * LANES, COL_DK // 2),
                          ctxb(2 * LANES, COL_DV // 2)],
                out_specs=pl.BlockSpec((n_ctx, d), lambda b: (b, 0)),
                out_shape=jax.ShapeDtypeStruct((t_ctx, d), BF16),
                scratch_shapes=[pltpu.VMEM((4 * n_ctx, n_ctx), F32),
                                pltpu.VMEM((4 * n_ctx, n_ctx), BF16),
                                pltpu.VMEM((4 * n_ctx, LANES), F32)],
                compiler_params=_params(("arbitrary",)),
                name=f"attn_ctx_l{l}",
            )(sink_perm, lamv, subln, *([qkv] * 12))

            xs = pl.pallas_call(
                functools.partial(_post_kernel, n_lat_tiles, False),
                grid=(n_lat_tiles + n_ctx_tiles,),
                in_specs=[tile_spec, mod_spec, y_spec, y_spec, y_spec, y_spec,
                          pl.BlockSpec((tm, d), lambda i: (jnp.maximum(i - n_lat_tiles, 0), 0)),
                          _resident((d, d))] + ffn2_specs + [_resident((1, d))],
                out_specs=tile_spec,
                out_shape=jax.ShapeDtypeStruct((t_all, d), F32),
                scratch_shapes=post_scratch,
                compiler_params=_params(("arbitrary",)),
                name=f"post_l{l}",
            )(xs, mods[l], y_a, y_b, y_c, y_d, y_ctx, w_out_l, *ffn2, fng)
        else:
            xs = pl.pallas_call(
                functools.partial(_post_kernel, None, True),
                grid=(n_lat_tiles,),
                in_specs=[tile_spec, mod_spec, y_spec, y_spec, y_spec, y_spec,
                          _resident((d, d))] + ffn2_specs + [_resident((1, d))],
                out_specs=tile_spec,
                out_shape=jax.ShapeDtypeStruct((t_lat, d), F32),
                scratch_shapes=post_scratch,
                compiler_params=_params(("arbitrary",)),
                name=f"post_l{l}",
            )(xs, mods[l], y_a, y_b, y_c, y_d, w_out_l, *ffn2, fng)

    return xs.reshape(batch, seq, d)
```

```python
import functools
import math

import jax
import jax.numpy as jnp
from jax import lax
from jax.experimental import pallas as pl
from jax.experimental.pallas import tpu as pltpu

D_MODEL = 1024
GRID_W = 64
HEAD_DIM = 64
N_GROUPS = 4
GROUP_HEADS = D_MODEL // (N_GROUPS * HEAD_DIM)
GROUP_WIDTH = GROUP_HEADS * HEAD_DIM
D_SUB = HEAD_DIM // 2
WINDOW = 128
BLOCK = 128
NA_KR = 8
NA_KC = 16
D_FF = 2816
ROPE_BASE = 10000.0
NORM_EPS = 1e-6
N_MOD = 9
NEG_INF = -1e30
IN_WIDTH = 2560

LANES = 128
MXU_N = 256
TOKEN_TILE = 256
ATTN_TILE = 128
ROW_BLOCK = 32
VMEM_LIMIT = 56 * 1024 * 1024
LOG2E = 1.4426950408889634

BF16 = jnp.bfloat16
F32 = jnp.float32

COL_AQ, COL_AK, COL_AV = 0, 2, 3
COL_BQ, COL_BK, COL_BV = 4, 6, 7
COL_CQ, COL_CK, COL_CV = 8, 10, 12
COL_DQ, COL_DK, COL_DV = 14, 16, 18

GQA_HEAD_ORDER = (0, 2, 1, 3)


def _dot(a, b):
    return jnp.dot(a, b, preferred_element_type=F32)


def _dot_nt(a, b):
    return lax.dot_general(a, b, (((1,), (1,)), ((), ())), preferred_element_type=F32)


def _params(semantics):
    return pltpu.CompilerParams(dimension_semantics=semantics, vmem_limit_bytes=VMEM_LIMIT)


def _resident(shape):
    nd = len(shape)
    return pl.BlockSpec(shape, lambda *_: (0,) * nd, pipeline_mode=pl.Buffered(1))


def _mods_kernel(c_ref, w_ref, b_ref, o_ref):
    c = c_ref[...]
    a = c * (1.0 / (1.0 + jnp.exp(-c)))
    o_ref[...] = _dot(a.astype(BF16), w_ref[...].astype(BF16)) + b_ref[...]


def _mods(cc, w_ada, b_ada):
    depth, d, n = w_ada.shape
    rows = cc.shape[0]
    tn = 1152
    return pl.pallas_call(
        _mods_kernel,
        grid=(depth, n // tn),
        in_specs=[
            pl.BlockSpec((rows, d), lambda l, j: (0, 0)),
            pl.BlockSpec((None, d, tn), lambda l, j: (l, 0, j)),
            pl.BlockSpec((None, 1, tn), lambda l, j: (l, 0, j)),
        ],
        out_specs=pl.BlockSpec((None, rows, tn), lambda l, j: (l, 0, j)),
        out_shape=jax.ShapeDtypeStruct((depth, rows, n), F32),
        compiler_params=_params(("arbitrary", "arbitrary")),
        name="adaln_mods",
    )(cc, w_ada, b_ada.reshape(depth, 1, n))


def _modulated(x, mod_ref, k):
    shift = mod_ref[0, k:k + 1, :]
    scale = mod_ref[0, k + 1:k + 2, :]
    ms = jnp.mean(x * x, axis=-1, keepdims=True)
    return (x * lax.rsqrt(ms + NORM_EPS)) * (1.0 + scale) + shift


def _swiglu(h, wg_ref, wu_ref, wd_ref, act_ref):
    hb = h.astype(BF16)
    for j in range(D_FF // MXU_N):
        cols = slice(j * MXU_N, (j + 1) * MXU_N)
        g = _dot(hb, wg_ref[:, cols])
        u = _dot(hb, wu_ref[:, cols])
        act_ref[:, cols] = ((g * (1.0 / (1.0 + jnp.exp(-g)))) * u).astype(BF16)
    return _dot(act_ref[...], wd_ref[...])


def _rope(v, cos, sin_signed, quarter):
    lane = lax.broadcasted_iota(jnp.int32, (1, LANES), 1)
    first = (lane & (2 * quarter - 1)) < quarter
    rot = jnp.where(first, pltpu.roll(v, LANES - quarter, 1), pltpu.roll(v, quarter, 1))
    return v * cos + rot * sin_signed


def _head_rms(v, gain):
    lane = lax.broadcasted_iota(jnp.int32, (1, LANES), 1)
    lo = lane < HEAD_DIM
    sq = v * v
    ms_lo = jnp.sum(jnp.where(lo, sq, 0.0), axis=-1, keepdims=True) * (1.0 / HEAD_DIM)
    ms_hi = jnp.sum(jnp.where(lo, 0.0, sq), axis=-1, keepdims=True) * (1.0 / HEAD_DIM)
    rs = jnp.where(lo, lax.rsqrt(ms_lo + NORM_EPS), lax.rsqrt(ms_hi + NORM_EPS))
    return v * rs * gain


_QK_SCALE = HEAD_DIM ** -0.5 * LOG2E
_SUB_SCALE = D_SUB ** -0.5 * LOG2E
_PROJ_BLOCKS = (
    (16, -1, _QK_SCALE), (16, -1, _QK_SCALE), (16, -1, 1.0), (0, -1, 1.0),
    (16, 0, _QK_SCALE), (16, 0, _QK_SCALE), (16, 1, 1.0), (0, -1, 1.0),
    (0, -1, _QK_SCALE), (0, -1, _QK_SCALE), (0, -1, 1.0), (0, -1, 1.0),
    (0, -1, 1.0), (0, -1, 1.0),
    (8, -1, _SUB_SCALE), (8, -1, _SUB_SCALE), (8, -1, 1.0), (8, -1, 1.0),
    (0, -1, 1.0), (0, -1, 1.0),
)


def _project(h, w_ref, rope_ref, gain_ref, o_ref):
    hb = h.astype(BF16)
    for j in range(IN_WIDTH // MXU_N):
        r = _dot(hb, w_ref[:, j * MXU_N:(j + 1) * MXU_N])
        for half in range(MXU_N // LANES):
            blk = j * (MXU_N // LANES) + half
            quarter, gain_idx, scale = _PROJ_BLOCKS[blk]
            v = r[:, half * LANES:(half + 1) * LANES]
            if gain_idx >= 0:
                v = _head_rms(v, gain_ref[gain_idx:gain_idx + 1, :])
            if quarter:
                t = 0 if quarter == 16 else 2
                v = _rope(v, rope_ref[t], rope_ref[t + 1], quarter)
            if scale != 1.0:
                v = v * scale
            o_ref[:, blk * LANES:(blk + 1) * LANES] = v.astype(BF16)


def _pre_kernel(x_ref, mod_ref, wg_ref, wu_ref, wd_ref, win_ref, rope_ref, gain_ref,
                xo_ref, qkv_ref, act_ref):
    x = x_ref[...]
    y = _swiglu(_modulated(x, mod_ref, 0), wg_ref, wu_ref, wd_ref, act_ref)
    x = x + (0.5 * mod_ref[0, 2:3, :]) * y
    xo_ref[...] = x
    _project(_modulated(x, mod_ref, 3), win_ref, rope_ref, gain_ref, qkv_ref)


def _post_kernel(n_lat_tiles, final_norm, *refs):
    if n_lat_tiles is None:
        (x_ref, mod_ref, ya_ref, yb_ref, yc_ref, yd_ref, wout_ref, wg_ref, wu_ref, wd_ref,
         fng_ref, xo_ref, y_scr, act_ref) = refs
        y_scr[...] = jnp.concatenate([ya_ref[...], yb_ref[...], yc_ref[...], yd_ref[...]], axis=1)
    else:
        (x_ref, mod_ref, ya_ref, yb_ref, yc_ref, yd_ref, yctx_ref, wout_ref, wg_ref, wu_ref,
         wd_ref, fng_ref, xo_ref, y_scr, act_ref) = refs
        is_latent = pl.program_id(0) < n_lat_tiles

        @pl.when(is_latent)
        def _():
            y_scr[...] = jnp.concatenate(
                [ya_ref[...], yb_ref[...], yc_ref[...], yd_ref[...]], axis=1)

        @pl.when(jnp.logical_not(is_latent))
        def _():
            y_scr[...] = yctx_ref[...]

    x = x_ref[...]
    x = x + mod_ref[0, 5:6, :] * _dot(y_scr[...], wout_ref[...])
    y = _swiglu(_modulated(x, mod_ref, 6), wg_ref, wu_ref, wd_ref, act_ref)
    x = x + (0.5 * mod_ref[0, 8:9, :]) * y
    if final_norm:
        ms = jnp.mean(x * x, axis=-1, keepdims=True)
        x = (x * lax.rsqrt(ms + NORM_EPS)) * fng_ref[...]
    xo_ref[...] = x


def _lane_masks(n, dtype):
    lane = lax.broadcasted_iota(jnp.int32, (1, LANES), 1)
    w = LANES // n
    return [jnp.where((lane >= k * w) & (lane < (k + 1) * w), 1.0, 0.0).astype(dtype)
            for k in range(n)]


def _stack_masked(q, masks):
    return jnp.concatenate([q * m for m in masks], axis=0)


def _fill_ext(ext_ref, v):
    ext_ref[:, :LANES] = v
    ext_ref[:, LANES:] = jnp.ones(v.shape, v.dtype)


def _attend(lhs, k_list, v_list, s_ref, p_ref, score_fn=None, floor=None):
    n_rows = lhs.shape[0]
    off = 0
    for idx, k in enumerate(k_list):
        s = _dot_nt(lhs, k)
        if idx == 0 and score_fn is not None:
            s = score_fn(s)
        s_ref[0:n_rows, off:off + k.shape[0]] = s
        off += k.shape[0]
    maxima = []
    for r in range(0, n_rows, ROW_BLOCK):
        rows = slice(r, r + ROW_BLOCK)
        m = jnp.max(s_ref[rows, 0:off], axis=-1, keepdims=True)
        if floor is not None:
            m = jnp.maximum(m, floor[rows])
        p_ref[rows, 0:off] = jnp.exp2(s_ref[rows, 0:off] - m).astype(BF16)
        maxima.append(m)
    o = None
    off = 0
    for v in v_list:
        part = _dot(p_ref[0:n_rows, off:off + v.shape[0]], v)
        o = part if o is None else o + part
        off += v.shape[0]
    return o, jnp.concatenate(maxima, axis=0)


def _pair_select(o, n_q):
    lane = lax.broadcasted_iota(jnp.int32, (1, LANES), 1)
    return jnp.where(lane < HEAD_DIM, o[:n_q], o[n_q:2 * n_q])


def _sub_ln(y, gain, post_scale):
    return _head_rms(y, gain) * post_scale


def _lam_value(lamv_ref, lam_init):
    t1 = jnp.sum(lamv_ref[0:1, :] * lamv_ref[1:2, :], axis=-1, keepdims=True)
    t2 = jnp.sum(lamv_ref[2:3, :] * lamv_ref[3:4, :], axis=-1, keepdims=True)
    return jnp.exp(t1) - jnp.exp(t2) + lam_init


def _sink_rows(sink_ref, pair, n_q):
    row = lax.broadcasted_iota(jnp.int32, (2 * n_q, 1), 0)
    return jnp.where(row < n_q, sink_ref[2 * pair], sink_ref[2 * pair + 1]) * LOG2E


def _diff_combine(o, n_q, lam):
    r = o[:, :LANES] / o[:, LANES:]
    return r[:n_q] - lam * r[n_q:]


def _bias_kernel(rel_ref, o_ref):
    h = pl.program_id(0)
    n_dr, n_dc = 2 * NA_KR - 1, 2 * NA_KC - 1
    q_col = lax.broadcasted_iota(jnp.int32, (GRID_W, LANES), 0)
    lane = lax.broadcasted_iota(jnp.int32, (GRID_W, LANES), 1)
    k_col = lane & (GRID_W - 1)
    dc = k_col - q_col + (NA_KC - 1)
    c_start = jnp.clip(q_col - NA_KC // 2, 0, GRID_W - NA_KC)
    in_window = (k_col >= c_start) & (k_col < c_start + NA_KC)
    tiles = []
    for dr in range(n_dr):
        t = jnp.zeros((GRID_W, LANES), F32)
        for d in range(n_dc):
            t = jnp.where(dc == d, rel_ref[(h * n_dr + dr) * n_dc + d], t)
        tiles.append(jnp.where(in_window, t * LOG2E, NEG_INF))
    low = lane < GRID_W
    for off in range(NA_KR):
        for c in range(NA_KR * GRID_W // LANES):
            dr = 2 * c - off + NA_KR - 1
            o_ref[0, off, :, c * LANES:(c + 1) * LANES] = jnp.where(low, tiles[dr], tiles[dr + 1])


def _attn_a_kernel(seq, sink_ref, q_ref, k_ref, v_ref, kc_ref, vc_ref, o_ref,
                   vl_ext, vc_ext, s0, p0, s1, p1):
    n_q = BLOCK
    span = BLOCK + 2 * WINDOW
    masks = _lane_masks(2, BF16)
    _fill_ext(vl_ext, v_ref[...])
    _fill_ext(vc_ext, vc_ref[...])

    def block(n, carry):
        q0 = pl.multiple_of(n * n_q, n_q)
        q = q_ref[pl.ds(q0, n_q), :]
        start = pl.multiple_of(jnp.clip(q0 - WINDOW, 0, seq - span), BLOCK)
        k_loc = k_ref[pl.ds(start, span), :]
        v_loc = vl_ext[pl.ds(start, span), :]
        q_pos = q0 + (lax.broadcasted_iota(jnp.int32, (2 * n_q, 1), 0) & (n_q - 1))
        k_pos = start + lax.broadcasted_iota(jnp.int32, (1, span), 1)
        valid = jnp.abs(k_pos - q_pos) <= WINDOW
        outs = []
        for pair, (s_ref, p_ref) in enumerate(((s0, p0), (s1, p1))):
            lhs = _stack_masked(q[:, pair * LANES:(pair + 1) * LANES], masks)
            sink = _sink_rows(sink_ref, pair, n_q)
            o, m = _attend(lhs, [k_loc, kc_ref[...]], [v_loc, vc_ext[...]], s_ref, p_ref,
                           score_fn=lambda s: jnp.where(valid, s, NEG_INF), floor=sink)
            den = o[:, LANES:] + jnp.exp2(sink - m)
            outs.append(_pair_select(o[:, :LANES] / den, n_q))
        o_ref[pl.ds(q0, n_q), :] = jnp.concatenate(outs, axis=1).astype(BF16)
        return carry

    lax.fori_loop(0, seq // n_q, block, 0)


def _attn_b_kernel(q_ref, k_ref, v_ref, kc_ref, vc_ref, o_ref, vl_ext, vc_ext, s0, p0, s1, p1):
    n_q = q_ref.shape[0]

    @pl.when(pl.program_id(1) == 0)
    def _():
        _fill_ext(vl_ext, v_ref[...])
        _fill_ext(vc_ext, vc_ref[...])

    masks = _lane_masks(2, BF16)
    q = q_ref[...]
    outs = []
    for pair, (s_ref, p_ref) in enumerate(((s0, p0), (s1, p1))):
        lhs = _stack_masked(q[:, pair * LANES:(pair + 1) * LANES], masks)
        o, _ = _attend(lhs, [k_ref[...], kc_ref[...]], [vl_ext[...], vc_ext[...]], s_ref, p_ref)
        outs.append(_pair_select(o[:, :LANES] / o[:, LANES:], n_q))
    o_ref[...] = jnp.concatenate(outs, axis=1).astype(BF16)


def _attn_c_kernel(rows_per_step, n_grid_rows, q_ref, k_ref, v_ref, kc_ref, vc_ref, bias_ref,
                   o_ref, vl_ext, vc_ext, s0, p0, s1, p1):
    n_q = GRID_W
    n_loc = NA_KR * GRID_W
    masks = _lane_masks(2, BF16)
    row_group = pl.program_id(1)

    @pl.when(row_group == 0)
    def _():
        for jb in range(2):
            cols = slice(jb * LANES, (jb + 1) * LANES)
            _fill_ext(vl_ext.at[jb], v_ref[:, cols])
            _fill_ext(vc_ext.at[jb], vc_ref[:, cols])

    def grid_row(i, carry):
        r = row_group * rows_per_step + i
        r_start = jnp.clip(r - NA_KR // 2, 0, n_grid_rows - NA_KR)
        off = r - r_start
        q0 = pl.multiple_of(i * n_q, n_q)
        k0 = pl.multiple_of(r_start * GRID_W, GRID_W)
        q = q_ref[pl.ds(q0, n_q), :]
        outs = []
        for jb, (s_ref, p_ref) in enumerate(((s0, p0), (s1, p1))):
            cols = slice(jb * LANES, (jb + 1) * LANES)
            lhs = _stack_masked(q[:, cols], masks)
            bias = jnp.concatenate([bias_ref[2 * jb, off], bias_ref[2 * jb + 1, off]], axis=0)
            o, _ = _attend(lhs, [k_ref[pl.ds(k0, n_loc), cols], kc_ref[:, cols]],
                           [vl_ext[jb, pl.ds(k0, n_loc), :], vc_ext[jb]], s_ref, p_ref,
                           score_fn=lambda s: s + bias)
            outs.append(_pair_select(o[:, :LANES] / o[:, LANES:], n_q))
        o_ref[pl.ds(q0, n_q), :] = jnp.concatenate(outs, axis=1).astype(BF16)
        return carry

    lax.fori_loop(0, rows_per_step, grid_row, 0)


def _attn_d_kernel(lam_init, lamv_ref, gain_ref, q_ref, k_ref, v_ref, kc_ref, vc_ref, o_ref,
                   vl_ext, vc_ext, s0, p0, s1, p1):
    n_q = q_ref.shape[0]

    @pl.when(pl.program_id(1) == 0)
    def _():
        for jb in range(2):
            cols = slice(jb * LANES, (jb + 1) * LANES)
            _fill_ext(vl_ext.at[jb], v_ref[:, cols])
            _fill_ext(vc_ext.at[jb], vc_ref[:, cols])

    masks = _lane_masks(4, BF16)
    lane = lax.broadcasted_iota(jnp.int32, (1, LANES), 1)
    lam = _lam_value(lamv_ref, lam_init)
    q = q_ref[...]
    outs = []
    for jb in range(2):
        cols = slice(jb * LANES, (jb + 1) * LANES)
        ys = []
        for head, (s_ref, p_ref) in enumerate(((s0, p0), (s1, p1))):
            lhs = _stack_masked(q[:, cols], masks[2 * head:2 * head + 2])
            o, _ = _attend(lhs, [k_ref[:, cols], kc_ref[:, cols]], [vl_ext[jb], vc_ext[jb]],
                           s_ref, p_ref)
            ys.append(_diff_combine(o, n_q, lam))
        y = jnp.where(lane < HEAD_DIM, ys[0], ys[1])
        outs.append(_sub_ln(y, gain_ref[...], 1.0 - lam_init))
    o_ref[...] = jnp.concatenate(outs, axis=1).astype(BF16)


def _attn_ctx_kernel(lam_init, sink_ref, lamv_ref, gain_ref, aq_ref, ak_ref, av_ref, bq_ref, bk_ref,
                     bv_ref, cq_ref, ck_ref, cv_ref, dq_ref, dk_ref, dv_ref, o_ref,
                     s0, p0, s1, p1):
    n_q = aq_ref.shape[0]
    m2 = _lane_masks(2, BF16)
    m4 = _lane_masks(4, BF16)
    lane = lax.broadcasted_iota(jnp.int32, (1, LANES), 1)
    sets = ((s0, p0), (s1, p1))

    def ext(v):
        return jnp.concatenate([v, jnp.ones(v.shape, v.dtype)], axis=1)

    def out_cols(group, pair):
        c0 = group * GROUP_WIDTH + pair * LANES
        return slice(c0, c0 + LANES)

    for group, (q_ref_, k_ref_, v_ref_) in enumerate(((aq_ref, ak_ref, av_ref),
                                                      (bq_ref, bk_ref, bv_ref))):
        q = q_ref_[...]
        v_ext = ext(v_ref_[...])
        for pair, (s_ref, p_ref) in enumerate(sets):
            lhs = _stack_masked(q[:, pair * LANES:(pair + 1) * LANES], m2)
            sink = _sink_rows(sink_ref, pair, n_q) if group == 0 else None
            o, m = _attend(lhs, [k_ref_[...]], [v_ext], s_ref, p_ref, floor=sink)
            den = o[:, LANES:]
            if group == 0:
                den = den + jnp.exp2(sink - m)
            o_ref[:, out_cols(group, pair)] = _pair_select(o[:, :LANES] / den, n_q).astype(BF16)

    q = cq_ref[...]
    for jb, (s_ref, p_ref) in enumerate(sets):
        cols = slice(jb * LANES, (jb + 1) * LANES)
        o, _ = _attend(_stack_masked(q[:, cols], m2), [ck_ref[:, cols]], [ext(cv_ref[:, cols])],
                       s_ref, p_ref)
        o_ref[:, out_cols(2, jb)] = _pair_select(o[:, :LANES] / o[:, LANES:], n_q).astype(BF16)

    lam = _lam_value(lamv_ref, lam_init)
    q = dq_ref[...]
    for jb in range(2):
        cols = slice(jb * LANES, (jb + 1) * LANES)
        v_ext = ext(dv_ref[:, cols])
        ys = []
        for head, (s_ref, p_ref) in enumerate(sets):
            lhs = _stack_masked(q[:, cols], m4[2 * head:2 * head + 2])
            o, _ = _attend(lhs, [dk_ref[:, cols]], [v_ext], s_ref, p_ref)
            ys.append(_diff_combine(o, n_q, lam))
        y = jnp.where(lane < HEAD_DIM, ys[0], ys[1])
        o_ref[:, out_cols(3, jb)] = _sub_ln(y, gain_ref[...], 1.0 - lam_init).astype(BF16)


def _rope_tables(seq, n_extra):
    t = jnp.arange(seq, dtype=jnp.int32)
    row = (t // GRID_W).astype(F32)
    col = (t % GRID_W).astype(F32)
    tables = []
    for dim in (HEAD_DIM, D_SUB):
        half = dim // 2
        freqs = ROPE_BASE ** (-jnp.arange(0, half, 2, dtype=F32) / half)
        ang_r = row[:, None] * freqs[None, :]
        ang_c = col[:, None] * freqs[None, :]
        ang = jnp.concatenate([ang_r, ang_r, ang_c, ang_c], axis=-1)
        quarter = dim // 4
        sign = jnp.where((jnp.arange(dim) % (2 * quarter)) < quarter, -1.0, 1.0).astype(F32)
        cos = jnp.tile(jnp.cos(ang), (1, LANES // dim))
        sin = jnp.tile(jnp.sin(ang) * sign[None, :], (1, LANES // dim))
        cos = jnp.concatenate([cos, jnp.ones((n_extra, LANES), F32)], axis=0)
        sin = jnp.concatenate([sin, jnp.zeros((n_extra, LANES), F32)], axis=0)
        tables += [cos, sin]
    return jnp.stack(tables)


def _neighbourhood_bias(rel_bias):
    heads = rel_bias.shape[0]
    return pl.pallas_call(
        _bias_kernel,
        grid=(heads,),
        in_specs=[pl.BlockSpec(memory_space=pltpu.SMEM)],
        out_specs=pl.BlockSpec((1, NA_KR, GRID_W, NA_KR * GRID_W), lambda h: (h, 0, 0, 0)),
        out_shape=jax.ShapeDtypeStruct((heads, NA_KR, GRID_W, NA_KR * GRID_W), F32),
        compiler_params=_params(("arbitrary",)),
        name="neighbourhood_bias",
    )(rel_bias.astype(F32).reshape(-1))


def _permute_q_heads(w, col0):
    blocks = [w[:, col0 + h * HEAD_DIM:col0 + (h + 1) * HEAD_DIM] for h in GQA_HEAD_ORDER]
    return jnp.concatenate([w[:, :col0]] + blocks + [w[:, col0 + GROUP_WIDTH:]], axis=1)


def _permute_out_heads(w, row0):
    blocks = [w[row0 + h * HEAD_DIM:row0 + (h + 1) * HEAD_DIM] for h in GQA_HEAD_ORDER]
    return jnp.concatenate([w[:row0]] + blocks + [w[row0 + GROUP_WIDTH:]], axis=0)


def _score_scratch(n_rows, n_keys):
    return [pltpu.VMEM((n_rows, n_keys), F32), pltpu.VMEM((n_rows, n_keys), BF16)] * 2


def kernel(x, c, ctx, c_ctx, w_ada, b_ada, w_ffn1_gate, w_ffn1_up, w_ffn1_down, w_in, w_out,
           sink_logit, q_norm_g, k_norm_g, rel_pos_bias, lam_q1, lam_k1, lam_q2, lam_k2, subln_g,
           w_ffn2_gate, w_ffn2_up, w_ffn2_down, final_norm_g):
    batch, seq, d = x.shape
    n_ctx = ctx.shape[1]
    depth = w_ada.shape[0]
    assert d == D_MODEL and seq % TOKEN_TILE == 0 and n_ctx == TOKEN_TILE
    assert seq % GRID_W == 0 and w_in.shape[-1] == IN_WIDTH
    t_lat, t_ctx = batch * seq, batch * n_ctx
    t_all = t_lat + t_ctx
    tm = TOKEN_TILE
    n_lat_tiles, n_ctx_tiles = t_lat // tm, t_ctx // tm
    tiles_per_seq = seq // tm
    grid_rows = seq // GRID_W
    assert grid_rows >= NA_KR and rel_pos_bias.shape[1:] == (GROUP_HEADS, 2 * NA_KR - 1, 2 * NA_KC - 1)

    cc = jnp.concatenate([c, c_ctx[None, :], jnp.zeros((16 - batch - 1, d), F32)], axis=0)
    mods = _mods(cc, w_ada, b_ada).reshape(depth, 16, N_MOD, d)

    def group(i):
        return jnp.minimum(i // tiles_per_seq, batch)

    rope = _rope_tables(seq, tm)

    def rope_block(i):
        return jnp.where(i < n_lat_tiles, i % tiles_per_seq, tiles_per_seq)

    lane_gain = lambda g: jnp.tile(g.astype(F32), LANES // HEAD_DIM)[None, :]

    xs = jnp.concatenate([x.reshape(t_lat, d), ctx.reshape(t_ctx, d)], axis=0)

    tile_spec = pl.BlockSpec((tm, d), lambda i: (i, 0))

    for l in range(depth):
        last = l == depth - 1
        lam_init = 0.8 - 0.6 * math.exp(-0.3 * l)
        mod_spec = pl.BlockSpec((1, N_MOD, d), lambda i: (group(i), 0, 0))

        w_in_l = _permute_q_heads(_permute_q_heads(w_in[l], COL_AQ * LANES), COL_BQ * LANES)
        gains = jnp.concatenate([lane_gain(q_norm_g[l]), lane_gain(k_norm_g[l]),
                                 jnp.zeros((6, LANES), F32)], axis=0)
        xs, qkv = pl.pallas_call(
            _pre_kernel,
            grid=(n_lat_tiles + n_ctx_tiles,),
            in_specs=[
                tile_spec, mod_spec,
                _resident((d, D_FF)), _resident((d, D_FF)), _resident((D_FF, d)),
                _resident((d, IN_WIDTH)),
                pl.BlockSpec((4, tm, LANES), lambda i: (0, rope_block(i), 0)),
                _resident((8, LANES)),
            ],
            out_specs=[tile_spec, pl.BlockSpec((tm, IN_WIDTH), lambda i: (i, 0))],
            out_shape=[jax.ShapeDtypeStruct((t_all, d), F32),
                       jax.ShapeDtypeStruct((t_all, IN_WIDTH), BF16)],
            scratch_shapes=[pltpu.VMEM((tm, D_FF), BF16)],
            compiler_params=_params(("arbitrary",)),
            name=f"pre_l{l}",
        )(xs, mods[l], w_ffn1_gate[l].astype(BF16), w_ffn1_up[l].astype(BF16),
          w_ffn1_down[l].astype(BF16), w_in_l.astype(BF16), rope, gains)

        ctx_row = t_lat // n_ctx
        sink_perm = sink_logit[l].astype(F32)[jnp.array(GQA_HEAD_ORDER)]
        lamv = jnp.zeros((8, LANES), F32)
        lamv = lamv.at[0, :D_SUB].set(lam_q1[l]).at[1, :D_SUB].set(lam_k1[l])
        lamv = lamv.at[2, :D_SUB].set(lam_q2[l]).at[3, :D_SUB].set(lam_k2[l])
        subln = lane_gain(subln_g[l])
        smem = pl.BlockSpec(memory_space=pltpu.SMEM)

        def lat(cols, col):
            return pl.BlockSpec((seq, cols), lambda b, *_: (b, col))

        def ctxb(cols, col):
            return pl.BlockSpec((n_ctx, cols), lambda b, *_: (ctx_row + b, col))

        def v_ext_scratch(n_blocks):
            lead = () if n_blocks == 1 else (n_blocks,)
            return [pltpu.VMEM(lead + (seq, 2 * LANES), BF16),
                    pltpu.VMEM(lead + (n_ctx, 2 * LANES), BF16)]

        y_a = pl.pallas_call(
            functools.partial(_attn_a_kernel, seq),
            grid=(batch,),
            in_specs=[smem, lat(2 * LANES, COL_AQ // 2), lat(LANES, COL_AK), lat(LANES, COL_AV),
                      ctxb(LANES, COL_AK), ctxb(LANES, COL_AV)],
            out_specs=pl.BlockSpec((seq, GROUP_WIDTH), lambda b: (b, 0)),
            out_shape=jax.ShapeDtypeStruct((t_lat, GROUP_WIDTH), BF16),
            scratch_shapes=v_ext_scratch(1) + _score_scratch(2 * BLOCK, BLOCK + 2 * WINDOW + n_ctx),
            compiler_params=_params(("arbitrary",)),
            name=f"attn_a_l{l}",
        )(sink_perm, qkv, qkv, qkv, qkv, qkv)

        tq = ATTN_TILE
        q_tiles = seq // tq
        y_b = pl.pallas_call(
            _attn_b_kernel,
            grid=(batch, q_tiles),
            in_specs=[pl.BlockSpec((tq, 2 * LANES), lambda b, i: (b * q_tiles + i, COL_BQ // 2)),
                      lat(LANES, COL_BK), lat(LANES, COL_BV),
                      ctxb(LANES, COL_BK), ctxb(LANES, COL_BV)],
            out_specs=pl.BlockSpec((tq, GROUP_WIDTH), lambda b, i: (b * q_tiles + i, 0)),
            out_shape=jax.ShapeDtypeStruct((t_lat, GROUP_WIDTH), BF16),
            scratch_shapes=v_ext_scratch(1) + _score_scratch(2 * tq, seq + n_ctx),
            compiler_params=_params(("arbitrary", "arbitrary")),
            name=f"attn_b_l{l}",
        )(qkv, qkv, qkv, qkv, qkv)

        rows_per_step = 8
        tc = rows_per_step * GRID_W
        c_tiles = seq // tc
        bias = _neighbourhood_bias(rel_pos_bias[l])
        y_c = pl.pallas_call(
            functools.partial(_attn_c_kernel, rows_per_step, grid_rows),
            grid=(batch, c_tiles),
            in_specs=[pl.BlockSpec((tc, 2 * LANES), lambda b, i: (b * c_tiles + i, COL_CQ // 2)),
                      lat(2 * LANES, COL_CK // 2), lat(2 * LANES, COL_CV // 2),
                      ctxb(2 * LANES, COL_CK // 2), ctxb(2 * LANES, COL_CV // 2),
                      _resident(bias.shape)],
            out_specs=pl.BlockSpec((tc, GROUP_WIDTH), lambda b, i: (b * c_tiles + i, 0)),
            out_shape=jax.ShapeDtypeStruct((t_lat, GROUP_WIDTH), BF16),
            scratch_shapes=v_ext_scratch(2) + _score_scratch(2 * GRID_W, NA_KR * GRID_W + n_ctx),
            compiler_params=_params(("arbitrary", "arbitrary")),
            name=f"attn_c_l{l}",
        )(qkv, qkv, qkv, qkv, qkv, bias)

        y_d = pl.pallas_call(
            functools.partial(_attn_d_kernel, lam_init),
            grid=(batch, q_tiles),
            in_specs=[_resident((8, LANES)), _resident((1, LANES)),
                      pl.BlockSpec((tq, 2 * LANES), lambda b, i: (b * q_tiles + i, COL_DQ // 2)),
                      lat(2 * LANES, COL_DK // 2), lat(2 * LANES, COL_DV // 2),
                      ctxb(2 * LANES, COL_DK // 2), ctxb(2 * LANES, COL_DV // 2)],
            out_specs=pl.BlockSpec((tq, GROUP_WIDTH), lambda b, i: (b * q_tiles + i, 0)),
            out_shape=jax.ShapeDtypeStruct((t_lat, GROUP_WIDTH), BF16),
            scratch_shapes=v_ext_scratch(2) + _score_scratch(2 * tq, seq + n_ctx),
            compiler_params=_params(("arbitrary", "arbitrary")),
            name=f"attn_d_l{l}",
        )(lamv, subln, qkv, qkv, qkv, qkv, qkv)

        w_out_l = _permute_out_heads(_permute_out_heads(w_out[l], 0), GROUP_WIDTH).astype(BF16)
        ffn2 = (w_ffn2_gate[l].astype(BF16), w_ffn2_up[l].astype(BF16), w_ffn2_down[l].astype(BF16))
        ffn2_specs = [_resident((d, D_FF)), _resident((d, D_FF)), _resident((D_FF, d))]
        fng = final_norm_g.astype(F32)[None, :]
        y_spec = pl.BlockSpec((tm, GROUP_WIDTH), lambda i: (jnp.minimum(i, n_lat_tiles - 1), 0))
        post_scratch = [pltpu.VMEM((tm, d), BF16), pltpu.VMEM((tm, D_FF), BF16)]

        if not last:
            y_ctx = pl.pallas_call(
                functools.partial(_attn_ctx_kernel, lam_init),
                grid=(batch,),
                in_specs=[smem, _resident((8, LANES)), _resident((1, LANES)),
                          ctxb(2 * LANES, COL_AQ // 2), ctxb(LANES, COL_AK), ctxb(LANES, COL_AV),
                          ctxb(2 * LANES, COL_BQ // 2), ctxb(LANES, COL_BK), ctxb(LANES, COL_BV),
                          ctxb(2 * LANES, COL_CQ // 2), ctxb(2 * LANES, COL_CK // 2),
                          ctxb(2 * LANES, COL_CV // 2),
                          ctxb(2 * LANES, COL_DQ // 2), ctxb(2 * LANES, COL_DK // 2),
                          ctxb(2 * LANES, COL_DV // 2)],
                out_specs=pl.BlockSpec((n_ctx, d), lambda b: (b, 0)),
                out_shape=jax.ShapeDtypeStruct((t_ctx, d), BF16),
                scratch_shapes=_score_scratch(2 * n_ctx, n_ctx),
                compiler_params=_params(("arbitrary",)),
                name=f"attn_ctx_l{l}",
            )(sink_perm, lamv, subln, *([qkv] * 12))

            xs = pl.pallas_call(
                functools.partial(_post_kernel, n_lat_tiles, False),
                grid=(n_lat_tiles + n_ctx_tiles,),
                in_specs=[tile_spec, mod_spec, y_spec, y_spec, y_spec, y_spec,
                          pl.BlockSpec((tm, d), lambda i: (jnp.maximum(i - n_lat_tiles, 0), 0)),
                          _resident((d, d))] + ffn2_specs + [_resident((1, d))],
                out_specs=tile_spec,
                out_shape=jax.ShapeDtypeStruct((t_all, d), F32),
                scratch_shapes=post_scratch,
                compiler_params=_params(("arbitrary",)),
                name=f"post_l{l}",
            )(xs, mods[l], y_a, y_b, y_c, y_d, y_ctx, w_out_l, *ffn2, fng)
        else:
            xs = pl.pallas_call(
                functools.partial(_post_kernel, None, True),
                grid=(n_lat_tiles,),
                in_specs=[tile_spec, mod_spec, y_spec, y_spec, y_spec, y_spec,
                          _resident((d, d))] + ffn2_specs + [_resident((1, d))],
                out_specs=tile_spec,
                out_shape=jax.ShapeDtypeStruct((t_lat, d), F32),
                scratch_shapes=post_scratch,
                compiler_params=_params(("arbitrary",)),
                name=f"post_l{l}",
            )(xs, mods[l], y_a, y_b, y_c, y_d, w_out_l, *ffn2, fng)

    return xs.reshape(batch, seq, d)
```

```python
import functools
import math

import jax
import jax.numpy as jnp
from jax import lax
from jax.experimental import pallas as pl
from jax.experimental.pallas import tpu as pltpu

D_MODEL = 1024
GRID_W = 64
HEAD_DIM = 64
N_GROUPS = 4
GROUP_HEADS = D_MODEL // (N_GROUPS * HEAD_DIM)
GROUP_WIDTH = GROUP_HEADS * HEAD_DIM
D_SUB = HEAD_DIM // 2
WINDOW = 128
BLOCK = 128
NA_KR = 8
NA_KC = 16
D_FF = 2816
ROPE_BASE = 10000.0
NORM_EPS = 1e-6
N_MOD = 9
NEG_INF = -1e30
IN_WIDTH = 2560

LANES = 128
MXU_N = 256
TOKEN_TILE = 256
ATTN_TILE = 128
ROW_BLOCK = 32
VMEM_LIMIT = 56 * 1024 * 1024
LOG2E = 1.4426950408889634

BF16 = jnp.bfloat16
F32 = jnp.float32

COL_AQ, COL_AK, COL_AV = 0, 2, 3
COL_BQ, COL_BK, COL_BV = 4, 6, 7
COL_CQ, COL_CK, COL_CV = 8, 10, 12
COL_DQ, COL_DK, COL_DV = 14, 16, 18

GQA_HEAD_ORDER = (0, 2, 1, 3)


def _dot(a, b):
    return jnp.dot(a, b, preferred_element_type=F32)


def _dot_nt(a, b):
    return lax.dot_general(a, b, (((1,), (1,)), ((), ())), preferred_element_type=F32)


def _params(semantics):
    return pltpu.CompilerParams(dimension_semantics=semantics, vmem_limit_bytes=VMEM_LIMIT)


def _resident(shape):
    nd = len(shape)
    return pl.BlockSpec(shape, lambda *_: (0,) * nd, pipeline_mode=pl.Buffered(1))


def _mods_kernel(c_ref, w_ref, b_ref, o_ref):
    c = c_ref[...]
    a = c * (1.0 / (1.0 + jnp.exp(-c)))
    o_ref[...] = _dot(a.astype(BF16), w_ref[...].astype(BF16)) + b_ref[...]


def _mods(cc, w_ada, b_ada):
    depth, d, n = w_ada.shape
    rows = cc.shape[0]
    tn = 1152
    return pl.pallas_call(
        _mods_kernel,
        grid=(depth, n // tn),
        in_specs=[
            pl.BlockSpec((rows, d), lambda l, j: (0, 0)),
            pl.BlockSpec((None, d, tn), lambda l, j: (l, 0, j)),
            pl.BlockSpec((None, 1, tn), lambda l, j: (l, 0, j)),
        ],
        out_specs=pl.BlockSpec((None, rows, tn), lambda l, j: (l, 0, j)),
        out_shape=jax.ShapeDtypeStruct((depth, rows, n), F32),
        compiler_params=_params(("arbitrary", "arbitrary")),
        name="adaln_mods",
    )(cc, w_ada, b_ada.reshape(depth, 1, n))


def _modulated(x, mod_ref, k):
    shift = mod_ref[0, k:k + 1, :]
    scale = mod_ref[0, k + 1:k + 2, :]
    ms = jnp.mean(x * x, axis=-1, keepdims=True)
    return (x * lax.rsqrt(ms + NORM_EPS)) * (1.0 + scale) + shift


def _swiglu(h, wg_ref, wu_ref, wd_ref, act_ref):
    hb = h.astype(BF16)
    for j in range(D_FF // MXU_N):
        cols = slice(j * MXU_N, (j + 1) * MXU_N)
        g = _dot(hb, wg_ref[:, cols])
        u = _dot(hb, wu_ref[:, cols])
        act_ref[:, cols] = ((g * (1.0 / (1.0 + jnp.exp(-g)))) * u).astype(BF16)
    return _dot(act_ref[...], wd_ref[...])


def _rope(v, cos, sin_signed, quarter):
    lane = lax.broadcasted_iota(jnp.int32, (1, LANES), 1)
    first = (lane & (2 * quarter - 1)) < quarter
    rot = jnp.where(first, pltpu.roll(v, LANES - quarter, 1), pltpu.roll(v, quarter, 1))
    return v * cos + rot * sin_signed


def _head_rms(v, gain):
    lane = lax.broadcasted_iota(jnp.int32, (1, LANES), 1)
    lo = lane < HEAD_DIM
    sq = v * v
    ms_lo = jnp.sum(jnp.where(lo, sq, 0.0), axis=-1, keepdims=True) * (1.0 / HEAD_DIM)
    ms_hi = jnp.sum(jnp.where(lo, 0.0, sq), axis=-1, keepdims=True) * (1.0 / HEAD_DIM)
    rs = jnp.where(lo, lax.rsqrt(ms_lo + NORM_EPS), lax.rsqrt(ms_hi + NORM_EPS))
    return v * rs * gain


_QK_SCALE = HEAD_DIM ** -0.5 * LOG2E
_SUB_SCALE = D_SUB ** -0.5 * LOG2E
_PROJ_BLOCKS = (
    (16, -1, _QK_SCALE), (16, -1, _QK_SCALE), (16, -1, 1.0), (0, -1, 1.0),
    (16, 0, _QK_SCALE), (16, 0, _QK_SCALE), (16, 1, 1.0), (0, -1, 1.0),
    (0, -1, _QK_SCALE), (0, -1, _QK_SCALE), (0, -1, 1.0), (0, -1, 1.0),
    (0, -1, 1.0), (0, -1, 1.0),
    (8, -1, _SUB_SCALE), (8, -1, _SUB_SCALE), (8, -1, 1.0), (8, -1, 1.0),
    (0, -1, 1.0), (0, -1, 1.0),
)


def _project(h, w_ref, rope_ref, gain_ref, o_ref):
    hb = h.astype(BF16)
    for j in range(IN_WIDTH // MXU_N):
        r = _dot(hb, w_ref[:, j * MXU_N:(j + 1) * MXU_N])
        for half in range(MXU_N // LANES):
            blk = j * (MXU_N // LANES) + half
            quarter, gain_idx, scale = _PROJ_BLOCKS[blk]
            v = r[:, half * LANES:(half + 1) * LANES]
            if gain_idx >= 0:
                v = _head_rms(v, gain_ref[gain_idx:gain_idx + 1, :])
            if quarter:
                t = 0 if quarter == 16 else 2
                v = _rope(v, rope_ref[t], rope_ref[t + 1], quarter)
            if scale != 1.0:
                v = v * scale
            o_ref[:, blk * LANES:(blk + 1) * LANES] = v.astype(BF16)


def _pre_kernel(n_lat_tiles, *refs):
    if n_lat_tiles is None:
        (x_ref, mod_ref, wg_ref, wu_ref, wd_ref, win_ref, rope_ref, gain_ref,
         xo_ref, qkv_ref, act_ref) = refs
        x = x_ref[...]
    else:
        (x_ref, c_ref, mod_ref, wg_ref, wu_ref, wd_ref, win_ref, rope_ref, gain_ref,
         xo_ref, qkv_ref, act_ref) = refs
        x = jnp.where(pl.program_id(0) < n_lat_tiles, x_ref[...], c_ref[...])
    y = _swiglu(_modulated(x, mod_ref, 0), wg_ref, wu_ref, wd_ref, act_ref)
    x = x + (0.5 * mod_ref[0, 2:3, :]) * y
    xo_ref[...] = x
    _project(_modulated(x, mod_ref, 3), win_ref, rope_ref, gain_ref, qkv_ref)


def _post_kernel(n_lat_tiles, final_norm, *refs):
    if n_lat_tiles is None:
        (x_ref, mod_ref, ya_ref, yb_ref, yc_ref, yd_ref, wout_ref, wg_ref, wu_ref, wd_ref,
         fng_ref, xo_ref, y_scr, act_ref) = refs
        y_scr[...] = jnp.concatenate([ya_ref[...], yb_ref[...], yc_ref[...], yd_ref[...]], axis=1)
    else:
        (x_ref, mod_ref, ya_ref, yb_ref, yc_ref, yd_ref, yctx_ref, wout_ref, wg_ref, wu_ref,
         wd_ref, fng_ref, xo_ref, y_scr, act_ref) = refs
        is_latent = pl.program_id(0) < n_lat_tiles

        @pl.when(is_latent)
        def _():
            y_scr[...] = jnp.concatenate(
                [ya_ref[...], yb_ref[...], yc_ref[...], yd_ref[...]], axis=1)

        @pl.when(jnp.logical_not(is_latent))
        def _():
            y_scr[...] = yctx_ref[...]

    x = x_ref[...]
    x = x + mod_ref[0, 5:6, :] * _dot(y_scr[...], wout_ref[...])
    y = _swiglu(_modulated(x, mod_ref, 6), wg_ref, wu_ref, wd_ref, act_ref)
    x = x + (0.5 * mod_ref[0, 8:9, :]) * y
    if final_norm:
        ms = jnp.mean(x * x, axis=-1, keepdims=True)
        x = (x * lax.rsqrt(ms + NORM_EPS)) * fng_ref[...]
    xo_ref[...] = x


def _lane_masks(n, dtype):
    lane = lax.broadcasted_iota(jnp.int32, (1, LANES), 1)
    w = LANES // n
    return [jnp.where((lane >= k * w) & (lane < (k + 1) * w), 1.0, 0.0).astype(dtype)
            for k in range(n)]


def _stack_masked(q, masks):
    return jnp.concatenate([q * m for m in masks], axis=0)


def _fill_ext(ext_ref, v):
    ext_ref[:, :LANES] = v
    ext_ref[:, LANES:] = jnp.ones(v.shape, v.dtype)


def _by_slot(four):
    return ((four[0], four[1]), (four[2], four[3]))


def _prime_pipeline(s_refs, p_refs, m_refs=()):
    @pl.when(pl.program_id(0) == 0)
    def _():
        for s_ref in tuple(s_refs) + tuple(m_refs):
            s_ref[...] = jnp.zeros(s_ref.shape, s_ref.dtype)
        for p_ref in p_refs:
            p_ref[...] = jnp.ones(p_ref.shape, p_ref.dtype)


def _scores(lhs, k_list, s_ref, score_fn=None):
    off = 0
    for idx, k in enumerate(k_list):
        s = _dot_nt(lhs, k)
        if idx == 0 and score_fn is not None:
            s = score_fn(s)
        s_ref[:, off:off + k.shape[0]] = s
        off += k.shape[0]


def _numerators(s_ref, p_ref, floor=None):
    maxima = []
    for r in range(0, s_ref.shape[0], ROW_BLOCK):
        rows = slice(r, r + ROW_BLOCK)
        m = jnp.max(s_ref[rows, :], axis=-1, keepdims=True)
        if floor is not None:
            m = jnp.maximum(m, floor[rows])
        p_ref[rows, :] = jnp.exp2(s_ref[rows, :] - m).astype(BF16)
        maxima.append(m)
    return jnp.concatenate(maxima, axis=0)


def _weighted_values(p_ref, v_list):
    o = None
    off = 0
    for v in v_list:
        part = _dot(p_ref[:, off:off + v.shape[0]], v)
        o = part if o is None else o + part
        off += v.shape[0]
    return o


def _attend(lhs, k_list, v_list, s_ref, p_ref, floor=None):
    _scores(lhs, k_list, s_ref)
    m = _numerators(s_ref, p_ref, floor)
    return _weighted_values(p_ref, v_list), m


def _pair_select(o, n_q):
    lane = lax.broadcasted_iota(jnp.int32, (1, LANES), 1)
    return jnp.where(lane < HEAD_DIM, o[:n_q], o[n_q:2 * n_q])


def _sub_ln(y, gain, post_scale):
    return _head_rms(y, gain) * post_scale


def _lam_value(lamv_ref, lam_init):
    t1 = jnp.sum(lamv_ref[0:1, :] * lamv_ref[1:2, :], axis=-1, keepdims=True)
    t2 = jnp.sum(lamv_ref[2:3, :] * lamv_ref[3:4, :], axis=-1, keepdims=True)
    return jnp.exp(t1) - jnp.exp(t2) + lam_init


def _sink_rows(sink_ref, pair, n_q):
    row = lax.broadcasted_iota(jnp.int32, (2 * n_q, 1), 0)
    return jnp.where(row < n_q, sink_ref[2 * pair], sink_ref[2 * pair + 1]) * LOG2E


def _diff_combine(o, n_q, lam):
    r = o[:, :LANES] / o[:, LANES:]
    return r[:n_q] - lam * r[n_q:]


def _bias_kernel(rel_ref, o_ref):
    h = pl.program_id(0)
    n_dr, n_dc = 2 * NA_KR - 1, 2 * NA_KC - 1
    q_col = lax.broadcasted_iota(jnp.int32, (GRID_W, LANES), 0)
    lane = lax.broadcasted_iota(jnp.int32, (GRID_W, LANES), 1)
    k_col = lane & (GRID_W - 1)
    dc = k_col - q_col + (NA_KC - 1)
    c_start = jnp.clip(q_col - NA_KC // 2, 0, GRID_W - NA_KC)
    in_window = (k_col >= c_start) & (k_col < c_start + NA_KC)
    tiles = []
    for dr in range(n_dr):
        t = jnp.zeros((GRID_W, LANES), F32)
        for d in range(n_dc):
            t = jnp.where(dc == d, rel_ref[(h * n_dr + dr) * n_dc + d], t)
        tiles.append(jnp.where(in_window, t * LOG2E, NEG_INF))
    low = lane < GRID_W
    for off in range(NA_KR):
        for c in range(NA_KR * GRID_W // LANES):
            dr = 2 * c - off + NA_KR - 1
            o_ref[0, off, :, c * LANES:(c + 1) * LANES] = jnp.where(low, tiles[dr], tiles[dr + 1])


def _attn_a_kernel(seq, sink_ref, q_ref, k_ref, v_ref, kc_ref, vc_ref, o_ref,
                   vl_ext, vc_ext, *bufs):
    n_q = BLOCK
    span = BLOCK + 2 * WINDOW
    n_blocks = seq // n_q
    masks = _lane_masks(2, BF16)
    s_refs, p_refs, m_refs = (_by_slot(bufs[i:i + 4]) for i in (0, 4, 8))
    _prime_pipeline(bufs[0:4], bufs[4:8], bufs[8:12])
    _fill_ext(vl_ext, v_ref[...])
    _fill_ext(vc_ext, vc_ref[...])

    def window(n):
        q0 = pl.multiple_of(n * n_q, n_q)
        return q0, pl.multiple_of(jnp.clip(q0 - WINDOW, 0, seq - span), BLOCK)

    def slot(t, new, old):
        q0, start = window(jnp.clip(t - 2, 0, n_blocks - 1))
        v_loc = vl_ext[pl.ds(start, span), :]
        outs = []
        for pair in range(2):
            o = _weighted_values(p_refs[new][pair], [v_loc, vc_ext[...]])
            den = o[:, LANES:] + jnp.exp2(_sink_rows(sink_ref, pair, n_q) - m_refs[new][pair][...])
            outs.append(_pair_select(o[:, :LANES] / den, n_q))
        o_ref[pl.ds(q0, n_q), :] = jnp.concatenate(outs, axis=1).astype(BF16)
        for pair in range(2):
            m = _numerators(s_refs[old][pair], p_refs[old][pair],
                            floor=_sink_rows(sink_ref, pair, n_q))
            m_refs[old][pair][...] = jnp.broadcast_to(m, (2 * n_q, LANES))
        q0, start = window(jnp.minimum(t, n_blocks - 1))
        q = q_ref[pl.ds(q0, n_q), :]
        k_loc = k_ref[pl.ds(start, span), :]
        q_pos = q0 + (lax.broadcasted_iota(jnp.int32, (2 * n_q, 1), 0) & (n_q - 1))
        k_pos = start + lax.broadcasted_iota(jnp.int32, (1, span), 1)
        valid = jnp.abs(k_pos - q_pos) <= WINDOW
        for pair in range(2):
            lhs = _stack_masked(q[:, pair * LANES:(pair + 1) * LANES], masks)
            _scores(lhs, [k_loc, kc_ref[...]], s_refs[new][pair],
                    score_fn=lambda s: jnp.where(valid, s, NEG_INF))

    def two_slots(j, carry):
        slot(2 * j, 0, 1)
        slot(2 * j + 1, 1, 0)
        return carry

    lax.fori_loop(0, n_blocks // 2 + 1, two_slots, 0)


def _attn_b_kernel(blocks_per_seq, q_ref, k_ref, v_ref, kc_ref, vc_ref, o_ref,
                   vl_ext, vc_ext, *bufs):
    n_q = q_ref.shape[0] // 2
    s_refs, p_refs = _by_slot(bufs[0:4]), _by_slot(bufs[4:8])
    _prime_pipeline(bufs[0:4], bufs[4:8])

    @pl.when(lax.rem(jnp.maximum(pl.program_id(0) - 1, 0), blocks_per_seq) == 0)
    def _():
        _fill_ext(vl_ext, v_ref[...])
        _fill_ext(vc_ext, vc_ref[...])

    masks = _lane_masks(2, BF16)
    for new, old in ((0, 1), (1, 0)):
        rows = slice(new * n_q, (new + 1) * n_q)
        outs = []
        for pair in range(2):
            o = _weighted_values(p_refs[new][pair], [vl_ext[...], vc_ext[...]])
            outs.append(_pair_select(o[:, :LANES] / o[:, LANES:], n_q))
        o_ref[rows, :] = jnp.concatenate(outs, axis=1).astype(BF16)
        for pair in range(2):
            _numerators(s_refs[old][pair], p_refs[old][pair])
        q = q_ref[rows, :]
        for pair in range(2):
            lhs = _stack_masked(q[:, pair * LANES:(pair + 1) * LANES], masks)
            _scores(lhs, [k_ref[...], kc_ref[...]], s_refs[new][pair])


def _attn_c_kernel(n_grid_rows, q_ref, k_ref, v_ref, kc_ref, vc_ref, bias_ref,
                   o_ref, vl_ext, vc_ext, *bufs):
    n_q = GRID_W
    n_loc = NA_KR * GRID_W
    masks = _lane_masks(2, BF16)
    s_refs, p_refs = _by_slot(bufs[0:4]), _by_slot(bufs[4:8])
    _prime_pipeline(bufs[0:4], bufs[4:8])
    for jb in range(2):
        cols = slice(jb * LANES, (jb + 1) * LANES)
        _fill_ext(vl_ext.at[jb], v_ref[:, cols])
        _fill_ext(vc_ext.at[jb], vc_ref[:, cols])

    def geometry(r):
        r_start = jnp.clip(r - NA_KR // 2, 0, n_grid_rows - NA_KR)
        return (pl.multiple_of(r * n_q, n_q), pl.multiple_of(r_start * GRID_W, GRID_W),
                r - r_start)

    def slot(t, new, old):
        q0, k0, _ = geometry(jnp.clip(t - 2, 0, n_grid_rows - 1))
        outs = []
        for jb in range(2):
            o = _weighted_values(p_refs[new][jb], [vl_ext[jb, pl.ds(k0, n_loc), :], vc_ext[jb]])
            outs.append(_pair_select(o[:, :LANES] / o[:, LANES:], n_q))
        o_ref[pl.ds(q0, n_q), :] = jnp.concatenate(outs, axis=1).astype(BF16)
        for jb in range(2):
            _numerators(s_refs[old][jb], p_refs[old][jb])
        q0, k0, off = geometry(jnp.minimum(t, n_grid_rows - 1))
        q = q_ref[pl.ds(q0, n_q), :]
        for jb in range(2):
            cols = slice(jb * LANES, (jb + 1) * LANES)
            lhs = _stack_masked(q[:, cols], masks)
            bias = jnp.concatenate([bias_ref[2 * jb, off], bias_ref[2 * jb + 1, off]], axis=0)
            _scores(lhs, [k_ref[pl.ds(k0, n_loc), cols], kc_ref[:, cols]], s_refs[new][jb],
                    score_fn=lambda s: s + bias)

    def two_slots(j, carry):
        slot(2 * j, 0, 1)
        slot(2 * j + 1, 1, 0)
        return carry

    lax.fori_loop(0, n_grid_rows // 2 + 1, two_slots, 0)


def _attn_d_kernel(lam_init, blocks_per_seq, lamv_ref, gain_ref, q_ref, k_ref, v_ref, kc_ref,
                   vc_ref, o_ref, vl_ext, vc_ext, *bufs):
    n_q = q_ref.shape[0] // 2
    s_refs = (_by_slot(bufs[0:4]), _by_slot(bufs[4:8]))
    p_refs = (_by_slot(bufs[8:12]), _by_slot(bufs[12:16]))
    _prime_pipeline(bufs[0:8], bufs[8:16])

    @pl.when(lax.rem(jnp.maximum(pl.program_id(0) - 1, 0), blocks_per_seq) == 0)
    def _():
        for jb in range(2):
            cols = slice(jb * LANES, (jb + 1) * LANES)
            _fill_ext(vl_ext.at[jb], v_ref[:, cols])
            _fill_ext(vc_ext.at[jb], vc_ref[:, cols])

    lane = lax.broadcasted_iota(jnp.int32, (1, LANES), 1)
    lam = _lam_value(lamv_ref, lam_init)
    masks = _lane_masks(4, BF16)
    for new, old in ((0, 1), (1, 0)):
        rows = slice(new * n_q, (new + 1) * n_q)
        outs = []
        for jb in range(2):
            ys = [_diff_combine(_weighted_values(p_refs[new][jb][head], [vl_ext[jb], vc_ext[jb]]),
                                n_q, lam) for head in range(2)]
            y = jnp.where(lane < HEAD_DIM, ys[0], ys[1])
            outs.append(_sub_ln(y, gain_ref[...], 1.0 - lam_init))
        o_ref[rows, :] = jnp.concatenate(outs, axis=1).astype(BF16)
        for jb in range(2):
            for head in range(2):
                _numerators(s_refs[old][jb][head], p_refs[old][jb][head])
        q = q_ref[rows, :]
        for jb in range(2):
            cols = slice(jb * LANES, (jb + 1) * LANES)
            for head in range(2):
                lhs = _stack_masked(q[:, cols], masks[2 * head:2 * head + 2])
                _scores(lhs, [k_ref[:, cols], kc_ref[:, cols]], s_refs[new][jb][head])


def _attn_ctx_kernel(lam_init, sink_ref, lamv_ref, gain_ref, aq_ref, ak_ref, av_ref, bq_ref, bk_ref,
                     bv_ref, cq_ref, ck_ref, cv_ref, dq_ref, dk_ref, dv_ref, o_ref,
                     s0, s1, p0, p1):
    n_q = aq_ref.shape[0]
    m2 = _lane_masks(2, BF16)
    m4 = _lane_masks(4, BF16)
    lane = lax.broadcasted_iota(jnp.int32, (1, LANES), 1)
    sets = ((s0, p0), (s1, p1))

    def ext(v):
        return jnp.concatenate([v, jnp.ones(v.shape, v.dtype)], axis=1)

    def out_cols(group, pair):
        c0 = group * GROUP_WIDTH + pair * LANES
        return slice(c0, c0 + LANES)

    for group, (q_ref_, k_ref_, v_ref_) in enumerate(((aq_ref, ak_ref, av_ref),
                                                      (bq_ref, bk_ref, bv_ref))):
        q = q_ref_[...]
        v_ext = ext(v_ref_[...])
        for pair, (s_ref, p_ref) in enumerate(sets):
            lhs = _stack_masked(q[:, pair * LANES:(pair + 1) * LANES], m2)
            sink = _sink_rows(sink_ref, pair, n_q) if group == 0 else None
            o, m = _attend(lhs, [k_ref_[...]], [v_ext], s_ref, p_ref, floor=sink)
            den = o[:, LANES:]
            if group == 0:
                den = den + jnp.exp2(sink - m)
            o_ref[:, out_cols(group, pair)] = _pair_select(o[:, :LANES] / den, n_q).astype(BF16)

    q = cq_ref[...]
    for jb, (s_ref, p_ref) in enumerate(sets):
        cols = slice(jb * LANES, (jb + 1) * LANES)
        o, _ = _attend(_stack_masked(q[:, cols], m2), [ck_ref[:, cols]], [ext(cv_ref[:, cols])],
                       s_ref, p_ref)
        o_ref[:, out_cols(2, jb)] = _pair_select(o[:, :LANES] / o[:, LANES:], n_q).astype(BF16)

    lam = _lam_value(lamv_ref, lam_init)
    q = dq_ref[...]
    for jb in range(2):
        cols = slice(jb * LANES, (jb + 1) * LANES)
        v_ext = ext(dv_ref[:, cols])
        ys = []
        for head, (s_ref, p_ref) in enumerate(sets):
            lhs = _stack_masked(q[:, cols], m4[2 * head:2 * head + 2])
            o, _ = _attend(lhs, [dk_ref[:, cols]], [v_ext], s_ref, p_ref)
            ys.append(_diff_combine(o, n_q, lam))
        y = jnp.where(lane < HEAD_DIM, ys[0], ys[1])
        o_ref[:, out_cols(3, jb)] = _sub_ln(y, gain_ref[...], 1.0 - lam_init).astype(BF16)


def _rope_tables(seq, n_extra):
    t = jnp.arange(seq, dtype=jnp.int32)
    row = (t // GRID_W).astype(F32)
    col = (t % GRID_W).astype(F32)
    tables = []
    for dim in (HEAD_DIM, D_SUB):
        half = dim // 2
        freqs = ROPE_BASE ** (-jnp.arange(0, half, 2, dtype=F32) / half)
        ang_r = row[:, None] * freqs[None, :]
        ang_c = col[:, None] * freqs[None, :]
        ang = jnp.concatenate([ang_r, ang_r, ang_c, ang_c], axis=-1)
        quarter = dim // 4
        sign = jnp.where((jnp.arange(dim) % (2 * quarter)) < quarter, -1.0, 1.0).astype(F32)
        cos = jnp.tile(jnp.cos(ang), (1, LANES // dim))
        sin = jnp.tile(jnp.sin(ang) * sign[None, :], (1, LANES // dim))
        cos = jnp.concatenate([cos, jnp.ones((n_extra, LANES), F32)], axis=0)
        sin = jnp.concatenate([sin, jnp.zeros((n_extra, LANES), F32)], axis=0)
        tables += [cos, sin]
    return jnp.stack(tables)


def _neighbourhood_bias(rel_bias):
    heads = rel_bias.shape[0]
    return pl.pallas_call(
        _bias_kernel,
        grid=(heads,),
        in_specs=[pl.BlockSpec(memory_space=pltpu.SMEM)],
        out_specs=pl.BlockSpec((1, NA_KR, GRID_W, NA_KR * GRID_W), lambda h: (h, 0, 0, 0)),
        out_shape=jax.ShapeDtypeStruct((heads, NA_KR, GRID_W, NA_KR * GRID_W), F32),
        compiler_params=_params(("arbitrary",)),
        name="neighbourhood_bias",
    )(rel_bias.astype(F32).reshape(-1))


def _permute_q_heads(w, col0):
    blocks = [w[:, col0 + h * HEAD_DIM:col0 + (h + 1) * HEAD_DIM] for h in GQA_HEAD_ORDER]
    return jnp.concatenate([w[:, :col0]] + blocks + [w[:, col0 + GROUP_WIDTH:]], axis=1)


def _permute_out_heads(w, row0):
    blocks = [w[row0 + h * HEAD_DIM:row0 + (h + 1) * HEAD_DIM] for h in GQA_HEAD_ORDER]
    return jnp.concatenate([w[:row0]] + blocks + [w[row0 + GROUP_WIDTH:]], axis=0)


def _score_scratch(n_rows, n_keys, n_groups=2):
    return ([pltpu.VMEM((n_rows, n_keys), F32)] * (2 * n_groups)
            + [pltpu.VMEM((n_rows, n_keys), BF16)] * (2 * n_groups))


def kernel(x, c, ctx, c_ctx, w_ada, b_ada, w_ffn1_gate, w_ffn1_up, w_ffn1_down, w_in, w_out,
           sink_logit, q_norm_g, k_norm_g, rel_pos_bias, lam_q1, lam_k1, lam_q2, lam_k2, subln_g,
           w_ffn2_gate, w_ffn2_up, w_ffn2_down, final_norm_g):
    batch, seq, d = x.shape
    n_ctx = ctx.shape[1]
    depth = w_ada.shape[0]
    assert d == D_MODEL and seq % TOKEN_TILE == 0 and n_ctx == TOKEN_TILE
    assert seq % GRID_W == 0 and w_in.shape[-1] == IN_WIDTH
    assert seq % (2 * ATTN_TILE) == 0 and seq % (2 * BLOCK) == 0 and seq % (2 * GRID_W) == 0
    t_lat, t_ctx = batch * seq, batch * n_ctx
    t_all = t_lat + t_ctx
    tm = TOKEN_TILE
    n_lat_tiles, n_ctx_tiles = t_lat // tm, t_ctx // tm
    tiles_per_seq = seq // tm
    grid_rows = seq // GRID_W
    assert grid_rows >= NA_KR and rel_pos_bias.shape[1:] == (GROUP_HEADS, 2 * NA_KR - 1, 2 * NA_KC - 1)

    cc = jnp.concatenate([c, c_ctx[None, :], jnp.zeros((16 - batch - 1, d), F32)], axis=0)
    mods = _mods(cc, w_ada, b_ada).reshape(depth, 16, N_MOD, d)

    def group(i):
        return jnp.minimum(i // tiles_per_seq, batch)

    rope = _rope_tables(seq, tm)

    def rope_block(i):
        return jnp.where(i < n_lat_tiles, i % tiles_per_seq, tiles_per_seq)

    lane_gain = lambda g: jnp.tile(g.astype(F32), LANES // HEAD_DIM)[None, :]

    tile_spec = pl.BlockSpec((tm, d), lambda i: (i, 0))

    for l in range(depth):
        last = l == depth - 1
        lam_init = 0.8 - 0.6 * math.exp(-0.3 * l)
        mod_spec = pl.BlockSpec((1, N_MOD, d), lambda i: (group(i), 0, 0))

        w_in_l = _permute_q_heads(_permute_q_heads(w_in[l], COL_AQ * LANES), COL_BQ * LANES)
        gains = jnp.concatenate([lane_gain(q_norm_g[l]), lane_gain(k_norm_g[l]),
                                 jnp.zeros((6, LANES), F32)], axis=0)
        if l == 0:
            streams = (x.reshape(t_lat, d), ctx.reshape(t_ctx, d))
            stream_specs = [
                pl.BlockSpec((tm, d), lambda i: (jnp.minimum(i, n_lat_tiles - 1), 0)),
                pl.BlockSpec((tm, d), lambda i: (jnp.maximum(i - n_lat_tiles, 0), 0))]
        else:
            streams, stream_specs = (xs,), [tile_spec]
        xs, qkv = pl.pallas_call(
            functools.partial(_pre_kernel, n_lat_tiles if l == 0 else None),
            grid=(n_lat_tiles + n_ctx_tiles,),
            in_specs=stream_specs + [
                mod_spec,
                _resident((d, D_FF)), _resident((d, D_FF)), _resident((D_FF, d)),
                _resident((d, IN_WIDTH)),
                pl.BlockSpec((4, tm, LANES), lambda i: (0, rope_block(i), 0)),
                _resident((8, LANES)),
            ],
            out_specs=[tile_spec, pl.BlockSpec((tm, IN_WIDTH), lambda i: (i, 0))],
            out_shape=[jax.ShapeDtypeStruct((t_all, d), F32),
                       jax.ShapeDtypeStruct((t_all, IN_WIDTH), BF16)],
            scratch_shapes=[pltpu.VMEM((tm, D_FF), BF16)],
            compiler_params=_params(("arbitrary",)),
            name=f"pre_l{l}",
        )(*streams, mods[l], w_ffn1_gate[l].astype(BF16), w_ffn1_up[l].astype(BF16),
          w_ffn1_down[l].astype(BF16), w_in_l.astype(BF16), rope, gains)

        ctx_row = t_lat // n_ctx
        sink_perm = sink_logit[l].astype(F32)[jnp.array(GQA_HEAD_ORDER)]
        lamv = jnp.zeros((8, LANES), F32)
        lamv = lamv.at[0, :D_SUB].set(lam_q1[l]).at[1, :D_SUB].set(lam_k1[l])
        lamv = lamv.at[2, :D_SUB].set(lam_q2[l]).at[3, :D_SUB].set(lam_k2[l])
        subln = lane_gain(subln_g[l])
        smem = pl.BlockSpec(memory_space=pltpu.SMEM)

        def lat(cols, col):
            return pl.BlockSpec((seq, cols), lambda b, *_: (b, col))

        def ctxb(cols, col):
            return pl.BlockSpec((n_ctx, cols), lambda b, *_: (ctx_row + b, col))

        def v_ext_scratch(n_blocks):
            lead = () if n_blocks == 1 else (n_blocks,)
            return [pltpu.VMEM(lead + (seq, 2 * LANES), BF16),
                    pltpu.VMEM(lead + (n_ctx, 2 * LANES), BF16)]

        y_a = pl.pallas_call(
            functools.partial(_attn_a_kernel, seq),
            grid=(batch,),
            in_specs=[smem, lat(2 * LANES, COL_AQ // 2), lat(LANES, COL_AK), lat(LANES, COL_AV),
                      ctxb(LANES, COL_AK), ctxb(LANES, COL_AV)],
            out_specs=pl.BlockSpec((seq, GROUP_WIDTH), lambda b: (b, 0)),
            out_shape=jax.ShapeDtypeStruct((t_lat, GROUP_WIDTH), BF16),
            scratch_shapes=(v_ext_scratch(1)
                            + _score_scratch(2 * BLOCK, BLOCK + 2 * WINDOW + n_ctx)
                            + [pltpu.VMEM((2 * BLOCK, LANES), F32)] * 4),
            compiler_params=_params(("arbitrary",)),
            name=f"attn_a_l{l}",
        )(sink_perm, qkv, qkv, qkv, qkv, qkv)

        tq = ATTN_TILE
        blocks_per_seq = seq // (2 * tq)
        n_blocks = batch * blocks_per_seq

        def cur(j):
            return jnp.minimum(j, n_blocks - 1)

        def prev(j):
            return jnp.maximum(j - 1, 0)

        def lagged(q_col, k_cols, k_col, v_col):
            return [pl.BlockSpec((2 * tq, 2 * LANES), lambda j: (cur(j), q_col)),
                    pl.BlockSpec((seq, k_cols), lambda j: (cur(j) // blocks_per_seq, k_col)),
                    pl.BlockSpec((seq, k_cols), lambda j: (prev(j) // blocks_per_seq, v_col)),
                    pl.BlockSpec((n_ctx, k_cols),
                                 lambda j: (ctx_row + cur(j) // blocks_per_seq, k_col)),
                    pl.BlockSpec((n_ctx, k_cols),
                                 lambda j: (ctx_row + prev(j) // blocks_per_seq, v_col))]

        lagged_out = pl.BlockSpec((2 * tq, GROUP_WIDTH), lambda j: (prev(j), 0))

        y_b = pl.pallas_call(
            functools.partial(_attn_b_kernel, blocks_per_seq),
            grid=(n_blocks + 1,),
            in_specs=lagged(COL_BQ // 2, LANES, COL_BK, COL_BV),
            out_specs=lagged_out,
            out_shape=jax.ShapeDtypeStruct((t_lat, GROUP_WIDTH), BF16),
            scratch_shapes=v_ext_scratch(1) + _score_scratch(2 * tq, seq + n_ctx),
            compiler_params=_params(("arbitrary",)),
            name=f"attn_b_l{l}",
        )(qkv, qkv, qkv, qkv, qkv)

        bias = _neighbourhood_bias(rel_pos_bias[l])
        y_c = pl.pallas_call(
            functools.partial(_attn_c_kernel, grid_rows),
            grid=(batch,),
            in_specs=[lat(2 * LANES, COL_CQ // 2),
                      lat(2 * LANES, COL_CK // 2), lat(2 * LANES, COL_CV // 2),
                      ctxb(2 * LANES, COL_CK // 2), ctxb(2 * LANES, COL_CV // 2),
                      _resident(bias.shape)],
            out_specs=pl.BlockSpec((seq, GROUP_WIDTH), lambda b: (b, 0)),
            out_shape=jax.ShapeDtypeStruct((t_lat, GROUP_WIDTH), BF16),
            scratch_shapes=v_ext_scratch(2) + _score_scratch(2 * GRID_W, NA_KR * GRID_W + n_ctx),
            compiler_params=_params(("arbitrary",)),
            name=f"attn_c_l{l}",
        )(qkv, qkv, qkv, qkv, qkv, bias)

        y_d = pl.pallas_call(
            functools.partial(_attn_d_kernel, lam_init, blocks_per_seq),
            grid=(n_blocks + 1,),
            in_specs=([_resident((8, LANES)), _resident((1, LANES))]
                      + lagged(COL_DQ // 2, 2 * LANES, COL_DK // 2, COL_DV // 2)),
            out_specs=lagged_out,
            out_shape=jax.ShapeDtypeStruct((t_lat, GROUP_WIDTH), BF16),
            scratch_shapes=v_ext_scratch(2) + _score_scratch(2 * tq, seq + n_ctx, n_groups=4),
            compiler_params=_params(("arbitrary",)),
            name=f"attn_d_l{l}",
        )(lamv, subln, qkv, qkv, qkv, qkv, qkv)

        w_out_l = _permute_out_heads(_permute_out_heads(w_out[l], 0), GROUP_WIDTH).astype(BF16)
        ffn2 = (w_ffn2_gate[l].astype(BF16), w_ffn2_up[l].astype(BF16), w_ffn2_down[l].astype(BF16))
        ffn2_specs = [_resident((d, D_FF)), _resident((d, D_FF)), _resident((D_FF, d))]
        fng = final_norm_g.astype(F32)[None, :]
        y_spec = pl.BlockSpec((tm, GROUP_WIDTH), lambda i: (jnp.minimum(i, n_lat_tiles - 1), 0))
        post_scratch = [pltpu.VMEM((tm, d), BF16), pltpu.VMEM((tm, D_FF), BF16)]

        if not last:
            y_ctx = pl.pallas_call(
                functools.partial(_attn_ctx_kernel, lam_init),
                grid=(batch,),
                in_specs=[smem, _resident((8, LANES)), _resident((1, LANES)),
                          ctxb(2 * LANES, COL_AQ // 2), ctxb(LANES, COL_AK), ctxb(LANES, COL_AV),
                          ctxb(2 * LANES, COL_BQ // 2), ctxb(LANES, COL_BK), ctxb(LANES, COL_BV),
                          ctxb(2 * LANES, COL_CQ // 2), ctxb(2 * LANES, COL_CK // 2),
                          ctxb(2 * LANES, COL_CV // 2),
                          ctxb(2 * LANES, COL_DQ // 2), ctxb(2 * LANES, COL_DK // 2),
                          ctxb(2 * LANES, COL_DV // 2)],
                out_specs=pl.BlockSpec((n_ctx, d), lambda b: (b, 0)),
                out_shape=jax.ShapeDtypeStruct((t_ctx, d), BF16),
                scratch_shapes=_score_scratch(2 * n_ctx, n_ctx, n_groups=1),
                compiler_params=_params(("arbitrary",)),
                name=f"attn_ctx_l{l}",
            )(sink_perm, lamv, subln, *([qkv] * 12))

            xs = pl.pallas_call(
                functools.partial(_post_kernel, n_lat_tiles, False),
                grid=(n_lat_tiles + n_ctx_tiles,),
                in_specs=[tile_spec, mod_spec, y_spec, y_spec, y_spec, y_spec,
                          pl.BlockSpec((tm, d), lambda i: (jnp.maximum(i - n_lat_tiles, 0), 0)),
                          _resident((d, d))] + ffn2_specs + [_resident((1, d))],
                out_specs=tile_spec,
                out_shape=jax.ShapeDtypeStruct((t_all, d), F32),
                scratch_shapes=post_scratch,
                compiler_params=_params(("arbitrary",)),
                name=f"post_l{l}",
            )(xs, mods[l], y_a, y_b, y_c, y_d, y_ctx, w_out_l, *ffn2, fng)
        else:
            xs = pl.pallas_call(
                functools.partial(_post_kernel, None, True),
                grid=(n_lat_tiles,),
                in_specs=[tile_spec, mod_spec, y_spec, y_spec, y_spec, y_spec,
                          _resident((d, d))] + ffn2_specs + [_resident((1, d))],
                out_specs=tile_spec,
                out_shape=jax.ShapeDtypeStruct((t_lat, d), F32),
                scratch_shapes=post_scratch,
                compiler_params=_params(("arbitrary",)),
                name=f"post_l{l}",
            )(xs, mods[l], y_a, y_b, y_c, y_d, w_out_l, *ffn2, fng)

    return xs.reshape(batch, seq, d)
```

```python
import functools
import math

import jax
import jax.numpy as jnp
from jax import lax
from jax.experimental import pallas as pl
from jax.experimental.pallas import tpu as pltpu

D_MODEL = 1024
GRID_W = 64
HEAD_DIM = 64
N_GROUPS = 4
GROUP_HEADS = D_MODEL // (N_GROUPS * HEAD_DIM)
GROUP_WIDTH = GROUP_HEADS * HEAD_DIM
D_SUB = HEAD_DIM // 2
WINDOW = 128
BLOCK = 128
NA_KR = 8
NA_KC = 16
D_FF = 2816
ROPE_BASE = 10000.0
NORM_EPS = 1e-6
N_MOD = 9
NEG_INF = -1e30
IN_WIDTH = 2560

LANES = 128
MXU_N = 256
TOKEN_TILE = 256
ATTN_TILE = 128
KEY_CHUNK = 512
VMEM_LIMIT = 56 * 1024 * 1024
LOG2E = 1.4426950408889634

BF16 = jnp.bfloat16
F32 = jnp.float32

COL_AQ, COL_AK, COL_AV = 0, 2, 3
COL_BQ, COL_BK, COL_BV = 4, 6, 7
COL_CQ, COL_CK, COL_CV = 8, 10, 12
COL_DQ, COL_DK, COL_DV = 14, 16, 18

GQA_HEAD_ORDER = (0, 2, 1, 3)


def _dot(a, b):
    return jnp.dot(a, b, preferred_element_type=F32)


def _dot_nt(a, b):
    return lax.dot_general(a, b, (((1,), (1,)), ((), ())), preferred_element_type=F32)


def _params(semantics):
    return pltpu.CompilerParams(dimension_semantics=semantics, vmem_limit_bytes=VMEM_LIMIT)


def _resident(shape):
    nd = len(shape)
    return pl.BlockSpec(shape, lambda *_: (0,) * nd, pipeline_mode=pl.Buffered(1))


def _swap_middle_heads(lo, hi):
    lane = lax.broadcasted_iota(jnp.int32, (1, LANES), 1)
    low = lane < HEAD_DIM
    return (jnp.where(low, lo, pltpu.roll(hi, HEAD_DIM, 1)),
            jnp.where(low, pltpu.roll(lo, HEAD_DIM, 1), hi))


def _mods_kernel(c_ref, w_ref, b_ref, o_ref):
    c = c_ref[...]
    a = c * (1.0 / (1.0 + jnp.exp(-c)))
    o_ref[...] = _dot(a.astype(BF16), w_ref[...].astype(BF16)) + b_ref[...]


def _mods(cc, w_ada, b_ada):
    depth, d, n = w_ada.shape
    rows = cc.shape[0]
    tn = 1152
    return pl.pallas_call(
        _mods_kernel,
        grid=(depth, n // tn),
        in_specs=[
            pl.BlockSpec((rows, d), lambda l, j: (0, 0)),
            pl.BlockSpec((None, d, tn), lambda l, j: (l, 0, j)),
            pl.BlockSpec((None, 1, tn), lambda l, j: (l, 0, j)),
        ],
        out_specs=pl.BlockSpec((None, rows, tn), lambda l, j: (l, 0, j)),
        out_shape=jax.ShapeDtypeStruct((depth, rows, n), F32),
        compiler_params=_params(("arbitrary", "arbitrary")),
        name="adaln_mods",
    )(cc, w_ada, b_ada.reshape(depth, 1, n))


def _modulated(x, mod_ref, k):
    shift = mod_ref[0, k:k + 1, :]
    scale = mod_ref[0, k + 1:k + 2, :]
    ms = jnp.mean(x * x, axis=-1, keepdims=True)
    return (x * lax.rsqrt(ms + NORM_EPS)) * (1.0 + scale) + shift


def _swiglu(h, wg_ref, wu_ref, wd_ref, act_ref):
    hb = h.astype(BF16)
    for j in range(D_FF // MXU_N):
        cols = slice(j * MXU_N, (j + 1) * MXU_N)
        g = _dot(hb, wg_ref[:, cols])
        u = _dot(hb, wu_ref[:, cols])
        act_ref[:, cols] = ((g * (1.0 / (1.0 + jnp.exp(-g)))) * u).astype(BF16)
    return _dot(act_ref[...], wd_ref[...])


def _rope(v, cos, sin_signed, quarter):
    lane = lax.broadcasted_iota(jnp.int32, (1, LANES), 1)
    first = (lane & (2 * quarter - 1)) < quarter
    rot = jnp.where(first, pltpu.roll(v, LANES - quarter, 1), pltpu.roll(v, quarter, 1))
    return v * cos + rot * sin_signed


def _head_rms(v, gain):
    lane = lax.broadcasted_iota(jnp.int32, (1, LANES), 1)
    lo = lane < HEAD_DIM
    sq = v * v
    ms_lo = jnp.sum(jnp.where(lo, sq, 0.0), axis=-1, keepdims=True) * (1.0 / HEAD_DIM)
    ms_hi = jnp.sum(jnp.where(lo, 0.0, sq), axis=-1, keepdims=True) * (1.0 / HEAD_DIM)
    rs = jnp.where(lo, lax.rsqrt(ms_lo + NORM_EPS), lax.rsqrt(ms_hi + NORM_EPS))
    return v * rs * gain


_QK_SCALE = HEAD_DIM ** -0.5 * LOG2E
_SUB_SCALE = D_SUB ** -0.5 * LOG2E
_PROJ_BLOCKS = (
    (16, -1, _QK_SCALE), (16, -1, _QK_SCALE), (16, -1, 1.0), (0, -1, 1.0),
    (16, 0, _QK_SCALE), (16, 0, _QK_SCALE), (16, 1, 1.0), (0, -1, 1.0),
    (0, -1, _QK_SCALE), (0, -1, _QK_SCALE), (0, -1, 1.0), (0, -1, 1.0),
    (0, -1, 1.0), (0, -1, 1.0),
    (8, -1, _SUB_SCALE), (8, -1, _SUB_SCALE), (8, -1, 1.0), (8, -1, 1.0),
    (0, -1, 1.0), (0, -1, 1.0),
)


def _project(h, w_ref, rope_ref, gain_ref, o_ref):
    hb = h.astype(BF16)
    for j in range(IN_WIDTH // MXU_N):
        r = _dot(hb, w_ref[:, j * MXU_N:(j + 1) * MXU_N])
        for half in range(MXU_N // LANES):
            blk = j * (MXU_N // LANES) + half
            quarter, gain_idx, scale = _PROJ_BLOCKS[blk]
            v = r[:, half * LANES:(half + 1) * LANES]
            if gain_idx >= 0:
                v = _head_rms(v, gain_ref[gain_idx:gain_idx + 1, :])
            if quarter:
                t = 0 if quarter == 16 else 2
                v = _rope(v, rope_ref[t], rope_ref[t + 1], quarter)
            if scale != 1.0:
                v = v * scale
            o_ref[:, blk * LANES:(blk + 1) * LANES] = v.astype(BF16)


def _pre_kernel(n_lat_tiles, *refs):
    if n_lat_tiles is None:
        (x_ref, mod_ref, wg_ref, wu_ref, wd_ref, win_ref, rope_ref, gain_ref,
         xo_ref, qkv_ref, act_ref) = refs
        x = x_ref[...]
    else:
        (x_ref, c_ref, mod_ref, wg_ref, wu_ref, wd_ref, win_ref, rope_ref, gain_ref,
         xo_ref, qkv_ref, act_ref) = refs
        x = jnp.where(pl.program_id(0) < n_lat_tiles, x_ref[...], c_ref[...])
    y = _swiglu(_modulated(x, mod_ref, 0), wg_ref, wu_ref, wd_ref, act_ref)
    x = x + (0.5 * mod_ref[0, 2:3, :]) * y
    xo_ref[...] = x
    _project(_modulated(x, mod_ref, 3), win_ref, rope_ref, gain_ref, qkv_ref)


def _post_kernel(n_lat_tiles, final_norm, *refs):
    if n_lat_tiles is None:
        (x_ref, mod_ref, ya_ref, yb_ref, yc_ref, yd_ref, wout_ref, wg_ref, wu_ref, wd_ref,
         fng_ref, xo_ref, y_scr, act_ref) = refs
        y_scr[...] = jnp.concatenate([ya_ref[...], yb_ref[...], yc_ref[...], yd_ref[...]], axis=1)
    else:
        (x_ref, mod_ref, ya_ref, yb_ref, yc_ref, yd_ref, yctx_ref, wout_ref, wg_ref, wu_ref,
         wd_ref, fng_ref, xo_ref, y_scr, act_ref) = refs
        is_latent = pl.program_id(0) < n_lat_tiles

        @pl.when(is_latent)
        def _():
            y_scr[...] = jnp.concatenate(
                [ya_ref[...], yb_ref[...], yc_ref[...], yd_ref[...]], axis=1)

        @pl.when(jnp.logical_not(is_latent))
        def _():
            y_scr[...] = yctx_ref[...]

    x = x_ref[...]
    x = x + mod_ref[0, 5:6, :] * _dot(y_scr[...], wout_ref[...])
    y = _swiglu(_modulated(x, mod_ref, 6), wg_ref, wu_ref, wd_ref, act_ref)
    x = x + (0.5 * mod_ref[0, 8:9, :]) * y
    if final_norm:
        ms = jnp.mean(x * x, axis=-1, keepdims=True)
        x = (x * lax.rsqrt(ms + NORM_EPS)) * fng_ref[...]
    xo_ref[...] = x


def _lane_masks(n, dtype):
    lane = lax.broadcasted_iota(jnp.int32, (1, LANES), 1)
    w = LANES // n
    return [jnp.where((lane >= k * w) & (lane < (k + 1) * w), 1.0, 0.0).astype(dtype)
            for k in range(n)]


def _stack_masked(q, masks):
    return jnp.concatenate([q * m for m in masks], axis=0)


def _fill_ext(ext_ref, v):
    ext_ref[:, :LANES] = v
    ext_ref[:, LANES:] = jnp.ones(v.shape, v.dtype)


def _by_parity(refs):
    half = len(refs) // 2
    return (tuple(refs[:half]), tuple(refs[half:]))


def _zero_at_first_step(refs):
    @pl.when(pl.program_id(0) == 0)
    def _():
        for ref in refs:
            ref[...] = jnp.zeros(ref.shape, ref.dtype)


def _key_chunks(lat_ref, ctx_ref, cols=slice(None)):
    n_lat = lat_ref.shape[0]
    return ([lat_ref[r:r + KEY_CHUNK, cols] for r in range(0, n_lat, KEY_CHUNK)]
            + [ctx_ref[:, cols]])


def _scores_stage(lhs, k_chunks, s_ref, m_ref, score_fn=None, floor=None):
    running = None
    off = 0
    for idx, k in enumerate(k_chunks):
        s = _dot_nt(lhs, k)
        if idx == 0 and score_fn is not None:
            s = score_fn(s)
        n = k.shape[0]
        s_ref[:, off:off + n] = s
        for c in range(0, n, LANES):
            tile = s[:, c:c + LANES]
            running = tile if running is None else jnp.maximum(running, tile)
        off += n
    m = jnp.max(running, axis=-1, keepdims=True)
    if floor is not None:
        m = jnp.maximum(m, floor)
    m_ref[...] = jnp.broadcast_to(m, m_ref.shape)


def _values_stage(s_ref, m_ref, v_chunks):
    m = m_ref[...]
    acc = None
    off = 0
    for v in v_chunks:
        n = v.shape[0]
        p = jnp.concatenate([jnp.exp2(s_ref[:, off + c:off + c + LANES] - m)
                             for c in range(0, n, LANES)], axis=1).astype(BF16)
        part = _dot(p, v)
        acc = part if acc is None else acc + part
        off += n
    return acc


def _pair_select(o, n_q):
    lane = lax.broadcasted_iota(jnp.int32, (1, LANES), 1)
    return jnp.where(lane < HEAD_DIM, o[:n_q], o[n_q:2 * n_q])


def _sub_ln(y, gain, post_scale):
    return _head_rms(y, gain) * post_scale


def _lam_value(lamv_ref, lam_init):
    t1 = jnp.sum(lamv_ref[0:1, :] * lamv_ref[1:2, :], axis=-1, keepdims=True)
    t2 = jnp.sum(lamv_ref[2:3, :] * lamv_ref[3:4, :], axis=-1, keepdims=True)
    return jnp.exp(t1) - jnp.exp(t2) + lam_init


def _sink_rows(sink_ref, pair, n_q):
    row = lax.broadcasted_iota(jnp.int32, (2 * n_q, 1), 0)
    return jnp.where(row < n_q, sink_ref[2 * pair], sink_ref[2 * pair + 1]) * LOG2E


def _diff_combine(o, n_q, lam):
    r = o[:, :LANES] / o[:, LANES:]
    return r[:n_q] - lam * r[n_q:]


def _gqa_output(groups):
    return jnp.concatenate(_swap_middle_heads(*groups), axis=1).astype(BF16)


def _bias_kernel(rel_ref, o_ref):
    h = pl.program_id(0)
    n_dr, n_dc = 2 * NA_KR - 1, 2 * NA_KC - 1
    q_col = lax.broadcasted_iota(jnp.int32, (GRID_W, LANES), 0)
    lane = lax.broadcasted_iota(jnp.int32, (GRID_W, LANES), 1)
    k_col = lane & (GRID_W - 1)
    dc = k_col - q_col + (NA_KC - 1)
    c_start = jnp.clip(q_col - NA_KC // 2, 0, GRID_W - NA_KC)
    in_window = (k_col >= c_start) & (k_col < c_start + NA_KC)
    tiles = []
    for dr in range(n_dr):
        t = jnp.zeros((GRID_W, LANES), F32)
        for d in range(n_dc):
            t = jnp.where(dc == d, rel_ref[(h * n_dr + dr) * n_dc + d], t)
        tiles.append(jnp.where(in_window, t * LOG2E, NEG_INF))
    low = lane < GRID_W
    for off in range(NA_KR):
        for c in range(NA_KR * GRID_W // LANES):
            dr = 2 * c - off + NA_KR - 1
            o_ref[0, off, :, c * LANES:(c + 1) * LANES] = jnp.where(low, tiles[dr], tiles[dr + 1])


def _attn_a_kernel(seq, sink_ref, q_ref, k_ref, v_ref, kc_ref, vc_ref, o_ref,
                   vl_ext, vc_ext, *bufs):
    n_q = BLOCK
    span = BLOCK + 2 * WINDOW
    n_blocks = seq // n_q
    masks = _lane_masks(2, BF16)
    s_refs, m_refs = _by_parity(bufs[0:4]), _by_parity(bufs[4:8])
    _zero_at_first_step(bufs)
    _fill_ext(vl_ext, v_ref[...])
    _fill_ext(vc_ext, vc_ref[...])

    def window(n):
        q0 = pl.multiple_of(n * n_q, n_q)
        return q0, pl.multiple_of(jnp.clip(q0 - WINDOW, 0, seq - span), BLOCK)

    def slot(t, parity):
        done, todo = 1 - parity, parity
        q0, start = window(jnp.clip(t - 1, 0, n_blocks - 1))
        v_loc = vl_ext[pl.ds(start, span), :]
        groups = []
        for pair in range(2):
            o = _values_stage(s_refs[done][pair], m_refs[done][pair], [v_loc, vc_ext[...]])
            den = o[:, LANES:] + jnp.exp2(_sink_rows(sink_ref, pair, n_q) - m_refs[done][pair][...])
            groups.append(_pair_select(o[:, :LANES] / den, n_q))
        o_ref[pl.ds(q0, n_q), :] = _gqa_output(groups)
        q0, start = window(jnp.minimum(t, n_blocks - 1))
        q = q_ref[pl.ds(q0, n_q), :]
        k_loc = k_ref[pl.ds(start, span), :]
        q_pos = q0 + (lax.broadcasted_iota(jnp.int32, (2 * n_q, 1), 0) & (n_q - 1))
        k_pos = start + lax.broadcasted_iota(jnp.int32, (1, span), 1)
        valid = jnp.abs(k_pos - q_pos) <= WINDOW
        for pair in range(2):
            lhs = _stack_masked(q[:, pair * LANES:(pair + 1) * LANES], masks)
            _scores_stage(lhs, [k_loc, kc_ref[...]], s_refs[todo][pair], m_refs[todo][pair],
                          score_fn=lambda s: jnp.where(valid, s, NEG_INF),
                          floor=_sink_rows(sink_ref, pair, n_q))

    def two_slots(j, carry):
        slot(2 * j, 0)
        slot(2 * j + 1, 1)
        return carry

    lax.fori_loop(0, n_blocks // 2 + 1, two_slots, 0)


def _dense_slots(n_q, qp_ref, qn_ref, kp_ref, kn_ref, kcp_ref, kcn_ref):
    return ((0, qp_ref, slice(n_q, 2 * n_q), kp_ref, kcp_ref),
            (1, qn_ref, slice(0, n_q), kn_ref, kcn_ref))


def _refill_ext_per_sample(blocks_per_seq, fill):
    pl.when(lax.rem(jnp.maximum(pl.program_id(0) - 1, 0), blocks_per_seq) == 0)(fill)


def _attn_b_kernel(blocks_per_seq, qp_ref, qn_ref, kp_ref, kn_ref, kcp_ref, kcn_ref, v_ref,
                   vc_ref, o_ref, vl_ext, vc_ext, *bufs):
    n_q = o_ref.shape[0] // 2
    s_refs, m_refs = _by_parity(bufs[0:4]), _by_parity(bufs[4:8])
    _zero_at_first_step(bufs)

    def fill():
        _fill_ext(vl_ext, v_ref[...])
        _fill_ext(vc_ext, vc_ref[...])

    _refill_ext_per_sample(blocks_per_seq, fill)
    masks = _lane_masks(2, BF16)
    for parity, q_ref, q_rows, k_ref, kc_ref in _dense_slots(n_q, qp_ref, qn_ref, kp_ref, kn_ref,
                                                            kcp_ref, kcn_ref):
        done, todo = parity, 1 - parity
        groups = []
        for pair in range(2):
            o = _values_stage(s_refs[done][pair], m_refs[done][pair], _key_chunks(vl_ext, vc_ext))
            groups.append(_pair_select(o[:, :LANES] / o[:, LANES:], n_q))
        o_ref[parity * n_q:(parity + 1) * n_q, :] = _gqa_output(groups)
        q = q_ref[q_rows, :]
        for pair in range(2):
            lhs = _stack_masked(q[:, pair * LANES:(pair + 1) * LANES], masks)
            _scores_stage(lhs, _key_chunks(k_ref, kc_ref), s_refs[todo][pair], m_refs[todo][pair])


def _attn_d_kernel(lam_init, blocks_per_seq, lamv_ref, gain_ref, qp_ref, qn_ref, kp_ref, kn_ref,
                   kcp_ref, kcn_ref, v_ref, vc_ref, o_ref, vl_ext, vc_ext, *bufs):
    n_q = o_ref.shape[0] // 2
    s_refs, m_refs = _by_parity(bufs[0:8]), _by_parity(bufs[8:16])
    _zero_at_first_step(bufs)

    def fill():
        for jb in range(2):
            cols = slice(jb * LANES, (jb + 1) * LANES)
            _fill_ext(vl_ext.at[jb], v_ref[:, cols])
            _fill_ext(vc_ext.at[jb], vc_ref[:, cols])

    _refill_ext_per_sample(blocks_per_seq, fill)
    lane = lax.broadcasted_iota(jnp.int32, (1, LANES), 1)
    lam = _lam_value(lamv_ref, lam_init)
    masks = _lane_masks(4, BF16)
    for parity, q_ref, q_rows, k_ref, kc_ref in _dense_slots(n_q, qp_ref, qn_ref, kp_ref, kn_ref,
                                                            kcp_ref, kcn_ref):
        done, todo = parity, 1 - parity
        outs = []
        for jb in range(2):
            ys = [_diff_combine(_values_stage(s_refs[done][2 * jb + head], m_refs[done][2 * jb + head],
                                              _key_chunks(vl_ext.at[jb], vc_ext.at[jb])), n_q, lam)
                  for head in range(2)]
            y = jnp.where(lane < HEAD_DIM, ys[0], ys[1])
            outs.append(_sub_ln(y, gain_ref[...], 1.0 - lam_init))
        o_ref[parity * n_q:(parity + 1) * n_q, :] = jnp.concatenate(outs, axis=1).astype(BF16)
        q = q_ref[q_rows, :]
        for jb in range(2):
            cols = slice(jb * LANES, (jb + 1) * LANES)
            for head in range(2):
                lhs = _stack_masked(q[:, cols], masks[2 * head:2 * head + 2])
                _scores_stage(lhs, _key_chunks(k_ref, kc_ref, cols),
                              s_refs[todo][2 * jb + head], m_refs[todo][2 * jb + head])


def _attn_c_kernel(n_grid_rows, q_ref, k_ref, v_ref, kc_ref, vc_ref, bias_ref,
                   o_ref, vl_ext, vc_ext, *bufs):
    n_q = GRID_W
    n_loc = NA_KR * GRID_W
    masks = _lane_masks(2, BF16)
    s_refs, m_refs = _by_parity(bufs[0:4]), _by_parity(bufs[4:8])
    _zero_at_first_step(bufs)
    for jb in range(2):
        cols = slice(jb * LANES, (jb + 1) * LANES)
        _fill_ext(vl_ext.at[jb], v_ref[:, cols])
        _fill_ext(vc_ext.at[jb], vc_ref[:, cols])

    def geometry(r):
        r_start = jnp.clip(r - NA_KR // 2, 0, n_grid_rows - NA_KR)
        return (pl.multiple_of(r * n_q, n_q), pl.multiple_of(r_start * GRID_W, GRID_W),
                r - r_start)

    def slot(t, parity):
        done, todo = 1 - parity, parity
        q0, k0, _ = geometry(jnp.clip(t - 1, 0, n_grid_rows - 1))
        outs = []
        for jb in range(2):
            o = _values_stage(s_refs[done][jb], m_refs[done][jb],
                              [vl_ext[jb, pl.ds(k0, n_loc), :], vc_ext[jb]])
            outs.append(_pair_select(o[:, :LANES] / o[:, LANES:], n_q))
        o_ref[pl.ds(q0, n_q), :] = jnp.concatenate(outs, axis=1).astype(BF16)
        q0, k0, off = geometry(jnp.minimum(t, n_grid_rows - 1))
        q = q_ref[pl.ds(q0, n_q), :]
        for jb in range(2):
            cols = slice(jb * LANES, (jb + 1) * LANES)
            lhs = _stack_masked(q[:, cols], masks)
            bias = jnp.concatenate([bias_ref[2 * jb, off], bias_ref[2 * jb + 1, off]], axis=0)
            _scores_stage(lhs, [k_ref[pl.ds(k0, n_loc), cols], kc_ref[:, cols]],
                          s_refs[todo][jb], m_refs[todo][jb], score_fn=lambda s: s + bias)

    def two_slots(j, carry):
        slot(2 * j, 0)
        slot(2 * j + 1, 1)
        return carry

    lax.fori_loop(0, n_grid_rows // 2 + 1, two_slots, 0)


def _attn_ctx_kernel(lam_init, sink_ref, lamv_ref, gain_ref, aq_ref, ak_ref, av_ref, bq_ref, bk_ref,
                     bv_ref, cq_ref, ck_ref, cv_ref, dq_ref, dk_ref, dv_ref, o_ref,
                     s0, s1, m0, m1):
    n_q = aq_ref.shape[0]
    m2 = _lane_masks(2, BF16)
    m4 = _lane_masks(4, BF16)
    lane = lax.broadcasted_iota(jnp.int32, (1, LANES), 1)
    sets = ((s0, m0), (s1, m1))

    def ext(v):
        return jnp.concatenate([v, jnp.ones(v.shape, v.dtype)], axis=1)

    def attend(lhs, k, v_ext, s_ref, m_ref, floor=None):
        _scores_stage(lhs, [k], s_ref, m_ref, floor=floor)
        return _values_stage(s_ref, m_ref, [v_ext])

    def out_cols(group, pair):
        c0 = group * GROUP_WIDTH + pair * LANES
        return slice(c0, c0 + LANES)

    for group, (q_ref_, k_ref_, v_ref_) in enumerate(((aq_ref, ak_ref, av_ref),
                                                      (bq_ref, bk_ref, bv_ref))):
        q = q_ref_[...]
        v_ext = ext(v_ref_[...])
        groups = []
        for pair, (s_ref, m_ref) in enumerate(sets):
            lhs = _stack_masked(q[:, pair * LANES:(pair + 1) * LANES], m2)
            sink = _sink_rows(sink_ref, pair, n_q) if group == 0 else None
            o = attend(lhs, k_ref_[...], v_ext, s_ref, m_ref, floor=sink)
            den = o[:, LANES:]
            if group == 0:
                den = den + jnp.exp2(sink - m_ref[...])
            groups.append(_pair_select(o[:, :LANES] / den, n_q))
        o_ref[:, group * GROUP_WIDTH:(group + 1) * GROUP_WIDTH] = _gqa_output(groups)

    q = cq_ref[...]
    for jb, (s_ref, m_ref) in enumerate(sets):
        cols = slice(jb * LANES, (jb + 1) * LANES)
        o = attend(_stack_masked(q[:, cols], m2), ck_ref[:, cols], ext(cv_ref[:, cols]),
                   s_ref, m_ref)
        o_ref[:, out_cols(2, jb)] = _pair_select(o[:, :LANES] / o[:, LANES:], n_q).astype(BF16)

    lam = _lam_value(lamv_ref, lam_init)
    q = dq_ref[...]
    for jb in range(2):
        cols = slice(jb * LANES, (jb + 1) * LANES)
        v_ext = ext(dv_ref[:, cols])
        ys = []
        for head, (s_ref, m_ref) in enumerate(sets):
            lhs = _stack_masked(q[:, cols], m4[2 * head:2 * head + 2])
            ys.append(_diff_combine(attend(lhs, dk_ref[:, cols], v_ext, s_ref, m_ref), n_q, lam))
        y = jnp.where(lane < HEAD_DIM, ys[0], ys[1])
        o_ref[:, out_cols(3, jb)] = _sub_ln(y, gain_ref[...], 1.0 - lam_init).astype(BF16)


def _rope_tables(seq, n_extra):
    t = jnp.arange(seq, dtype=jnp.int32)
    row = (t // GRID_W).astype(F32)
    col = (t % GRID_W).astype(F32)
    tables = []
    for dim in (HEAD_DIM, D_SUB):
        half = dim // 2
        freqs = ROPE_BASE ** (-jnp.arange(0, half, 2, dtype=F32) / half)
        ang_r = row[:, None] * freqs[None, :]
        ang_c = col[:, None] * freqs[None, :]
        ang = jnp.concatenate([ang_r, ang_r, ang_c, ang_c], axis=-1)
        quarter = dim // 4
        sign = jnp.where((jnp.arange(dim) % (2 * quarter)) < quarter, -1.0, 1.0).astype(F32)
        cos = jnp.tile(jnp.cos(ang), (1, LANES // dim))
        sin = jnp.tile(jnp.sin(ang) * sign[None, :], (1, LANES // dim))
        cos = jnp.concatenate([cos, jnp.ones((n_extra, LANES), F32)], axis=0)
        sin = jnp.concatenate([sin, jnp.zeros((n_extra, LANES), F32)], axis=0)
        tables += [cos, sin]
    return jnp.stack(tables)


def _neighbourhood_bias(rel_bias):
    heads = rel_bias.shape[0]
    return pl.pallas_call(
        _bias_kernel,
        grid=(heads,),
        in_specs=[pl.BlockSpec(memory_space=pltpu.SMEM)],
        out_specs=pl.BlockSpec((1, NA_KR, GRID_W, NA_KR * GRID_W), lambda h: (h, 0, 0, 0)),
        out_shape=jax.ShapeDtypeStruct((heads, NA_KR, GRID_W, NA_KR * GRID_W), F32),
        compiler_params=_params(("arbitrary",)),
        name="neighbourhood_bias",
    )(rel_bias.astype(F32).reshape(-1))


def _permute_q_heads(w, col0):
    blocks = [w[:, col0 + h * HEAD_DIM:col0 + (h + 1) * HEAD_DIM] for h in GQA_HEAD_ORDER]
    return jnp.concatenate([w[:, :col0]] + blocks + [w[:, col0 + GROUP_WIDTH:]], axis=1)


def _stage_scratch(n_rows, n_keys, n_groups=2):
    return ([pltpu.VMEM((n_rows, n_keys), F32)] * (2 * n_groups)
            + [pltpu.VMEM((n_rows, LANES), F32)] * (2 * n_groups))


def kernel(x, c, ctx, c_ctx, w_ada, b_ada, w_ffn1_gate, w_ffn1_up, w_ffn1_down, w_in, w_out,
           sink_logit, q_norm_g, k_norm_g, rel_pos_bias, lam_q1, lam_k1, lam_q2, lam_k2, subln_g,
           w_ffn2_gate, w_ffn2_up, w_ffn2_down, final_norm_g):
    batch, seq, d = x.shape
    n_ctx = ctx.shape[1]
    depth = w_ada.shape[0]
    assert d == D_MODEL and seq % TOKEN_TILE == 0 and n_ctx == TOKEN_TILE
    assert seq % GRID_W == 0 and w_in.shape[-1] == IN_WIDTH and seq % KEY_CHUNK == 0
    assert seq % (2 * ATTN_TILE) == 0 and seq % (2 * BLOCK) == 0 and seq % (2 * GRID_W) == 0
    t_lat, t_ctx = batch * seq, batch * n_ctx
    t_all = t_lat + t_ctx
    tm = TOKEN_TILE
    n_lat_tiles, n_ctx_tiles = t_lat // tm, t_ctx // tm
    tiles_per_seq = seq // tm
    grid_rows = seq // GRID_W
    assert grid_rows >= NA_KR and rel_pos_bias.shape[1:] == (GROUP_HEADS, 2 * NA_KR - 1, 2 * NA_KC - 1)

    cc = jnp.concatenate([c, c_ctx[None, :], jnp.zeros((16 - batch - 1, d), F32)], axis=0)
    mods = _mods(cc, w_ada, b_ada).reshape(depth, 16, N_MOD, d)

    def group(i):
        return jnp.minimum(i // tiles_per_seq, batch)

    rope = _rope_tables(seq, tm)

    def rope_block(i):
        return jnp.where(i < n_lat_tiles, i % tiles_per_seq, tiles_per_seq)

    lane_gain = lambda g: jnp.tile(g.astype(F32), LANES // HEAD_DIM)[None, :]

    tile_spec = pl.BlockSpec((tm, d), lambda i: (i, 0))

    for l in range(depth):
        last = l == depth - 1
        lam_init = 0.8 - 0.6 * math.exp(-0.3 * l)
        mod_spec = pl.BlockSpec((1, N_MOD, d), lambda i: (group(i), 0, 0))

        w_in_l = _permute_q_heads(_permute_q_heads(w_in[l], COL_AQ * LANES), COL_BQ * LANES)
        gains =jnp.concatenate([lane_gain(q_norm_g[l]), lane_gain(k_norm_g[l]),
                                 jnp.zeros((6, LANES), F32)], axis=0)
        if l == 0:
            streams = (x.reshape(t_lat, d), ctx.reshape(t_ctx, d))
            stream_specs = [
                pl.BlockSpec((tm, d), lambda i: (jnp.minimum(i, n_lat_tiles - 1), 0)),
                pl.BlockSpec((tm, d), lambda i: (jnp.maximum(i - n_lat_tiles, 0), 0))]
        else:
            streams, stream_specs = (xs,), [tile_spec]
        xs, qkv = pl.pallas_call(
            functools.partial(_pre_kernel, n_lat_tiles if l == 0 else None),
            grid=(n_lat_tiles + n_ctx_tiles,),
            in_specs=stream_specs + [
                mod_spec,
                _resident((d, D_FF)), _resident((d, D_FF)), _resident((D_FF, d)),
                _resident((d, IN_WIDTH)),
                pl.BlockSpec((4, tm, LANES), lambda i: (0, rope_block(i), 0)),
                _resident((8, LANES)),
            ],
            out_specs=[tile_spec, pl.BlockSpec((tm, IN_WIDTH), lambda i: (i, 0))],
            out_shape=[jax.ShapeDtypeStruct((t_all, d), F32),
                       jax.ShapeDtypeStruct((t_all, IN_WIDTH), BF16)],
            scratch_shapes=[pltpu.VMEM((tm, D_FF), BF16)],
            compiler_params=_params(("arbitrary",)),
            name=f"pre_l{l}",
        )(*streams, mods[l], w_ffn1_gate[l].astype(BF16), w_ffn1_up[l].astype(BF16),
          w_ffn1_down[l].astype(BF16), w_in_l.astype(BF16), rope, gains)

        ctx_row = t_lat // n_ctx
        sink_perm = sink_logit[l].astype(F32)[jnp.array(GQA_HEAD_ORDER)]
        lamv = jnp.zeros((8, LANES), F32)
        lamv = lamv.at[0, :D_SUB].set(lam_q1[l]).at[1, :D_SUB].set(lam_k1[l])
        lamv = lamv.at[2, :D_SUB].set(lam_q2[l]).at[3, :D_SUB].set(lam_k2[l])
        subln = lane_gain(subln_g[l])
        smem = pl.BlockSpec(memory_space=pltpu.SMEM)

        def lat(cols, col):
            return pl.BlockSpec((seq, cols), lambda b, *_: (b, col))

        def ctxb(cols, col):
            return pl.BlockSpec((n_ctx, cols), lambda b, *_: (ctx_row + b, col))

        def v_ext_scratch(n_blocks):
            lead = () if n_blocks == 1 else (n_blocks,)
            return [pltpu.VMEM(lead + (seq, 2 * LANES), BF16),
                    pltpu.VMEM(lead + (n_ctx, 2 * LANES), BF16)]

        y_a = pl.pallas_call(
            functools.partial(_attn_a_kernel, seq),
            grid=(batch,),
            in_specs=[smem, lat(2 * LANES, COL_AQ // 2), lat(LANES, COL_AK), lat(LANES, COL_AV),
                      ctxb(LANES, COL_AK), ctxb(LANES, COL_AV)],
            out_specs=pl.BlockSpec((seq, GROUP_WIDTH), lambda b: (b, 0)),
            out_shape=jax.ShapeDtypeStruct((t_lat, GROUP_WIDTH), BF16),
            scratch_shapes=(v_ext_scratch(1)
                            + _stage_scratch(2 * BLOCK, BLOCK + 2 * WINDOW + n_ctx)),
            compiler_params=_params(("arbitrary",)),
            name=f"attn_a_l{l}",
        )(sink_perm, qkv, qkv, qkv, qkv, qkv)

        tq = ATTN_TILE
        blocks_per_seq = seq // (2 * tq)
        n_blocks = batch * blocks_per_seq

        def cur(j):
            return jnp.minimum(j, n_blocks - 1)

        def prev(j):
            return jnp.maximum(j - 1, 0)

        def staggered(q_col, k_cols, k_col, v_col):
            def sample_of(block):
                return lambda j: block(j) // blocks_per_seq
            specs = [pl.BlockSpec((2 * tq, 2 * LANES), lambda j: (prev(j), q_col)),
                     pl.BlockSpec((2 * tq, 2 * LANES), lambda j: (cur(j), q_col))]
            for rows, row0 in ((seq, 0), (n_ctx, ctx_row)):
                for block in (prev, cur):
                    specs.append(pl.BlockSpec(
                        (rows, k_cols), lambda j, b=sample_of(block), r=row0: (r + b(j), k_col)))
            for rows, row0 in ((seq, 0), (n_ctx, ctx_row)):
                specs.append(pl.BlockSpec(
                    (rows, k_cols), lambda j, b=sample_of(prev), r=row0: (r + b(j), v_col)))
            return specs

        lagged_out = pl.BlockSpec((2 * tq, GROUP_WIDTH), lambda j: (prev(j), 0))

        y_b = pl.pallas_call(
            functools.partial(_attn_b_kernel, blocks_per_seq),
            grid=(n_blocks + 1,),
            in_specs=staggered(COL_BQ // 2, LANES, COL_BK, COL_BV),
            out_specs=lagged_out,
            out_shape=jax.ShapeDtypeStruct((t_lat, GROUP_WIDTH), BF16),
            scratch_shapes=v_ext_scratch(1) + _stage_scratch(2 * tq, seq + n_ctx),
            compiler_params=_params(("arbitrary",)),
            name=f"attn_b_l{l}",
        )(*([qkv] * 8))

        y_d = pl.pallas_call(
            functools.partial(_attn_d_kernel, lam_init, blocks_per_seq),
            grid=(n_blocks + 1,),
            in_specs=([_resident((8, LANES)), _resident((1, LANES))]
                      + staggered(COL_DQ // 2, 2 * LANES, COL_DK // 2, COL_DV // 2)),
            out_specs=lagged_out,
            out_shape=jax.ShapeDtypeStruct((t_lat, GROUP_WIDTH), BF16),
            scratch_shapes=v_ext_scratch(2) + _stage_scratch(2 * tq, seq + n_ctx, n_groups=4),
            compiler_params=_params(("arbitrary",)),
            name=f"attn_d_l{l}",
        )(lamv, subln, *([qkv] * 8))

        bias = _neighbourhood_bias(rel_pos_bias[l])
        y_c = pl.pallas_call(
            functools.partial(_attn_c_kernel, grid_rows),
            grid=(batch,),
            in_specs=[lat(2 * LANES, COL_CQ // 2),
                      lat(2 * LANES, COL_CK // 2), lat(2 * LANES, COL_CV // 2),
                      ctxb(2 * LANES, COL_CK // 2), ctxb(2 * LANES, COL_CV // 2),
                      _resident(bias.shape)],
            out_specs=pl.BlockSpec((seq, GROUP_WIDTH), lambda b: (b, 0)),
            out_shape=jax.ShapeDtypeStruct((t_lat, GROUP_WIDTH), BF16),
            scratch_shapes=v_ext_scratch(2) + _stage_scratch(2 * GRID_W, NA_KR * GRID_W + n_ctx),
            compiler_params=_params(("arbitrary",)),
            name=f"attn_c_l{l}",
        )(qkv, qkv, qkv, qkv, qkv, bias)

        ffn2 = (w_ffn2_gate[l].astype(BF16), w_ffn2_up[l].astype(BF16), w_ffn2_down[l].astype(BF16))
        ffn2_specs = [_resident((d, D_FF)), _resident((d, D_FF)), _resident((D_FF, d))]
        fng = final_norm_g.astype(F32)[None, :]
        y_spec = pl.BlockSpec((tm, GROUP_WIDTH), lambda i: (jnp.minimum(i, n_lat_tiles - 1), 0))
        post_scratch = [pltpu.VMEM((tm, d), BF16), pltpu.VMEM((tm, D_FF), BF16)]

        if not last:
            y_ctx = pl.pallas_call(
                functools.partial(_attn_ctx_kernel, lam_init),
                grid=(batch,),
                in_specs=[smem, _resident((8, LANES)), _resident((1, LANES)),
                          ctxb(2 * LANES, COL_AQ // 2), ctxb(LANES, COL_AK), ctxb(LANES, COL_AV),
                          ctxb(2 * LANES, COL_BQ // 2), ctxb(LANES, COL_BK), ctxb(LANES, COL_BV),
                          ctxb(2 * LANES, COL_CQ // 2), ctxb(2 * LANES, COL_CK // 2),
                          ctxb(2 * LANES, COL_CV // 2),
                          ctxb(2 * LANES, COL_DQ // 2), ctxb(2 * LANES, COL_DK // 2),
                          ctxb(2 * LANES, COL_DV // 2)],
                out_specs=pl.BlockSpec((n_ctx, d), lambda b: (b, 0)),
                out_shape=jax.ShapeDtypeStruct((t_ctx, d), BF16),
                scratch_shapes=_stage_scratch(2 * n_ctx, n_ctx, n_groups=1),
                compiler_params=_params(("arbitrary",)),
                name=f"attn_ctx_l{l}",
            )(sink_perm, lamv, subln, *([qkv] * 12))

            xs = pl.pallas_call(
                functools.partial(_post_kernel, n_lat_tiles, False),
                grid=(n_lat_tiles + n_ctx_tiles,),
                in_specs=[tile_spec, mod_spec, y_spec, y_spec, y_spec, y_spec,
                          pl.BlockSpec((tm, d), lambda i: (jnp.maximum(i - n_lat_tiles, 0), 0)),
                          _resident((d, d))] + ffn2_specs + [_resident((1, d))],
                out_specs=tile_spec,
                out_shape=jax.ShapeDtypeStruct((t_all, d), F32),
                scratch_shapes=post_scratch,
                compiler_params=_params(("arbitrary",)),
                name=f"post_l{l}",
            )(xs, mods[l], y_a, y_b, y_c, y_d, y_ctx, w_out[l].astype(BF16), *ffn2, fng)
        else:
            xs = pl.pallas_call(
                functools.partial(_post_kernel, None, True),
                grid=(n_lat_tiles,),
                in_specs=[tile_spec, mod_spec, y_spec, y_spec, y_spec, y_spec,
                          _resident((d, d))] + ffn2_specs + [_resident((1, d))],
                out_specs=tile_spec,
                out_shape=jax.ShapeDtypeStruct((t_lat, d), F32),
                scratch_shapes=post_scratch,
                compiler_params=_params(("arbitrary",)),
                name=f"post_l{l}",
            )(xs, mods[l], y_a, y_b, y_c, y_d, w_out[l].astype(BF16), *ffn2, fng)

    return xs.reshape(batch, seq, d)
```

```python
import functools
import math

import numpy as np
import jax
import jax.numpy as jnp
from jax import lax
from jax.experimental import pallas as pl
from jax.experimental.pallas import tpu as pltpu

D_MODEL = 1024
GRID_W = 64
HEAD_DIM = 64
N_GROUPS = 4
GROUP_HEADS = D_MODEL // (N_GROUPS * HEAD_DIM)
GROUP_WIDTH = GROUP_HEADS * HEAD_DIM
D_SUB = HEAD_DIM // 2
WINDOW = 128
BLOCK = 128
NA_KR = 8
NA_KC = 16
D_FF = 2816
ROPE_BASE = 10000.0
NORM_EPS = 1e-6
N_MOD = 9
NEG_INF = -1e30
IN_WIDTH = 2560

LANES = 128
MXU_N = 256
TOKEN_TILE = 512
ATTN_TILE = 128
KEY_CHUNK = 512
VMEM_LIMIT = 56 * 1024 * 1024
LOG2E = 1.4426950408889634

BF16 = jnp.bfloat16
F32 = jnp.float32

COL_AQ, COL_AK, COL_AV = 0, 2, 3
COL_BQ, COL_BK, COL_BV = 4, 6, 7
COL_CQ, COL_CK, COL_CV = 8, 10, 12
COL_DQ, COL_DK, COL_DV = 14, 16, 18

GQA_HEAD_ORDER = (0, 2, 1, 3)


def _dot(a, b):
    return jnp.dot(a, b, preferred_element_type=F32)


def _dot_nt(a, b):
    return lax.dot_general(a, b, (((1,), (1,)), ((), ())), preferred_element_type=F32)


def _params(semantics):
    return pltpu.CompilerParams(dimension_semantics=semantics, vmem_limit_bytes=VMEM_LIMIT)


def _resident(shape, layer=None):
    nd = len(shape)
    if layer is None:
        return pl.BlockSpec(shape, lambda *_: (0,) * nd, pipeline_mode=pl.Buffered(1))
    return pl.BlockSpec((None,) + tuple(shape), lambda *_: (layer,) + (0,) * nd,
                        pipeline_mode=pl.Buffered(1))


def _swap_middle_heads(lo, hi):
    lane = lax.broadcasted_iota(jnp.int32, (1, LANES), 1)
    low = lane < HEAD_DIM
    return (jnp.where(low, lo, pltpu.roll(hi, HEAD_DIM, 1)),
            jnp.where(low, pltpu.roll(lo, HEAD_DIM, 1), hi))


CAST_BLOCK_BYTES = 2 * 1024 * 1024


def _cast_kernel(swap_cols, w_ref, o_ref):
    o_ref[...] = w_ref[...].astype(BF16)
    for c0 in swap_cols:
        lo, hi = _swap_middle_heads(w_ref[:, c0:c0 + LANES], w_ref[:, c0 + LANES:c0 + 2 * LANES])
        o_ref[:, c0:c0 + LANES] = lo.astype(BF16)
        o_ref[:, c0 + LANES:c0 + 2 * LANES] = hi.astype(BF16)


def _to_bf16(w, swap_cols=()):
    depth, r, c = w.shape
    tr = max(t for t in range(16, r + 1, 16) if r % t == 0 and t * c * 4 <= CAST_BLOCK_BYTES)
    spec = pl.BlockSpec((None, tr, c), lambda l, i: (l, i, 0))
    return pl.pallas_call(
        functools.partial(_cast_kernel, tuple(swap_cols)),
        grid=(depth, r // tr),
        in_specs=[spec],
        out_specs=spec,
        out_shape=jax.ShapeDtypeStruct(w.shape, BF16),
        compiler_params=_params(("arbitrary", "arbitrary")),
        name="cast_bf16",
    )(w)


def _mods_kernel(c_ref, w_ref, b_ref, o_ref):
    c = c_ref[...]
    a = c * (1.0 / (1.0 + jnp.exp(-c)))
    o_ref[...] = _dot(a.astype(BF16), w_ref[...].astype(BF16)) + b_ref[...]


def _mods(cc, w_ada, b_ada):
    depth, d, n = w_ada.shape
    rows = cc.shape[0]
    tn = 1152
    return pl.pallas_call(
        _mods_kernel,
        grid=(depth, n // tn),
        in_specs=[
            pl.BlockSpec((rows, d), lambda l, j: (0, 0)),
            pl.BlockSpec((None, d, tn), lambda l, j: (l, 0, j)),
            pl.BlockSpec((None, 1, tn), lambda l, j: (l, 0, j)),
        ],
        out_specs=pl.BlockSpec((None, rows, tn), lambda l, j: (l, 0, j)),
        out_shape=jax.ShapeDtypeStruct((depth, rows, n), F32),
        compiler_params=_params(("arbitrary", "arbitrary")),
        name="adaln_mods",
    )(cc, w_ada, b_ada.reshape(depth, 1, n))


def _modulated(x, mod_ref, k):
    shift = mod_ref[0, k:k + 1, :]
    scale = mod_ref[0, k + 1:k + 2, :]
    ms = jnp.mean(x * x, axis=-1, keepdims=True)
    return (x * lax.rsqrt(ms + NORM_EPS)) * (1.0 + scale) + shift


def _swiglu(h, wg_ref, wu_ref, wd_ref, act_ref):
    hb = h.astype(BF16)
    for j in range(D_FF // MXU_N):
        cols = slice(j * MXU_N, (j + 1) * MXU_N)
        g = _dot(hb, wg_ref[:, cols])
        u = _dot(hb, wu_ref[:, cols])
        act_ref[:, cols] = ((g * (1.0 / (1.0 + jnp.exp(-g)))) * u).astype(BF16)
    return _dot(act_ref[...], wd_ref[...])


def _rope(v, cos, sin_signed, quarter):
    lane = lax.broadcasted_iota(jnp.int32, (1, LANES), 1)
    first = (lane & (2 * quarter - 1)) < quarter
    rot = jnp.where(first, pltpu.roll(v, LANES - quarter, 1), pltpu.roll(v, quarter, 1))
    return v * cos + rot * sin_signed


def _head_rms(v, gain):
    lane = lax.broadcasted_iota(jnp.int32, (1, LANES), 1)
    lo = lane < HEAD_DIM
    sq = v * v
    ms_lo = jnp.sum(jnp.where(lo, sq, 0.0), axis=-1, keepdims=True) * (1.0 / HEAD_DIM)
    ms_hi = jnp.sum(jnp.where(lo, 0.0, sq), axis=-1, keepdims=True) * (1.0 / HEAD_DIM)
    rs = jnp.where(lo, lax.rsqrt(ms_lo + NORM_EPS), lax.rsqrt(ms_hi + NORM_EPS))
    return v * rs * gain


_QK_SCALE = HEAD_DIM ** -0.5 * LOG2E
_SUB_SCALE = D_SUB ** -0.5 * LOG2E
_PROJ_BLOCKS = (
    (16, -1, _QK_SCALE), (16, -1, _QK_SCALE), (16, -1, 1.0), (0, -1, 1.0),
    (16, 0, _QK_SCALE), (16, 0, _QK_SCALE), (16, 1, 1.0), (0, -1, 1.0),
    (0, -1, _QK_SCALE), (0, -1, _QK_SCALE), (0, -1, 1.0), (0, -1, 1.0),
    (0, -1, 1.0), (0, -1, 1.0),
    (8, -1, _SUB_SCALE), (8, -1, _SUB_SCALE), (8, -1, 1.0), (8, -1, 1.0),
    (0, -1, 1.0), (0, -1, 1.0),
)


def _project(h, w_ref, rope_ref, gain_ref, o_ref):
    hb = h.astype(BF16)
    for j in range(IN_WIDTH // MXU_N):
        r = _dot(hb, w_ref[:, j * MXU_N:(j + 1) * MXU_N])
        for half in range(MXU_N // LANES):
            blk = j * (MXU_N // LANES) + half
            quarter, gain_idx, scale = _PROJ_BLOCKS[blk]
            v = r[:, half * LANES:(half + 1) * LANES]
            if gain_idx >= 0:
                v = _head_rms(v, gain_ref[gain_idx:gain_idx + 1, :])
            if quarter:
                t = 0 if quarter == 16 else 2
                v = _rope(v, rope_ref[t], rope_ref[t + 1], quarter)
            if scale != 1.0:
                v = v * scale
            o_ref[:, blk * LANES:(blk + 1) * LANES] = v.astype(BF16)


def _pre_kernel(n_lat_tiles, *refs):
    if n_lat_tiles is None:
        (x_ref, mod_ref, wg_ref, wu_ref, wd_ref, win_ref, rope_ref, gain_ref,
         xo_ref, qkv_ref, act_ref) = refs
        x = x_ref[...]
    else:
        (x_ref, c_ref, mod_ref, wg_ref, wu_ref, wd_ref, win_ref, rope_ref, gain_ref,
         xo_ref, qkv_ref, act_ref) = refs
        x = jnp.where(pl.program_id(0) < n_lat_tiles, x_ref[...], c_ref[...])
    y = _swiglu(_modulated(x, mod_ref, 0), wg_ref, wu_ref, wd_ref, act_ref)
    x = x + (0.5 * mod_ref[0, 2:3, :]) * y
    xo_ref[...] = x
    _project(_modulated(x, mod_ref, 3), win_ref, rope_ref, gain_ref, qkv_ref)


def _post_kernel(n_lat_tiles, final_norm, *refs):
    if n_lat_tiles is None:
        (x_ref, mod_ref, ya_ref, yb_ref, yc_ref, yd_ref, wout_ref, wg_ref, wu_ref, wd_ref,
         fng_ref, xo_ref, y_scr, act_ref) = refs
        y_scr[...] = jnp.concatenate([ya_ref[...], yb_ref[...], yc_ref[...], yd_ref[...]], axis=1)
    else:
        (x_ref, mod_ref, ya_ref, yb_ref, yc_ref, yd_ref, yctx_ref, wout_ref, wg_ref, wu_ref,
         wd_ref, fng_ref, xo_ref, y_scr, act_ref) = refs
        is_latent = pl.program_id(0) < n_lat_tiles

        @pl.when(is_latent)
        def _():
            y_scr[...] = jnp.concatenate(
                [ya_ref[...], yb_ref[...], yc_ref[...], yd_ref[...]], axis=1)

        @pl.when(jnp.logical_not(is_latent))
        def _():
            y_scr[...] = yctx_ref[...]

    x = x_ref[...]
    x = x + mod_ref[0, 5:6, :] * _dot(y_scr[...], wout_ref[...])
    y = _swiglu(_modulated(x, mod_ref, 6), wg_ref, wu_ref, wd_ref, act_ref)
    x = x + (0.5 * mod_ref[0, 8:9, :]) * y
    if final_norm:
        ms = jnp.mean(x * x, axis=-1, keepdims=True)
        x = (x * lax.rsqrt(ms + NORM_EPS)) * fng_ref[...]
    xo_ref[...] = x


def _lane_masks(n, dtype):
    lane = lax.broadcasted_iota(jnp.int32, (1, LANES), 1)
    w = LANES // n
    return [jnp.where((lane >= k * w) & (lane < (k + 1) * w), 1.0, 0.0).astype(dtype)
            for k in range(n)]


def _stack_masked(q, masks):
    return jnp.concatenate([q * m for m in masks], axis=0)


def _fill_ext(ext_ref, v):
    ext_ref[:, :LANES] = v
    ext_ref[:, LANES:] = jnp.ones(v.shape, v.dtype)


def _by_parity(refs):
    half = len(refs) // 2
    return (tuple(refs[:half]), tuple(refs[half:]))


def _zero_at_first_step(refs):
    @pl.when(pl.program_id(0) == 0)
    def _():
        for ref in refs:
            ref[...] = jnp.zeros(ref.shape, ref.dtype)


ROW_BLOCK = 32


def _prime_loop_pipeline(s_refs, p_refs, m_refs=()):
    @pl.when(pl.program_id(0) == 0)
    def _():
        for ref in tuple(s_refs) + tuple(m_refs):
            ref[...] = jnp.zeros(ref.shape, ref.dtype)
        for ref in p_refs:
            ref[...] = jnp.ones(ref.shape, ref.dtype)


def _run_loop_slots(n_tiles, slot):
    def two_slots(j, carry):
        slot(2 * j, 0, 1)
        slot(2 * j + 1, 1, 0)
        return carry

    lax.fori_loop(0, n_tiles // 2 + 1, two_slots, 0)


def _scores(lhs, k_list, s_ref, score_fn=None):
    off = 0
    for idx, k in enumerate(k_list):
        s = _dot_nt(lhs, k)
        if idx == 0 and score_fn is not None:
            s = score_fn(s)
        s_ref[:, off:off + k.shape[0]] = s
        off += k.shape[0]


def _numerators(s_ref, p_ref, floor=None):
    maxima = []
    for r in range(0, s_ref.shape[0], ROW_BLOCK):
        rows = slice(r, r + ROW_BLOCK)
        m = jnp.max(s_ref[rows, :], axis=-1, keepdims=True)
        if floor is not None:
            m = jnp.maximum(m, floor[rows])
        p_ref[rows, :] = jnp.exp2(s_ref[rows, :] - m).astype(BF16)
        maxima.append(m)
    return jnp.concatenate(maxima, axis=0)


def _weighted_values(p_ref, v_list):
    o = None
    off = 0
    for v in v_list:
        part = _dot(p_ref[:, off:off + v.shape[0]], v)
        o = part if o is None else o + part
        off += v.shape[0]
    return o


def _key_chunks(lat_ref, ctx_ref, cols=slice(None)):
    n_lat = lat_ref.shape[0]
    return ([lat_ref[r:r + KEY_CHUNK, cols] for r in range(0, n_lat, KEY_CHUNK)]
            + [ctx_ref[:, cols]])


def _scores_stage(lhs, k_chunks, s_ref, m_ref, score_fn=None, floor=None):
    running = None
    off = 0
    for idx, k in enumerate(k_chunks):
        s = _dot_nt(lhs, k)
        if idx == 0 and score_fn is not None:
            s = score_fn(s)
        n = k.shape[0]
        s_ref[:, off:off + n] = s
        for c in range(0, n, LANES):
            tile = s[:, c:c + LANES]
            running = tile if running is None else jnp.maximum(running, tile)
        off += n
    m = jnp.max(running, axis=-1, keepdims=True)
    if floor is not None:
        m = jnp.maximum(m, floor)
    m_ref[...] = jnp.broadcast_to(m, m_ref.shape)


def _values_stage(s_ref, m_ref, v_chunks):
    m = m_ref[...]
    acc = None
    off = 0
    for v in v_chunks:
        n = v.shape[0]
        p = jnp.concatenate([jnp.exp2(s_ref[:, off + c:off + c + LANES] - m)
                             for c in range(0, n, LANES)], axis=1).astype(BF16)
        part = _dot(p, v)
        acc = part if acc is None else acc + part
        off += n
    return acc


def _pair_select(o, n_q):
    lane = lax.broadcasted_iota(jnp.int32, (1, LANES), 1)
    return jnp.where(lane < HEAD_DIM, o[:n_q], o[n_q:2 * n_q])


def _sub_ln(y, gain, post_scale):
    return _head_rms(y, gain) * post_scale


def _lam_value(lamv_ref, lam_init):
    t1 = jnp.sum(lamv_ref[0:1, :] * lamv_ref[1:2, :], axis=-1, keepdims=True)
    t2 = jnp.sum(lamv_ref[2:3, :] * lamv_ref[3:4, :], axis=-1, keepdims=True)
    return jnp.exp(t1) - jnp.exp(t2) + lam_init


def _sink_rows(sink_ref, pair, n_q):
    row = lax.broadcasted_iota(jnp.int32, (2 * n_q, 1), 0)
    return jnp.where(row < n_q, sink_ref[2 * pair], sink_ref[2 * pair + 1]) * LOG2E


def _diff_combine(o, n_q, lam):
    r = o[:, :LANES] / o[:, LANES:]
    return r[:n_q] - lam * r[n_q:]


def _gqa_output(groups):
    return jnp.concatenate(_swap_middle_heads(*groups), axis=1).astype(BF16)


def _bias_kernel(rel_ref, o_ref):
    h = pl.program_id(0)
    n_dr, n_dc = 2 * NA_KR - 1, 2 * NA_KC - 1
    q_col = lax.broadcasted_iota(jnp.int32, (GRID_W, LANES), 0)
    lane = lax.broadcasted_iota(jnp.int32, (GRID_W, LANES), 1)
    k_col = lane & (GRID_W - 1)
    dc = k_col - q_col + (NA_KC - 1)
    c_start = jnp.clip(q_col - NA_KC // 2, 0, GRID_W - NA_KC)
    in_window = (k_col >= c_start) & (k_col < c_start + NA_KC)
    tiles = []
    for dr in range(n_dr):
        t = jnp.zeros((GRID_W, LANES), F32)
        for d in range(n_dc):
            t = jnp.where(dc == d, rel_ref[(h * n_dr + dr) * n_dc + d], t)
        tiles.append(jnp.where(in_window, t * LOG2E, NEG_INF))
    low = lane < GRID_W
    for off in range(NA_KR):
        for c in range(NA_KR * GRID_W // LANES):
            dr = 2 * c - off + NA_KR - 1
            o_ref[0, off, :, c * LANES:(c + 1) * LANES] = jnp.where(low, tiles[dr], tiles[dr + 1])


def _attn_a_kernel(seq, sink_ref, q_ref, k_ref, v_ref, kc_ref, vc_ref, o_ref,
                   vl_ext, vc_ext, *bufs):
    n_q = BLOCK
    span = BLOCK + 2 * WINDOW
    n_blocks = seq // n_q
    masks = _lane_masks(2, BF16)
    s_refs, p_refs, m_refs = (_by_parity(bufs[i:i + 4]) for i in (0, 4, 8))
    _prime_loop_pipeline(bufs[0:4], bufs[4:8], bufs[8:12])
    _fill_ext(vl_ext, v_ref[...])
    _fill_ext(vc_ext, vc_ref[...])

    def window(n):
        q0 = pl.multiple_of(n * n_q, n_q)
        return q0, pl.multiple_of(jnp.clip(q0 - WINDOW, 0, seq - span), BLOCK)

    def slot(t, new, old):
        q0, start = window(jnp.clip(t - 2, 0, n_blocks - 1))
        v_loc = vl_ext[pl.ds(start, span), :]
        groups = []
        for pair in range(2):
            o = _weighted_values(p_refs[new][pair], [v_loc, vc_ext[...]])
            den = o[:, LANES:] + jnp.exp2(_sink_rows(sink_ref, pair, n_q) - m_refs[new][pair][...])
            groups.append(_pair_select(o[:, :LANES] / den, n_q))
        o_ref[pl.ds(q0, n_q), :] = _gqa_output(groups)
        for pair in range(2):
            m = _numerators(s_refs[old][pair], p_refs[old][pair],
                            floor=_sink_rows(sink_ref, pair, n_q))
            m_refs[old][pair][...] = jnp.broadcast_to(m, (2 * n_q, LANES))
        q0, start = window(jnp.minimum(t, n_blocks - 1))
        q = q_ref[pl.ds(q0, n_q), :]
        k_loc = k_ref[pl.ds(start, span), :]
        q_pos = q0 + (lax.broadcasted_iota(jnp.int32, (2 * n_q, 1), 0) & (n_q - 1))
        k_pos = start + lax.broadcasted_iota(jnp.int32, (1, span), 1)
        valid = jnp.abs(k_pos - q_pos) <= WINDOW
        for pair in range(2):
            lhs = _stack_masked(q[:, pair * LANES:(pair + 1) * LANES], masks)
            _scores(lhs, [k_loc, kc_ref[...]], s_refs[new][pair],
                    score_fn=lambda s: jnp.where(valid, s, NEG_INF))

    _run_loop_slots(n_blocks, slot)


def _dense_slots(n_q, qp_ref, qn_ref, kp_ref, kn_ref, kcp_ref, kcn_ref):
    return ((0, qp_ref, slice(n_q, 2 * n_q), kp_ref, kcp_ref),
            (1, qn_ref, slice(0, n_q), kn_ref, kcn_ref))


def _refill_ext_per_sample(blocks_per_seq, fill):
    pl.when(lax.rem(jnp.maximum(pl.program_id(0) - 1, 0), blocks_per_seq) == 0)(fill)


def _attn_b_kernel(blocks_per_seq, qp_ref, qn_ref, kp_ref, kn_ref, kcp_ref, kcn_ref, v_ref,
                   vc_ref, o_ref, vl_ext, vc_ext, *bufs):
    n_q = o_ref.shape[0] // 2
    s_refs, m_refs = _by_parity(bufs[0:4]), _by_parity(bufs[4:8])
    _zero_at_first_step(bufs)

    def fill():
        _fill_ext(vl_ext, v_ref[...])
        _fill_ext(vc_ext, vc_ref[...])

    _refill_ext_per_sample(blocks_per_seq, fill)
    masks = _lane_masks(2, BF16)
    for parity, q_ref, q_rows, k_ref, kc_ref in _dense_slots(n_q, qp_ref, qn_ref, kp_ref, kn_ref,
                                                            kcp_ref, kcn_ref):
        done, todo = parity, 1 - parity
        groups = []
        for pair in range(2):
            o = _values_stage(s_refs[done][pair], m_refs[done][pair], _key_chunks(vl_ext, vc_ext))
            groups.append(_pair_select(o[:, :LANES] / o[:, LANES:], n_q))
        o_ref[parity * n_q:(parity + 1) * n_q, :] = _gqa_output(groups)
        q = q_ref[q_rows, :]
        for pair in range(2):
            lhs = _stack_masked(q[:, pair * LANES:(pair + 1) * LANES], masks)
            _scores_stage(lhs, _key_chunks(k_ref, kc_ref), s_refs[todo][pair], m_refs[todo][pair])


def _attn_d_kernel(lam_init, blocks_per_seq, lamv_ref, gain_ref, qp_ref, qn_ref, kp_ref, kn_ref,
                   kcp_ref, kcn_ref, v_ref, vc_ref, o_ref, vl_ext, vc_ext, *bufs):
    n_q = o_ref.shape[0] // 2
    s_refs, m_refs = _by_parity(bufs[0:8]), _by_parity(bufs[8:16])
    _zero_at_first_step(bufs)

    def fill():
        for jb in range(2):
            cols = slice(jb * LANES, (jb + 1) * LANES)
            _fill_ext(vl_ext.at[jb], v_ref[:, cols])
            _fill_ext(vc_ext.at[jb], vc_ref[:, cols])

    _refill_ext_per_sample(blocks_per_seq, fill)
    lane = lax.broadcasted_iota(jnp.int32, (1, LANES), 1)
    lam = _lam_value(lamv_ref, lam_init)
    masks = _lane_masks(4, BF16)
    for parity, q_ref, q_rows, k_ref, kc_ref in _dense_slots(n_q, qp_ref, qn_ref, kp_ref, kn_ref,
                                                            kcp_ref, kcn_ref):
        done, todo = parity, 1 - parity
        outs = []
        for jb in range(2):
            ys = [_diff_combine(_values_stage(s_refs[done][2 * jb + head], m_refs[done][2 * jb + head],
                                              _key_chunks(vl_ext.at[jb], vc_ext.at[jb])), n_q, lam)
                  for head in range(2)]
            y = jnp.where(lane < HEAD_DIM, ys[0], ys[1])
            outs.append(_sub_ln(y, gain_ref[...], 1.0 - lam_init))
        o_ref[parity * n_q:(parity + 1) * n_q, :] = jnp.concatenate(outs, axis=1).astype(BF16)
        q = q_ref[q_rows, :]
        for jb in range(2):
            cols = slice(jb * LANES, (jb + 1) * LANES)
            for head in range(2):
                lhs = _stack_masked(q[:, cols], masks[2 * head:2 * head + 2])
                _scores_stage(lhs, _key_chunks(k_ref, kc_ref, cols),
                              s_refs[todo][2 * jb + head], m_refs[todo][2 * jb + head])


def _attn_c_kernel(n_grid_rows, q_ref, k_ref, v_ref, kc_ref, vc_ref, bias_ref,
                   o_ref, vl_ext, vc_ext, *bufs):
    n_q = GRID_W
    n_loc = NA_KR * GRID_W
    masks = _lane_masks(2, BF16)
    s_refs, p_refs = _by_parity(bufs[0:4]), _by_parity(bufs[4:8])
    _prime_loop_pipeline(bufs[0:4], bufs[4:8])
    for jb in range(2):
        cols = slice(jb * LANES, (jb + 1) * LANES)
        _fill_ext(vl_ext.at[jb], v_ref[:, cols])
        _fill_ext(vc_ext.at[jb], vc_ref[:, cols])

    def geometry(r):
        r_start = jnp.clip(r - NA_KR // 2, 0, n_grid_rows - NA_KR)
        return (pl.multiple_of(r * n_q, n_q), pl.multiple_of(r_start * GRID_W, GRID_W),
                r - r_start)

    def slot(t, new, old):
        q0, k0, _ = geometry(jnp.clip(t - 2, 0, n_grid_rows - 1))
        outs = []
        for jb in range(2):
            o = _weighted_values(p_refs[new][jb], [vl_ext[jb, pl.ds(k0, n_loc), :], vc_ext[jb]])
            outs.append(_pair_select(o[:, :LANES] / o[:, LANES:], n_q))
        o_ref[pl.ds(q0, n_q), :] = jnp.concatenate(outs, axis=1).astype(BF16)
        for jb in range(2):
            _numerators(s_refs[old][jb], p_refs[old][jb])
        q0, k0, off = geometry(jnp.minimum(t, n_grid_rows - 1))
        q = q_ref[pl.ds(q0, n_q), :]
        for jb in range(2):
            cols = slice(jb * LANES, (jb + 1) * LANES)
            lhs = _stack_masked(q[:, cols], masks)
            bias = jnp.concatenate([bias_ref[2 * jb, off], bias_ref[2 * jb + 1, off]], axis=0)
            _scores(lhs, [k_ref[pl.ds(k0, n_loc), cols], kc_ref[:, cols]], s_refs[new][jb],
                    score_fn=lambda s: s + bias)

    _run_loop_slots(n_grid_rows, slot)


def _attn_ctx_kernel(lam_init, sink_ref, lamv_ref, gain_ref, aq_ref, ak_ref, av_ref, bq_ref, bk_ref,
                     bv_ref, cq_ref, ck_ref, cv_ref, dq_ref, dk_ref, dv_ref, o_ref,
                     s0, s1, m0, m1):
    n_q = aq_ref.shape[0]
    m2 = _lane_masks(2, BF16)
    m4 = _lane_masks(4, BF16)
    lane = lax.broadcasted_iota(jnp.int32, (1, LANES), 1)
    sets = ((s0, m0), (s1, m1))

    def ext(v):
        return jnp.concatenate([v, jnp.ones(v.shape, v.dtype)], axis=1)

    def attend(lhs, k, v_ext, s_ref, m_ref, floor=None):
        _scores_stage(lhs, [k], s_ref, m_ref, floor=floor)
        return _values_stage(s_ref, m_ref, [v_ext])

    def out_cols(group, pair):
        c0 = group * GROUP_WIDTH + pair * LANES
        return slice(c0, c0 + LANES)

    for group, (q_ref_, k_ref_, v_ref_) in enumerate(((aq_ref, ak_ref, av_ref),
                                                      (bq_ref, bk_ref, bv_ref))):
        q = q_ref_[...]
        v_ext = ext(v_ref_[...])
        groups = []
        for pair, (s_ref, m_ref) in enumerate(sets):
            lhs = _stack_masked(q[:, pair * LANES:(pair + 1) * LANES], m2)
            sink = _sink_rows(sink_ref, pair, n_q) if group == 0 else None
            o = attend(lhs, k_ref_[...], v_ext, s_ref, m_ref, floor=sink)
            den = o[:, LANES:]
            if group == 0:
                den = den + jnp.exp2(sink - m_ref[...])
            groups.append(_pair_select(o[:, :LANES] / den, n_q))
        o_ref[:, group * GROUP_WIDTH:(group + 1) * GROUP_WIDTH] = _gqa_output(groups)

    q = cq_ref[...]
    for jb, (s_ref, m_ref) in enumerate(sets):
        cols = slice(jb * LANES, (jb + 1) * LANES)
        o = attend(_stack_masked(q[:, cols], m2), ck_ref[:, cols], ext(cv_ref[:, cols]),
                   s_ref, m_ref)
        o_ref[:, out_cols(2, jb)] = _pair_select(o[:, :LANES] / o[:, LANES:], n_q).astype(BF16)

    lam = _lam_value(lamv_ref, lam_init)
    q = dq_ref[...]
    for jb in range(2):
        cols = slice(jb * LANES, (jb + 1) * LANES)
        v_ext = ext(dv_ref[:, cols])
        ys = []
        for head, (s_ref, m_ref) in enumerate(sets):
            lhs = _stack_masked(q[:, cols], m4[2 * head:2 * head + 2])
            ys.append(_diff_combine(attend(lhs, dk_ref[:, cols], v_ext, s_ref, m_ref), n_q, lam))
        y = jnp.where(lane < HEAD_DIM, ys[0], ys[1])
        o_ref[:, out_cols(3, jb)] = _sub_ln(y, gain_ref[...], 1.0 - lam_init).astype(BF16)


def _rope_tables(seq, n_extra):
    t = np.arange(seq)
    row = (t // GRID_W).astype(np.float64)
    col = (t % GRID_W).astype(np.float64)
    tables = []
    for dim in (HEAD_DIM, D_SUB):
        half = dim // 2
        freqs = ROPE_BASE ** (-np.arange(0, half, 2, dtype=np.float64) / half)
        ang_r = row[:, None] * freqs[None, :]
        ang_c = col[:, None] * freqs[None, :]
        ang = np.concatenate([ang_r, ang_r, ang_c, ang_c], axis=-1)
        quarter = dim // 4
        sign = np.where((np.arange(dim) % (2 * quarter)) < quarter, -1.0, 1.0)
        cos = np.tile(np.cos(ang), (1, LANES // dim))
        sin = np.tile(np.sin(ang) * sign[None, :], (1, LANES // dim))
        cos = np.concatenate([cos, np.ones((n_extra, LANES))], axis=0)
        sin = np.concatenate([sin, np.zeros((n_extra, LANES))], axis=0)
        tables += [cos, sin]
    return jnp.asarray(np.stack(tables), dtype=F32)


def _neighbourhood_bias(rel_bias):
    heads = rel_bias.shape[0]
    return pl.pallas_call(
        _bias_kernel,
        grid=(heads,),
        in_specs=[pl.BlockSpec(memory_space=pltpu.SMEM)],
        out_specs=pl.BlockSpec((1, NA_KR, GRID_W, NA_KR * GRID_W), lambda h: (h, 0, 0, 0)),
        out_shape=jax.ShapeDtypeStruct((heads, NA_KR, GRID_W, NA_KR * GRID_W), F32),
        compiler_params=_params(("arbitrary",)),
        name="neighbourhood_bias",
    )(rel_bias.astype(F32).reshape(-1))


def _stage_scratch(n_rows, n_keys, n_groups=2):
    return ([pltpu.VMEM((n_rows, n_keys), F32)] * (2 * n_groups)
            + [pltpu.VMEM((n_rows, LANES), F32)] * (2 * n_groups))


def _loop_scratch(n_rows, n_keys, with_maxima=False):
    maxima = [pltpu.VMEM((n_rows, LANES), F32)] * 4 if with_maxima else []
    return ([pltpu.VMEM((n_rows, n_keys), F32)] * 4 + [pltpu.VMEM((n_rows, n_keys), BF16)] * 4
            + maxima)


def kernel(x, c, ctx, c_ctx, w_ada, b_ada, w_ffn1_gate, w_ffn1_up, w_ffn1_down, w_in, w_out,
           sink_logit, q_norm_g, k_norm_g, rel_pos_bias, lam_q1, lam_k1, lam_q2, lam_k2, subln_g,
           w_ffn2_gate, w_ffn2_up, w_ffn2_down, final_norm_g):
    batch, seq, d = x.shape
    n_ctx = ctx.shape[1]
    depth = w_ada.shape[0]
    assert d == D_MODEL and seq % TOKEN_TILE == 0 and (batch * n_ctx) % TOKEN_TILE == 0
    assert seq % GRID_W == 0 and w_in.shape[-1] == IN_WIDTH and seq % KEY_CHUNK == 0
    assert seq % (2 * ATTN_TILE) == 0
    assert seq % (2 * BLOCK) == 0 and seq % (2 * GRID_W) == 0
    t_lat, t_ctx = batch * seq, batch * n_ctx
    t_all = t_lat + t_ctx
    tm = TOKEN_TILE
    n_lat_tiles, n_ctx_tiles = t_lat // tm, t_ctx // tm
    tiles_per_seq = seq // tm
    grid_rows = seq // GRID_W
    assert grid_rows >= NA_KR and rel_pos_bias.shape[1:] == (GROUP_HEADS, 2 * NA_KR - 1, 2 * NA_KC - 1)

    cc = jnp.concatenate([c, c_ctx[None, :], jnp.zeros((16 - batch - 1, d), F32)], axis=0)
    mods = _mods(cc, w_ada, b_ada).reshape(depth, 16, N_MOD, d)

    def group(i):
        return jnp.minimum(i // tiles_per_seq, batch)

    rope = _rope_tables(seq, tm)

    def rope_block(i):
        return jnp.where(i < n_lat_tiles, i % tiles_per_seq, tiles_per_seq)

    lane_gain = lambda g: jnp.tile(g.astype(F32), LANES // HEAD_DIM)[None, :]

    tile_spec = pl.BlockSpec((tm, d), lambda i: (i, 0))

    ffn1 = tuple(_to_bf16(w) for w in (w_ffn1_gate, w_ffn1_up, w_ffn1_down))
    ffn2 = tuple(_to_bf16(w) for w in (w_ffn2_gate, w_ffn2_up, w_ffn2_down))
    w_in_b = _to_bf16(w_in, swap_cols=(COL_AQ * LANES, COL_BQ * LANES))
    w_out_b = _to_bf16(w_out)

    for l in range(depth):
        last = l == depth - 1
        lam_init = 0.8 - 0.6 * math.exp(-0.3 * l)
        mod_spec = pl.BlockSpec((1, N_MOD, d), lambda i: (group(i), 0, 0))

        gains = jnp.concatenate([lane_gain(q_norm_g[l]), lane_gain(k_norm_g[l]),
                                 jnp.zeros((6, LANES), F32)], axis=0)
        if l == 0:
            streams = (x.reshape(t_lat, d), ctx.reshape(t_ctx, d))
            stream_specs = [
                pl.BlockSpec((tm, d), lambda i: (jnp.minimum(i, n_lat_tiles - 1), 0)),
                pl.BlockSpec((tm, d), lambda i: (jnp.maximum(i - n_lat_tiles, 0), 0))]
        else:
            streams, stream_specs = (xs,), [tile_spec]
        xs, qkv = pl.pallas_call(
            functools.partial(_pre_kernel, n_lat_tiles if l == 0 else None),
            grid=(n_lat_tiles + n_ctx_tiles,),
            in_specs=stream_specs + [
                mod_spec,
                _resident((d, D_FF), l), _resident((d, D_FF), l), _resident((D_FF, d), l),
                _resident((d, IN_WIDTH), l),
                pl.BlockSpec((4, tm, LANES), lambda i: (0, rope_block(i), 0)),
                _resident((8, LANES)),
            ],
            out_specs=[tile_spec, pl.BlockSpec((tm, IN_WIDTH), lambda i: (i, 0))],
            out_shape=[jax.ShapeDtypeStruct((t_all, d), F32),
                       jax.ShapeDtypeStruct((t_all, IN_WIDTH), BF16)],
            scratch_shapes=[pltpu.VMEM((tm, D_FF), BF16)],
            compiler_params=_params(("arbitrary",)),
            name=f"pre_l{l}",
        )(*streams, mods[l], *ffn1, w_in_b, rope, gains)

        ctx_row = t_lat // n_ctx
        sink_perm = sink_logit[l].astype(F32)[jnp.array(GQA_HEAD_ORDER)]
        lamv = jnp.pad(jnp.stack([lam_q1[l], lam_k1[l], lam_q2[l], lam_k2[l]]).astype(F32),
                       ((0, 4), (0, LANES - D_SUB)))
        subln = lane_gain(subln_g[l])
        smem = pl.BlockSpec(memory_space=pltpu.SMEM)

        def lat(cols, col):
            return pl.BlockSpec((seq, cols), lambda b, *_: (b, col))

        def ctxb(cols, col):
            return pl.BlockSpec((n_ctx, cols), lambda b, *_: (ctx_row + b, col))

        def v_ext_scratch(n_blocks):
            lead = () if n_blocks == 1 else (n_blocks,)
            return [pltpu.VMEM(lead + (seq, 2 * LANES), BF16),
                    pltpu.VMEM(lead + (n_ctx, 2 * LANES), BF16)]

        y_a = pl.pallas_call(
            functools.partial(_attn_a_kernel, seq),
            grid=(batch,),
            in_specs=[smem, lat(2 * LANES, COL_AQ // 2), lat(LANES, COL_AK), lat(LANES, COL_AV),
                      ctxb(LANES, COL_AK), ctxb(LANES, COL_AV)],
            out_specs=pl.BlockSpec((seq, GROUP_WIDTH), lambda b: (b, 0)),
            out_shape=jax.ShapeDtypeStruct((t_lat, GROUP_WIDTH), BF16),
            scratch_shapes=(v_ext_scratch(1)
                            + _loop_scratch(2 * BLOCK, BLOCK + 2 * WINDOW + n_ctx,
                                            with_maxima=True)),
            compiler_params=_params(("arbitrary",)),
            name=f"attn_a_l{l}",
        )(sink_perm, qkv, qkv, qkv, qkv, qkv)

        tq = ATTN_TILE
        blocks_per_seq = seq // (2 * tq)
        n_blocks = batch * blocks_per_seq

        def cur(j):
            return jnp.minimum(j, n_blocks - 1)

        def prev(j):
            return jnp.maximum(j - 1, 0)

        def staggered(q_col, k_cols, k_col, v_col):
            def sample_of(block):
                return lambda j: block(j) // blocks_per_seq
            specs = [pl.BlockSpec((2 * tq, 2 * LANES), lambda j: (prev(j), q_col)),
                     pl.BlockSpec((2 * tq, 2 * LANES), lambda j: (cur(j), q_col))]
            for rows, row0 in ((seq, 0), (n_ctx, ctx_row)):
                for block in (prev, cur):
                    specs.append(pl.BlockSpec(
                        (rows, k_cols), lambda j, b=sample_of(block), r=row0: (r + b(j), k_col)))
            for rows, row0 in ((seq, 0), (n_ctx, ctx_row)):
                specs.append(pl.BlockSpec(
                    (rows, k_cols), lambda j, b=sample_of(prev), r=row0: (r + b(j), v_col)))
            return specs

        lagged_out = pl.BlockSpec((2 * tq, GROUP_WIDTH), lambda j: (prev(j), 0))

        y_b = pl.pallas_call(
            functools.partial(_attn_b_kernel, blocks_per_seq),
            grid=(n_blocks + 1,),
            in_specs=staggered(COL_BQ // 2, LANES, COL_BK, COL_BV),
            out_specs=lagged_out,
            out_shape=jax.ShapeDtypeStruct((t_lat, GROUP_WIDTH), BF16),
            scratch_shapes=v_ext_scratch(1) + _stage_scratch(2 * tq, seq + n_ctx),
            compiler_params=_params(("arbitrary",)),
            name=f"attn_b_l{l}",
        )(*([qkv] * 8))

        y_d = pl.pallas_call(
            functools.partial(_attn_d_kernel, lam_init, blocks_per_seq),
            grid=(n_blocks + 1,),
            in_specs=([_resident((8, LANES)), _resident((1, LANES))]
                      + staggered(COL_DQ // 2, 2 * LANES, COL_DK // 2, COL_DV // 2)),
            out_specs=lagged_out,
            out_shape=jax.ShapeDtypeStruct((t_lat, GROUP_WIDTH), BF16),
            scratch_shapes=v_ext_scratch(2) + _stage_scratch(2 * tq, seq + n_ctx, n_groups=4),
            compiler_params=_params(("arbitrary",)),
            name=f"attn_d_l{l}",
        )(lamv, subln, *([qkv] * 8))

        bias = _neighbourhood_bias(rel_pos_bias[l])
        y_c = pl.pallas_call(
            functools.partial(_attn_c_kernel, grid_rows),
            grid=(batch,),
            in_specs=[lat(2 * LANES, COL_CQ // 2),
                      lat(2 * LANES, COL_CK // 2), lat(2 * LANES, COL_CV // 2),
                      ctxb(2 * LANES, COL_CK // 2), ctxb(2 * LANES, COL_CV // 2),
                      _resident(bias.shape)],
            out_specs=pl.BlockSpec((seq, GROUP_WIDTH), lambda b: (b, 0)),
            out_shape=jax.ShapeDtypeStruct((t_lat, GROUP_WIDTH), BF16),
            scratch_shapes=(v_ext_scratch(2)
                            + _loop_scratch(2 * GRID_W, NA_KR * GRID_W + n_ctx)),
            compiler_params=_params(("arbitrary",)),
            name=f"attn_c_l{l}",
        )(qkv, qkv, qkv, qkv, qkv, bias)

        ffn2_specs = [_resident((d, D_FF), l), _resident((d, D_FF), l), _resident((D_FF, d), l)]
        fng = final_norm_g.astype(F32)[None, :]
        y_spec = pl.BlockSpec((tm, GROUP_WIDTH), lambda i: (jnp.minimum(i, n_lat_tiles - 1), 0))
        post_scratch = [pltpu.VMEM((tm, d), BF16), pltpu.VMEM((tm, D_FF), BF16)]

        if not last:
            y_ctx = pl.pallas_call(
                functools.partial(_attn_ctx_kernel, lam_init),
                grid=(batch,),
                in_specs=[smem, _resident((8, LANES)), _resident((1, LANES)),
                          ctxb(2 * LANES, COL_AQ // 2), ctxb(LANES, COL_AK), ctxb(LANES, COL_AV),
                          ctxb(2 * LANES, COL_BQ // 2), ctxb(LANES, COL_BK), ctxb(LANES, COL_BV),
                          ctxb(2 * LANES, COL_CQ // 2), ctxb(2 * LANES, COL_CK // 2),
                          ctxb(2 * LANES, COL_CV // 2),
                          ctxb(2 * LANES, COL_DQ // 2), ctxb(2 * LANES, COL_DK // 2),
                          ctxb(2 * LANES, COL_DV // 2)],
                out_specs=pl.BlockSpec((n_ctx, d), lambda b: (b, 0)),
                out_shape=jax.ShapeDtypeStruct((t_ctx, d), BF16),
                scratch_shapes=_stage_scratch(2 * n_ctx, n_ctx, n_groups=1),
                compiler_params=_params(("arbitrary",)),
                name=f"attn_ctx_l{l}",
            )(sink_perm, lamv, subln, *([qkv] * 12))

            xs = pl.pallas_call(
                functools.partial(_post_kernel, n_lat_tiles, False),
                grid=(n_lat_tiles + n_ctx_tiles,),
                in_specs=[tile_spec, mod_spec, y_spec, y_spec, y_spec, y_spec,
                          pl.BlockSpec((tm, d), lambda i: (jnp.maximum(i - n_lat_tiles, 0), 0)),
                          _resident((d, d), l)] + ffn2_specs + [_resident((1, d))],
                out_specs=tile_spec,
                out_shape=jax.ShapeDtypeStruct((t_all, d), F32),
                scratch_shapes=post_scratch,
                compiler_params=_params(("arbitrary",)),
                name=f"post_l{l}",
            )(xs, mods[l], y_a, y_b, y_c, y_d, y_ctx, w_out_b, *ffn2, fng)
        else:
            xs = pl.pallas_call(
                functools.partial(_post_kernel, None, True),
                grid=(n_lat_tiles,),
                in_specs=[tile_spec, mod_spec, y_spec, y_spec, y_spec, y_spec,
                          _resident((d, d), l)] + ffn2_specs + [_resident((1, d))],
                out_specs=tile_spec,
                out_shape=jax.ShapeDtypeStruct((t_lat, d), F32),
                scratch_shapes=post_scratch,
                compiler_params=_params(("arbitrary",)),
                name=f"post_l{l}",
            )(xs, mods[l], y_a, y_b, y_c, y_d, w_out_b, *ffn2, fng)

    return xs.reshape(batch, seq, d)
```

```python
import functools
import math

import numpy as np
import jax
import jax.numpy as jnp
from jax import lax
from jax.experimental import pallas as pl
from jax.experimental.pallas import tpu as pltpu

D_MODEL = 1024
GRID_W = 64
HEAD_DIM = 64
N_GROUPS = 4
GROUP_HEADS = D_MODEL // (N_GROUPS * HEAD_DIM)
GROUP_WIDTH = GROUP_HEADS * HEAD_DIM
D_SUB = HEAD_DIM // 2
WINDOW = 128
BLOCK = 128
NA_KR = 8
NA_KC = 16
D_FF = 2816
ROPE_BASE = 10000.0
NORM_EPS = 1e-6
N_MOD = 9
NEG_INF = -1e30
IN_WIDTH = 2560

LANES = 128
MXU_N = 256
TOKEN_TILE = 512
GQA_TILE = 256
DIFF_TILE = 128
KEY_CHUNK = 512
VMEM_LIMIT = 56 * 1024 * 1024
LOG2E = 1.4426950408889634

BF16 = jnp.bfloat16
F32 = jnp.float32

COL_AQ, COL_AK, COL_AV = 0, 2, 3
COL_BQ, COL_BK, COL_BV = 4, 6, 7
COL_CQ, COL_CK, COL_CV = 8, 10, 12
COL_DQ, COL_DK, COL_DV = 14, 16, 18

GQA_HEAD_ORDER = (0, 2, 1, 3)


def _dot(a, b):
    return jnp.dot(a, b, preferred_element_type=F32)


def _dot_nt(a, b):
    return lax.dot_general(a, b, (((1,), (1,)), ((), ())), preferred_element_type=F32)


def _params(semantics):
    return pltpu.CompilerParams(dimension_semantics=semantics, vmem_limit_bytes=VMEM_LIMIT)


def _resident(shape, layer=None):
    nd = len(shape)
    if layer is None:
        return pl.BlockSpec(shape, lambda *_: (0,) * nd, pipeline_mode=pl.Buffered(1))
    return pl.BlockSpec((None,) + tuple(shape), lambda *_: (layer,) + (0,) * nd,
                        pipeline_mode=pl.Buffered(1))


def _swap_middle_heads(lo, hi):
    lane = lax.broadcasted_iota(jnp.int32, (1, LANES), 1)
    low = lane < HEAD_DIM
    return (jnp.where(low, lo, pltpu.roll(hi, HEAD_DIM, 1)),
            jnp.where(low, pltpu.roll(lo, HEAD_DIM, 1), hi))


CAST_BLOCK_BYTES = 6 * 1024 * 1024


def _cast_kernel(swap_cols, w_ref, o_ref):
    o_ref[...] = w_ref[...].astype(BF16)
    for c0 in swap_cols:
        lo, hi = _swap_middle_heads(w_ref[:, c0:c0 + LANES], w_ref[:, c0 + LANES:c0 + 2 * LANES])
        o_ref[:, c0:c0 + LANES] = lo.astype(BF16)
        o_ref[:, c0 + LANES:c0 + 2 * LANES] = hi.astype(BF16)


def _to_bf16(w, swap_cols=()):
    depth, r, c = w.shape
    tr = max(t for t in range(16, r + 1, 16) if r % t == 0 and t * c * 4 <= CAST_BLOCK_BYTES)
    spec = pl.BlockSpec((None, tr, c), lambda l, i: (l, i, 0))
    return pl.pallas_call(
        functools.partial(_cast_kernel, tuple(swap_cols)),
        grid=(depth, r // tr),
        in_specs=[spec],
        out_specs=spec,
        out_shape=jax.ShapeDtypeStruct(w.shape, BF16),
        compiler_params=_params(("arbitrary", "arbitrary")),
        name="cast_bf16",
    )(w)


def _mods_kernel(c_ref, w_ref, b_ref, o_ref):
    c = c_ref[...]
    a = c * (1.0 / (1.0 + jnp.exp(-c)))
    o_ref[...] = _dot(a.astype(BF16), w_ref[...].astype(BF16)) + b_ref[...]


def _mods(cc, w_ada, b_ada):
    depth, d, n = w_ada.shape
    rows = cc.shape[0]
    tn = 1152
    return pl.pallas_call(
        _mods_kernel,
        grid=(depth, n // tn),
        in_specs=[
            pl.BlockSpec((rows, d), lambda l, j: (0, 0)),
            pl.BlockSpec((None, d, tn), lambda l, j: (l, 0, j)),
            pl.BlockSpec((None, 1, tn), lambda l, j: (l, 0, j)),
        ],
        out_specs=pl.BlockSpec((None, rows, tn), lambda l, j: (l, 0, j)),
        out_shape=jax.ShapeDtypeStruct((depth, rows, n), F32),
        compiler_params=_params(("arbitrary", "arbitrary")),
        name="adaln_mods",
    )(cc, w_ada, b_ada.reshape(depth, 1, n))


def _modulated(x, mod_ref, k):
    shift = mod_ref[0, k:k + 1, :]
    scale = mod_ref[0, k + 1:k + 2, :]
    ms = jnp.mean(x * x, axis=-1, keepdims=True)
    return (x * lax.rsqrt(ms + NORM_EPS)) * (1.0 + scale) + shift


def _swiglu(h, wg_ref, wu_ref, wd_ref, act_ref):
    hb = h.astype(BF16)
    for j in range(D_FF // MXU_N):
        cols = slice(j * MXU_N, (j + 1) * MXU_N)
        g = _dot(hb, wg_ref[:, cols])
        u = _dot(hb, wu_ref[:, cols])
        act_ref[:, cols] = ((g * (1.0 / (1.0 + jnp.exp(-g)))) * u).astype(BF16)
    return _dot(act_ref[...], wd_ref[...])


def _rope(v, cos, sin_signed, quarter):
    lane = lax.broadcasted_iota(jnp.int32, (1, LANES), 1)
    first = (lane & (2 * quarter - 1)) < quarter
    rot = jnp.where(first, pltpu.roll(v, LANES - quarter, 1), pltpu.roll(v, quarter, 1))
    return v * cos + rot * sin_signed


def _head_rms(v, gain):
    lane = lax.broadcasted_iota(jnp.int32, (1, LANES), 1)
    lo = lane < HEAD_DIM
    sq = v * v
    ms_lo = jnp.sum(jnp.where(lo, sq, 0.0), axis=-1, keepdims=True) * (1.0 / HEAD_DIM)
    ms_hi = jnp.sum(jnp.where(lo, 0.0, sq), axis=-1, keepdims=True) * (1.0 / HEAD_DIM)
    rs = jnp.where(lo, lax.rsqrt(ms_lo + NORM_EPS), lax.rsqrt(ms_hi + NORM_EPS))
    return v * rs * gain


_QK_SCALE = HEAD_DIM ** -0.5 * LOG2E
_SUB_SCALE = D_SUB ** -0.5 * LOG2E
_PROJ_BLOCKS = (
    (16, -1, _QK_SCALE), (16, -1, _QK_SCALE), (16, -1, 1.0), (0, -1, 1.0),
    (16, 0, _QK_SCALE), (16, 0, _QK_SCALE), (16, 1, 1.0), (0, -1, 1.0),
    (0, -1, _QK_SCALE), (0, -1, _QK_SCALE), (0, -1, 1.0), (0, -1, 1.0),
    (0, -1, 1.0), (0, -1, 1.0),
    (8, -1, _SUB_SCALE), (8, -1, _SUB_SCALE), (8, -1, 1.0), (8, -1, 1.0),
    (0, -1, 1.0), (0, -1, 1.0),
)


def _project(h, w_ref, rope_ref, gain_ref, o_ref):
    hb = h.astype(BF16)
    for j in range(IN_WIDTH // MXU_N):
        r = _dot(hb, w_ref[:, j * MXU_N:(j + 1) * MXU_N])
        for half in range(MXU_N // LANES):
            blk = j * (MXU_N // LANES) + half
            quarter, gain_idx, scale = _PROJ_BLOCKS[blk]
            v = r[:, half * LANES:(half + 1) * LANES]
            if gain_idx >= 0:
                v = _head_rms(v, gain_ref[gain_idx:gain_idx + 1, :])
            if quarter:
                t = 0 if quarter == 16 else 2
                v = _rope(v, rope_ref[t], rope_ref[t + 1], quarter)
            if scale != 1.0:
                v = v * scale
            o_ref[:, blk * LANES:(blk + 1) * LANES] = v.astype(BF16)


def _pre_kernel(n_lat_tiles, *refs):
    if n_lat_tiles is None:
        (x_ref, mod_ref, wg_ref, wu_ref, wd_ref, win_ref, rope_ref, gain_ref,
         xo_ref, qkv_ref, act_ref) = refs
        x = x_ref[...]
    else:
        (x_ref, c_ref, mod_ref, wg_ref, wu_ref, wd_ref, win_ref, rope_ref, gain_ref,
         xo_ref, qkv_ref, act_ref) = refs
        x = jnp.where(pl.program_id(0) < n_lat_tiles, x_ref[...], c_ref[...])
    y = _swiglu(_modulated(x, mod_ref, 0), wg_ref, wu_ref, wd_ref, act_ref)
    x = x + (0.5 * mod_ref[0, 2:3, :]) * y
    xo_ref[...] = x
    _project(_modulated(x, mod_ref, 3), win_ref, rope_ref, gain_ref, qkv_ref)


def _post_kernel(n_lat_tiles, final_norm, *refs):
    if n_lat_tiles is None:
        (x_ref, mod_ref, ya_ref, yb_ref, yc_ref, yd_ref, wout_ref, wg_ref, wu_ref, wd_ref,
         fng_ref, xo_ref, y_scr, act_ref) = refs
        y_scr[...] = jnp.concatenate([ya_ref[...], yb_ref[...], yc_ref[...], yd_ref[...]], axis=1)
    else:
        (x_ref, mod_ref, ya_ref, yb_ref, yc_ref, yd_ref, yctx_ref, wout_ref, wg_ref, wu_ref,
         wd_ref, fng_ref, xo_ref, y_scr, act_ref) = refs
        is_latent = pl.program_id(0) < n_lat_tiles

        @pl.when(is_latent)
        def _():
            y_scr[...] = jnp.concatenate(
                [ya_ref[...], yb_ref[...], yc_ref[...], yd_ref[...]], axis=1)

        @pl.when(jnp.logical_not(is_latent))
        def _():
            y_scr[...] = yctx_ref[...]

    x = x_ref[...]
    x = x + mod_ref[0, 5:6, :] * _dot(y_scr[...], wout_ref[...])
    y = _swiglu(_modulated(x, mod_ref, 6), wg_ref, wu_ref, wd_ref, act_ref)
    x = x + (0.5 * mod_ref[0, 8:9, :]) * y
    if final_norm:
        ms = jnp.mean(x * x, axis=-1, keepdims=True)
        x = (x * lax.rsqrt(ms + NORM_EPS)) * fng_ref[...]
    xo_ref[...] = x


def _lane_masks(n, dtype):
    lane = lax.broadcasted_iota(jnp.int32, (1, LANES), 1)
    w = LANES // n
    return [jnp.where((lane >= k * w) & (lane < (k + 1) * w), 1.0, 0.0).astype(dtype)
            for k in range(n)]


def _stack_masked(q, masks):
    return jnp.concatenate([q * m for m in masks], axis=0)


def _fill_ext(ext_ref, v):
    ext_ref[:, :LANES] = v
    ext_ref[:, LANES:] = jnp.ones(v.shape, v.dtype)


def _by_parity(refs):
    half = len(refs) // 2
    return (tuple(refs[:half]), tuple(refs[half:]))


def _zero_at_first_step(refs):
    @pl.when(pl.program_id(0) == 0)
    def _():
        for ref in refs:
            ref[...] = jnp.zeros(ref.shape, ref.dtype)


ROW_BLOCK = 32


def _prime_loop_pipeline(s_refs, p_refs, m_refs=()):
    @pl.when(pl.program_id(0) == 0)
    def _():
        for ref in tuple(s_refs) + tuple(m_refs):
            ref[...] = jnp.zeros(ref.shape, ref.dtype)
        for ref in p_refs:
            ref[...] = jnp.ones(ref.shape, ref.dtype)


def _run_loop_slots(n_tiles, slot):
    def two_slots(j, carry):
        slot(2 * j, 0, 1)
        slot(2 * j + 1, 1, 0)
        return carry

    lax.fori_loop(0, n_tiles // 2 + 1, two_slots, 0)


def _scores(lhs, k_list, s_ref, score_fn=None):
    off = 0
    for idx, k in enumerate(k_list):
        s = _dot_nt(lhs, k)
        if idx == 0 and score_fn is not None:
            s = score_fn(s)
        s_ref[:, off:off + k.shape[0]] = s
        off += k.shape[0]


def _numerators(s_ref, p_ref, floor=None):
    maxima = []
    for r in range(0, s_ref.shape[0], ROW_BLOCK):
        rows = slice(r, r + ROW_BLOCK)
        m = jnp.max(s_ref[rows, :], axis=-1, keepdims=True)
        if floor is not None:
            m = jnp.maximum(m, floor[rows])
        p_ref[rows, :] = jnp.exp2(s_ref[rows, :] - m).astype(BF16)
        maxima.append(m)
    return jnp.concatenate(maxima, axis=0)


def _weighted_values(p_ref, v_list):
    o = None
    off = 0
    for v in v_list:
        part = _dot(p_ref[:, off:off + v.shape[0]], v)
        o = part if o is None else o + part
        off += v.shape[0]
    return o


def _key_chunks(lat_ref, ctx_ref, cols=slice(None)):
    n_lat = lat_ref.shape[0]
    return ([lat_ref[r:r + KEY_CHUNK, cols] for r in range(0, n_lat, KEY_CHUNK)]
            + [ctx_ref[:, cols]])


def _scores_stage(lhs, k_chunks, s_ref, m_ref, score_fn=None, floor=None):
    running = None
    off = 0
    for idx, k in enumerate(k_chunks):
        s = _dot_nt(lhs, k)
        if idx == 0 and score_fn is not None:
            s = score_fn(s)
        n = k.shape[0]
        s_ref[:, off:off + n] = s
        for c in range(0, n, LANES):
            tile = s[:, c:c + LANES]
            running = tile if running is None else jnp.maximum(running, tile)
        off += n
    m = jnp.max(running, axis=-1, keepdims=True)
    if floor is not None:
        m = jnp.maximum(m, floor)
    m_ref[...] = jnp.broadcast_to(m, m_ref.shape)


def _values_stage(s_ref, m_ref, v_chunks):
    m = m_ref[...]
    acc = None
    off = 0
    for v in v_chunks:
        n = v.shape[0]
        p = jnp.concatenate([jnp.exp2(s_ref[:, off + c:off + c + LANES] - m)
                             for c in range(0, n, LANES)], axis=1).astype(BF16)
        part = _dot(p, v)
        acc = part if acc is None else acc + part
        off += n
    return acc


def _pair_select(o, n_q):
    lane = lax.broadcasted_iota(jnp.int32, (1, LANES), 1)
    return jnp.where(lane < HEAD_DIM, o[:n_q], o[n_q:2 * n_q])


def _sub_ln(y, gain, post_scale):
    return _head_rms(y, gain) * post_scale


def _lam_value(lamv_ref, lam_init):
    t1 = jnp.sum(lamv_ref[0:1, :] * lamv_ref[1:2, :], axis=-1, keepdims=True)
    t2 = jnp.sum(lamv_ref[2:3, :] * lamv_ref[3:4, :], axis=-1, keepdims=True)
    return jnp.exp(t1) - jnp.exp(t2) + lam_init


def _sink_rows(sink_ref, pair, n_q):
    row = lax.broadcasted_iota(jnp.int32, (2 * n_q, 1), 0)
    return jnp.where(row < n_q, sink_ref[2 * pair], sink_ref[2 * pair + 1]) * LOG2E


def _diff_combine(o, n_q, lam):
    r = o[:, :LANES] / o[:, LANES:]
    return r[:n_q] - lam * r[n_q:]


def _gqa_output(groups):
    return jnp.concatenate(_swap_middle_heads(*groups), axis=1).astype(BF16)


def _bias_kernel(rel_ref, o_ref):
    h = pl.program_id(0)
    n_dr, n_dc = 2 * NA_KR - 1, 2 * NA_KC - 1
    q_col = lax.broadcasted_iota(jnp.int32, (GRID_W, LANES), 0)
    lane = lax.broadcasted_iota(jnp.int32, (GRID_W, LANES), 1)
    k_col = lane & (GRID_W - 1)
    dc = k_col - q_col + (NA_KC - 1)
    c_start = jnp.clip(q_col - NA_KC // 2, 0, GRID_W - NA_KC)
    in_window = (k_col >= c_start) & (k_col < c_start + NA_KC)
    tiles = []
    for dr in range(n_dr):
        t = jnp.zeros((GRID_W, LANES), F32)
        for d in range(n_dc):
            t = jnp.where(dc == d, rel_ref[(h * n_dr + dr) * n_dc + d], t)
        tiles.append(jnp.where(in_window, t * LOG2E, NEG_INF))
    low = lane < GRID_W
    for off in range(NA_KR):
        for c in range(NA_KR * GRID_W // LANES):
            dr = 2 * c - off + NA_KR - 1
            o_ref[0, off, :, c * LANES:(c + 1) * LANES] = jnp.where(low, tiles[dr], tiles[dr + 1])


def _attn_a_kernel(seq, sink_ref, q_ref, k_ref, v_ref, kc_ref, vc_ref, o_ref,
                   vl_ext, vc_ext, *bufs):
    n_q = BLOCK
    span = BLOCK + 2 * WINDOW
    n_blocks = seq // n_q
    masks = _lane_masks(2, BF16)
    s_refs, p_refs, m_refs = (_by_parity(bufs[i:i + 4]) for i in (0, 4, 8))
    _prime_loop_pipeline(bufs[0:4], bufs[4:8], bufs[8:12])
    _fill_ext(vl_ext, v_ref[...])
    _fill_ext(vc_ext, vc_ref[...])

    def window(n):
        q0 = pl.multiple_of(n * n_q, n_q)
        return q0, pl.multiple_of(jnp.clip(q0 - WINDOW, 0, seq - span), BLOCK)

    def slot(t, new, old):
        q0, start = window(jnp.clip(t - 2, 0, n_blocks - 1))
        v_loc = vl_ext[pl.ds(start, span), :]
        groups = []
        for pair in range(2):
            o = _weighted_values(p_refs[new][pair], [v_loc, vc_ext[...]])
            den = o[:, LANES:] + jnp.exp2(_sink_rows(sink_ref, pair, n_q) - m_refs[new][pair][...])
            groups.append(_pair_select(o[:, :LANES] / den, n_q))
        o_ref[pl.ds(q0, n_q), :] = _gqa_output(groups)
        for pair in range(2):
            m = _numerators(s_refs[old][pair], p_refs[old][pair],
                            floor=_sink_rows(sink_ref, pair, n_q))
            m_refs[old][pair][...] = jnp.broadcast_to(m, (2 * n_q, LANES))
        q0, start = window(jnp.minimum(t, n_blocks - 1))
        q = q_ref[pl.ds(q0, n_q), :]
        k_loc = k_ref[pl.ds(start, span), :]
        q_pos = q0 + (lax.broadcasted_iota(jnp.int32, (2 * n_q, 1), 0) & (n_q - 1))
        k_pos = start + lax.broadcasted_iota(jnp.int32, (1, span), 1)
        valid = jnp.abs(k_pos - q_pos) <= WINDOW
        for pair in range(2):
            lhs = _stack_masked(q[:, pair * LANES:(pair + 1) * LANES], masks)
            _scores(lhs, [k_loc, kc_ref[...]], s_refs[new][pair],
                    score_fn=lambda s: jnp.where(valid, s, NEG_INF))

    _run_loop_slots(n_blocks, slot)


def _dense_slots(n_q, qp_ref, qn_ref, kp_ref, kn_ref, kcp_ref, kcn_ref):
    return ((0, qp_ref, slice(n_q, 2 * n_q), kp_ref, kcp_ref),
            (1, qn_ref, slice(0, n_q), kn_ref, kcn_ref))


def _refill_ext_per_sample(blocks_per_seq, fill):
    pl.when(lax.rem(jnp.maximum(pl.program_id(0) - 1, 0), blocks_per_seq) == 0)(fill)


def _attn_b_kernel(blocks_per_seq, qp_ref, qn_ref, kp_ref, kn_ref, kcp_ref, kcn_ref, v_ref,
                   vc_ref, o_ref, vl_ext, vc_ext, *bufs):
    n_q = o_ref.shape[0] // 2
    s_refs, m_refs = _by_parity(bufs[0:4]), _by_parity(bufs[4:8])
    _zero_at_first_step(bufs)

    def fill():
        _fill_ext(vl_ext, v_ref[...])
        _fill_ext(vc_ext, vc_ref[...])

    _refill_ext_per_sample(blocks_per_seq, fill)
    masks = _lane_masks(2, BF16)
    for parity, q_ref, q_rows, k_ref, kc_ref in _dense_slots(n_q, qp_ref, qn_ref, kp_ref, kn_ref,
                                                            kcp_ref, kcn_ref):
        done, todo = parity, 1 - parity
        groups = []
        for pair in range(2):
            o = _values_stage(s_refs[done][pair], m_refs[done][pair], _key_chunks(vl_ext, vc_ext))
            groups.append(_pair_select(o[:, :LANES] / o[:, LANES:], n_q))
        o_ref[parity * n_q:(parity + 1) * n_q, :] = _gqa_output(groups)
        q = q_ref[q_rows, :]
        for pair in range(2):
            lhs = _stack_masked(q[:, pair * LANES:(pair + 1) * LANES], masks)
            _scores_stage(lhs, _key_chunks(k_ref, kc_ref), s_refs[todo][pair], m_refs[todo][pair])


def _attn_d_kernel(lam_init, blocks_per_seq, lamv_ref, gain_ref, qp_ref, qn_ref, kp_ref, kn_ref,
                   kcp_ref, kcn_ref, v_ref, vc_ref, o_ref, vl_ext, vc_ext, *bufs):
    n_q = o_ref.shape[0] // 2
    s_refs, m_refs = _by_parity(bufs[0:8]), _by_parity(bufs[8:16])
    _zero_at_first_step(bufs)

    def fill():
        for jb in range(2):
            cols = slice(jb * LANES, (jb + 1) * LANES)
            _fill_ext(vl_ext.at[jb], v_ref[:, cols])
            _fill_ext(vc_ext.at[jb], vc_ref[:, cols])

    _refill_ext_per_sample(blocks_per_seq, fill)
    lane = lax.broadcasted_iota(jnp.int32, (1, LANES), 1)
    lam = _lam_value(lamv_ref, lam_init)
    masks = _lane_masks(4, BF16)
    for parity, q_ref, q_rows, k_ref, kc_ref in _dense_slots(n_q, qp_ref, qn_ref, kp_ref, kn_ref,
                                                            kcp_ref, kcn_ref):
        done, todo = parity, 1 - parity
        outs = []
        for jb in range(2):
            ys = [_diff_combine(_values_stage(s_refs[done][2 * jb + head], m_refs[done][2 * jb + head],
                                              _key_chunks(vl_ext.at[jb], vc_ext.at[jb])), n_q, lam)
                  for head in range(2)]
            y = jnp.where(lane < HEAD_DIM, ys[0], ys[1])
            outs.append(_sub_ln(y, gain_ref[...], 1.0 - lam_init))
        o_ref[parity * n_q:(parity + 1) * n_q, :] = jnp.concatenate(outs, axis=1).astype(BF16)
        q = q_ref[q_rows, :]
        for jb in range(2):
            cols = slice(jb * LANES, (jb + 1) * LANES)
            for head in range(2):
                lhs = _stack_masked(q[:, cols], masks[2 * head:2 * head + 2])
                _scores_stage(lhs, _key_chunks(k_ref, kc_ref, cols),
                              s_refs[todo][2 * jb + head], m_refs[todo][2 * jb + head])


def _attn_c_kernel(n_grid_rows, q_ref, k_ref, v_ref, kc_ref, vc_ref, bias_ref,
                   o_ref, vl_ext, vc_ext, *bufs):
    n_q = GRID_W
    n_loc = NA_KR * GRID_W
    masks = _lane_masks(2, BF16)
    s_refs, p_refs = _by_parity(bufs[0:4]), _by_parity(bufs[4:8])
    _prime_loop_pipeline(bufs[0:4], bufs[4:8])
    for jb in range(2):
        cols = slice(jb * LANES, (jb + 1) * LANES)
        _fill_ext(vl_ext.at[jb], v_ref[:, cols])
        _fill_ext(vc_ext.at[jb], vc_ref[:, cols])

    def geometry(r):
        r_start = jnp.clip(r - NA_KR // 2, 0, n_grid_rows - NA_KR)
        return (pl.multiple_of(r * n_q, n_q), pl.multiple_of(r_start * GRID_W, GRID_W),
                r - r_start)

    def slot(t, new, old):
        q0, k0, _ = geometry(jnp.clip(t - 2, 0, n_grid_rows - 1))
        outs = []
        for jb in range(2):
            o = _weighted_values(p_refs[new][jb], [vl_ext[jb, pl.ds(k0, n_loc), :], vc_ext[jb]])
            outs.append(_pair_select(o[:, :LANES] / o[:, LANES:], n_q))
        o_ref[pl.ds(q0, n_q), :] = jnp.concatenate(outs, axis=1).astype(BF16)
        for jb in range(2):
            _numerators(s_refs[old][jb], p_refs[old][jb])
        q0, k0, off = geometry(jnp.minimum(t, n_grid_rows - 1))
        q = q_ref[pl.ds(q0, n_q), :]
        for jb in range(2):
            cols = slice(jb * LANES, (jb + 1) * LANES)
            lhs = _stack_masked(q[:, cols], masks)
            bias = jnp.concatenate([bias_ref[2 * jb, off], bias_ref[2 * jb + 1, off]], axis=0)
            _scores(lhs, [k_ref[pl.ds(k0, n_loc), cols], kc_ref[:, cols]], s_refs[new][jb],
                    score_fn=lambda s: s + bias)

    _run_loop_slots(n_grid_rows, slot)


def _attn_ctx_kernel(lam_init, sink_ref, lamv_ref, gain_ref, aq_ref, ak_ref, av_ref, bq_ref, bk_ref,
                     bv_ref, cq_ref, ck_ref, cv_ref, dq_ref, dk_ref, dv_ref, o_ref,
                     s0, s1, m0, m1):
    n_q = aq_ref.shape[0]
    m2 = _lane_masks(2, BF16)
    m4 = _lane_masks(4, BF16)
    lane = lax.broadcasted_iota(jnp.int32, (1, LANES), 1)
    sets = ((s0, m0), (s1, m1))

    def ext(v):
        return jnp.concatenate([v, jnp.ones(v.shape, v.dtype)], axis=1)

    def attend(lhs, k, v_ext, s_ref, m_ref, floor=None):
        _scores_stage(lhs, [k], s_ref, m_ref, floor=floor)
        return _values_stage(s_ref, m_ref, [v_ext])

    def out_cols(group, pair):
        c0 = group * GROUP_WIDTH + pair * LANES
        return slice(c0, c0 + LANES)

    for group, (q_ref_, k_ref_, v_ref_) in enumerate(((aq_ref, ak_ref, av_ref),
                                                      (bq_ref, bk_ref, bv_ref))):
        q = q_ref_[...]
        v_ext = ext(v_ref_[...])
        groups = []
        for pair, (s_ref, m_ref) in enumerate(sets):
            lhs = _stack_masked(q[:, pair * LANES:(pair + 1) * LANES], m2)
            sink = _sink_rows(sink_ref, pair, n_q) if group == 0 else None
            o = attend(lhs, k_ref_[...], v_ext, s_ref, m_ref, floor=sink)
            den = o[:, LANES:]
            if group == 0:
                den = den + jnp.exp2(sink - m_ref[...])
            groups.append(_pair_select(o[:, :LANES] / den, n_q))
        o_ref[:, group * GROUP_WIDTH:(group + 1) * GROUP_WIDTH] = _gqa_output(groups)

    q = cq_ref[...]
    for jb, (s_ref, m_ref) in enumerate(sets):
        cols = slice(jb * LANES, (jb + 1) * LANES)
        o = attend(_stack_masked(q[:, cols], m2), ck_ref[:, cols], ext(cv_ref[:, cols]),
                   s_ref, m_ref)
        o_ref[:, out_cols(2, jb)] = _pair_select(o[:, :LANES] / o[:, LANES:], n_q).astype(BF16)

    lam = _lam_value(lamv_ref, lam_init)
    q = dq_ref[...]
    for jb in range(2):
        cols = slice(jb * LANES, (jb + 1) * LANES)
        v_ext = ext(dv_ref[:, cols])
        ys = []
        for head, (s_ref, m_ref) in enumerate(sets):
            lhs = _stack_masked(q[:, cols], m4[2 * head:2 * head + 2])
            ys.append(_diff_combine(attend(lhs, dk_ref[:, cols], v_ext, s_ref, m_ref), n_q, lam))
        y = jnp.where(lane < HEAD_DIM, ys[0], ys[1])
        o_ref[:, out_cols(3, jb)] = _sub_ln(y, gain_ref[...], 1.0 - lam_init).astype(BF16)


def _rope_tables(seq, n_extra):
    t = np.arange(seq)
    row = (t // GRID_W).astype(np.float64)
    col = (t % GRID_W).astype(np.float64)
    tables = []
    for dim in (HEAD_DIM, D_SUB):
        half = dim // 2
        freqs = ROPE_BASE ** (-np.arange(0, half, 2, dtype=np.float64) / half)
        ang_r = row[:, None] * freqs[None, :]
        ang_c = col[:, None] * freqs[None, :]
        ang = np.concatenate([ang_r, ang_r, ang_c, ang_c], axis=-1)
        quarter = dim // 4
        sign = np.where((np.arange(dim) % (2 * quarter)) < quarter, -1.0, 1.0)
        cos = np.tile(np.cos(ang), (1, LANES // dim))
        sin = np.tile(np.sin(ang) * sign[None, :], (1, LANES // dim))
        cos = np.concatenate([cos, np.ones((n_extra, LANES))], axis=0)
        sin = np.concatenate([sin, np.zeros((n_extra, LANES))], axis=0)
        tables += [cos, sin]
    return jnp.asarray(np.stack(tables), dtype=F32)


def _neighbourhood_bias(rel_bias):
    heads = rel_bias.shape[0]
    return pl.pallas_call(
        _bias_kernel,
        grid=(heads,),
        in_specs=[pl.BlockSpec(memory_space=pltpu.SMEM)],
        out_specs=pl.BlockSpec((1, NA_KR, GRID_W, NA_KR * GRID_W), lambda h: (h, 0, 0, 0)),
        out_shape=jax.ShapeDtypeStruct((heads, NA_KR, GRID_W, NA_KR * GRID_W), F32),
        compiler_params=_params(("arbitrary",)),
        name="neighbourhood_bias",
    )(rel_bias.astype(F32).reshape(-1))


def _stage_scratch(n_rows, n_keys, n_groups=2):
    return ([pltpu.VMEM((n_rows, n_keys), F32)] * (2 * n_groups)
            + [pltpu.VMEM((n_rows, LANES), F32)] * (2 * n_groups))


def _loop_scratch(n_rows, n_keys, with_maxima=False):
    maxima = [pltpu.VMEM((n_rows, LANES), F32)] * 4 if with_maxima else []
    return ([pltpu.VMEM((n_rows, n_keys), F32)] * 4 + [pltpu.VMEM((n_rows, n_keys), BF16)] * 4
            + maxima)


def kernel(x, c, ctx, c_ctx, w_ada, b_ada, w_ffn1_gate, w_ffn1_up, w_ffn1_down, w_in, w_out,
           sink_logit, q_norm_g, k_norm_g, rel_pos_bias, lam_q1, lam_k1, lam_q2, lam_k2, subln_g,
           w_ffn2_gate, w_ffn2_up, w_ffn2_down, final_norm_g):
    batch, seq, d = x.shape
    n_ctx = ctx.shape[1]
    depth = w_ada.shape[0]
    assert d == D_MODEL and seq % TOKEN_TILE == 0 and (batch * n_ctx) % TOKEN_TILE == 0
    assert seq % GRID_W == 0 and w_in.shape[-1] == IN_WIDTH and seq % KEY_CHUNK == 0
    assert seq % (2 * GQA_TILE) == 0 and seq % (2 * DIFF_TILE) == 0
    assert seq % (2 * BLOCK) == 0 and seq % (2 * GRID_W) == 0
    t_lat, t_ctx = batch * seq, batch * n_ctx
    t_all = t_lat + t_ctx
    tm = TOKEN_TILE
    n_lat_tiles, n_ctx_tiles = t_lat // tm, t_ctx // tm
    tiles_per_seq = seq // tm
    grid_rows = seq // GRID_W
    assert grid_rows >= NA_KR and rel_pos_bias.shape[1:] == (GROUP_HEADS, 2 * NA_KR - 1, 2 * NA_KC - 1)

    cc = jnp.concatenate([c, c_ctx[None, :], jnp.zeros((16 - batch - 1, d), F32)], axis=0)
    mods = _mods(cc, w_ada, b_ada).reshape(depth, 16, N_MOD, d)

    def group(i):
        return jnp.minimum(i // tiles_per_seq, batch)

    rope = _rope_tables(seq, tm)

    def rope_block(i):
        return jnp.where(i < n_lat_tiles, i % tiles_per_seq, tiles_per_seq)

    lane_gain = lambda g: jnp.tile(g.astype(F32), LANES // HEAD_DIM)[None, :]

    tile_spec = pl.BlockSpec((tm, d), lambda i: (i, 0))

    ffn1 = tuple(_to_bf16(w) for w in (w_ffn1_gate, w_ffn1_up, w_ffn1_down))
    ffn2 = tuple(_to_bf16(w) for w in (w_ffn2_gate, w_ffn2_up, w_ffn2_down))
    w_in_b = _to_bf16(w_in, swap_cols=(COL_AQ * LANES, COL_BQ * LANES))
    w_out_b = _to_bf16(w_out)

    for l in range(depth):
        last = l == depth - 1
        lam_init = 0.8 - 0.6 * math.exp(-0.3 * l)
        mod_spec = pl.BlockSpec((1, N_MOD, d), lambda i: (group(i), 0, 0))

        gains = jnp.concatenate([lane_gain(q_norm_g[l]), lane_gain(k_norm_g[l]),
                                 jnp.zeros((6, LANES), F32)], axis=0)
        if l == 0:
            streams = (x.reshape(t_lat, d), ctx.reshape(t_ctx, d))
            stream_specs = [
                pl.BlockSpec((tm, d), lambda i: (jnp.minimum(i, n_lat_tiles - 1), 0)),
                pl.BlockSpec((tm, d), lambda i: (jnp.maximum(i - n_lat_tiles, 0), 0))]
        else:
            streams, stream_specs = (xs,), [tile_spec]
        xs, qkv = pl.pallas_call(
            functools.partial(_pre_kernel, n_lat_tiles if l == 0 else None),
            grid=(n_lat_tiles + n_ctx_tiles,),
            in_specs=stream_specs + [
                mod_spec,
                _resident((d, D_FF), l), _resident((d, D_FF), l), _resident((D_FF, d), l),
                _resident((d, IN_WIDTH), l),
                pl.BlockSpec((4, tm, LANES), lambda i: (0, rope_block(i), 0)),
                _resident((8, LANES)),
            ],
            out_specs=[tile_spec, pl.BlockSpec((tm, IN_WIDTH), lambda i: (i, 0))],
            out_shape=[jax.ShapeDtypeStruct((t_all, d), F32),
                       jax.ShapeDtypeStruct((t_all, IN_WIDTH), BF16)],
            scratch_shapes=[pltpu.VMEM((tm, D_FF), BF16)],
            compiler_params=_params(("arbitrary",)),
            name=f"pre_l{l}",
        )(*streams, mods[l], *ffn1, w_in_b, rope, gains)

        ctx_row = t_lat // n_ctx
        sink_perm = sink_logit[l].astype(F32)[jnp.array(GQA_HEAD_ORDER)]
        lamv = jnp.pad(jnp.stack([lam_q1[l], lam_k1[l], lam_q2[l], lam_k2[l]]).astype(F32),
                       ((0, 4), (0, LANES - D_SUB)))
        subln = lane_gain(subln_g[l])
        smem = pl.BlockSpec(memory_space=pltpu.SMEM)

        def lat(cols, col):
            return pl.BlockSpec((seq, cols), lambda b, *_: (b, col))

        def ctxb(cols, col):
            return pl.BlockSpec((n_ctx, cols), lambda b, *_: (ctx_row + b, col))

        def v_ext_scratch(n_blocks):
            lead = () if n_blocks == 1 else (n_blocks,)
            return [pltpu.VMEM(lead + (seq, 2 * LANES), BF16),
                    pltpu.VMEM(lead + (n_ctx, 2 * LANES), BF16)]

        y_a = pl.pallas_call(
            functools.partial(_attn_a_kernel, seq),
            grid=(batch,),
            in_specs=[smem, lat(2 * LANES, COL_AQ // 2), lat(LANES, COL_AK), lat(LANES, COL_AV),
                      ctxb(LANES, COL_AK), ctxb(LANES, COL_AV)],
            out_specs=pl.BlockSpec((seq, GROUP_WIDTH), lambda b: (b, 0)),
            out_shape=jax.ShapeDtypeStruct((t_lat, GROUP_WIDTH), BF16),
            scratch_shapes=(v_ext_scratch(1)
                            + _loop_scratch(2 * BLOCK, BLOCK + 2 * WINDOW + n_ctx,
                                            with_maxima=True)),
            compiler_params=_params(("arbitrary",)),
            name=f"attn_a_l{l}",
        )(sink_perm, qkv, qkv, qkv, qkv, qkv)

        def dense_mixer(body, name, tq, n_groups, q_col, k_cols, k_col, v_col, params=(),
                        ext_blocks=0):
            blocks_per_seq = seq // (2 * tq)
            n_blocks = batch * blocks_per_seq

            def cur(j):
                return jnp.minimum(j, n_blocks - 1)

            def prev(j):
                return jnp.maximum(j - 1, 0)

            def sample_of(block):
                return lambda j: block(j) // blocks_per_seq

            specs = [pl.BlockSpec((2 * tq, 2 * LANES), lambda j: (prev(j), q_col)),
                     pl.BlockSpec((2 * tq, 2 * LANES), lambda j: (cur(j), q_col))]
            for rows, row0 in ((seq, 0), (n_ctx, ctx_row)):
                for block in (prev, cur):
                    specs.append(pl.BlockSpec(
                        (rows, k_cols), lambda j, b=sample_of(block), r=row0: (r + b(j), k_col)))
            for rows, row0 in ((seq, 0), (n_ctx, ctx_row)):
                specs.append(pl.BlockSpec(
                    (rows, k_cols), lambda j, b=sample_of(prev), r=row0: (r + b(j), v_col)))
            return pl.pallas_call(
                functools.partial(body, blocks_per_seq),
                grid=(n_blocks + 1,),
                in_specs=[_resident(p.shape) for p in params] + specs,
                out_specs=pl.BlockSpec((2 * tq, GROUP_WIDTH), lambda j: (prev(j), 0)),
                out_shape=jax.ShapeDtypeStruct((t_lat, GROUP_WIDTH), BF16),
                scratch_shapes=((v_ext_scratch(ext_blocks) if ext_blocks else [])
                                + _stage_scratch(2 * tq, seq + n_ctx, n_groups=n_groups)),
                compiler_params=_params(("arbitrary",)),
                name=name,
            )(*params, *([qkv] * 8))

        y_b = dense_mixer(_attn_b_kernel, f"attn_b_l{l}", GQA_TILE, 2,
                          COL_BQ // 2, LANES, COL_BK, COL_BV, ext_blocks=1)
        y_d = dense_mixer(functools.partial(_attn_d_kernel, lam_init), f"attn_d_l{l}", DIFF_TILE, 4,
                          COL_DQ // 2, 2 * LANES, COL_DK // 2, COL_DV // 2, params=(lamv, subln),
                          ext_blocks=2)

        bias = _neighbourhood_bias(rel_pos_bias[l])
        y_c = pl.pallas_call(
            functools.partial(_attn_c_kernel, grid_rows),
            grid=(batch,),
            in_specs=[lat(2 * LANES, COL_CQ // 2),
                      lat(2 * LANES, COL_CK // 2), lat(2 * LANES, COL_CV // 2),
                      ctxb(2 * LANES, COL_CK // 2), ctxb(2 * LANES, COL_CV // 2),
                      _resident(bias.shape)],
            out_specs=pl.BlockSpec((seq, GROUP_WIDTH), lambda b: (b, 0)),
            out_shape=jax.ShapeDtypeStruct((t_lat, GROUP_WIDTH), BF16),
            scratch_shapes=(v_ext_scratch(2)
                            + _loop_scratch(2 * GRID_W, NA_KR * GRID_W + n_ctx)),
            compiler_params=_params(("arbitrary",)),
            name=f"attn_c_l{l}",
        )(qkv, qkv, qkv, qkv, qkv, bias)

        ffn2_specs = [_resident((d, D_FF), l), _resident((d, D_FF), l), _resident((D_FF, d), l)]
        fng = final_norm_g.astype(F32)[None, :]
        y_spec = pl.BlockSpec((tm, GROUP_WIDTH), lambda i: (jnp.minimum(i, n_lat_tiles - 1), 0))
        post_scratch = [pltpu.VMEM((tm, d), BF16), pltpu.VMEM((tm, D_FF), BF16)]

        if not last:
            y_ctx = pl.pallas_call(
                functools.partial(_attn_ctx_kernel, lam_init),
                grid=(batch,),
                in_specs=[smem, _resident((8, LANES)), _resident((1, LANES)),
                          ctxb(2 * LANES, COL_AQ // 2), ctxb(LANES, COL_AK), ctxb(LANES, COL_AV),
                          ctxb(2 * LANES, COL_BQ // 2), ctxb(LANES, COL_BK), ctxb(LANES, COL_BV),
                          ctxb(2 * LANES, COL_CQ // 2), ctxb(2 * LANES, COL_CK // 2),
                          ctxb(2 * LANES, COL_CV // 2),
                          ctxb(2 * LANES, COL_DQ // 2), ctxb(2 * LANES, COL_DK // 2),
                          ctxb(2 * LANES, COL_DV // 2)],
                out_specs=pl.BlockSpec((n_ctx, d), lambda b: (b, 0)),
                out_shape=jax.ShapeDtypeStruct((t_ctx, d), BF16),
                scratch_shapes=_stage_scratch(2 * n_ctx, n_ctx, n_groups=1),
                compiler_params=_params(("arbitrary",)),
                name=f"attn_ctx_l{l}",
            )(sink_perm, lamv, subln, *([qkv] * 12))

            xs = pl.pallas_call(
                functools.partial(_post_kernel, n_lat_tiles, False),
                grid=(n_lat_tiles + n_ctx_tiles,),
                in_specs=[tile_spec, mod_spec, y_spec, y_spec, y_spec, y_spec,
                          pl.BlockSpec((tm, d), lambda i: (jnp.maximum(i - n_lat_tiles, 0), 0)),
                          _resident((d, d), l)] + ffn2_specs + [_resident((1, d))],
                out_specs=tile_spec,
                out_shape=jax.ShapeDtypeStruct((t_all, d), F32),
                scratch_shapes=post_scratch,
                compiler_params=_params(("arbitrary",)),
                name=f"post_l{l}",
            )(xs, mods[l], y_a, y_b, y_c, y_d, y_ctx, w_out_b, *ffn2, fng)
        else:
            xs = pl.pallas_call(
                functools.partial(_post_kernel, None, True),
                grid=(n_lat_tiles,),
                in_specs=[tile_spec, mod_spec, y_spec, y_spec, y_spec, y_spec,
                          _resident((d, d), l)] + ffn2_specs + [_resident((1, d))],
                out_specs=tile_spec,
                out_shape=jax.ShapeDtypeStruct((t_lat, d), F32),
                scratch_shapes=post_scratch,
                compiler_params=_params(("arbitrary",)),
                name=f"post_l{l}",
            )(xs, mods[l], y_a, y_b, y_c, y_d, w_out_b, *ffn2, fng)

    return xs.reshape(batch, seq, d)
```

```python
import functools
import math

import numpy as np
import jax
import jax.numpy as jnp
from jax import lax
from jax.experimental import pallas as pl
from jax.experimental.pallas import tpu as pltpu

D_MODEL = 1024
GRID_W = 64
HEAD_DIM = 64
N_GROUPS = 4
GROUP_HEADS = D_MODEL // (N_GROUPS * HEAD_DIM)
GROUP_WIDTH = GROUP_HEADS * HEAD_DIM
D_SUB = HEAD_DIM // 2
WINDOW = 128
BLOCK = 128
NA_KR = 8
NA_KC = 16
D_FF = 2816
ROPE_BASE = 10000.0
NORM_EPS = 1e-6
N_MOD = 9
NEG_INF = -1e30
IN_WIDTH = 2560

LANES = 128
MXU_N = 256
TOKEN_TILE = 512
GQA_TILE = 256
DIFF_TILE = 128
KEY_CHUNK = 512
VMEM_LIMIT = 56 * 1024 * 1024
LOG2E = 1.4426950408889634

BF16 = jnp.bfloat16
F32 = jnp.float32

COL_AQ, COL_AK, COL_AV = 0, 2, 3
COL_BQ, COL_BK, COL_BV = 4, 6, 7
COL_CQ, COL_CK, COL_CV = 8, 10, 12
COL_DQ, COL_DK, COL_DV = 14, 16, 18

GQA_HEAD_ORDER = (0, 2, 1, 3)


def _dot(a, b):
    return jnp.dot(a, b, preferred_element_type=F32)


def _dot_nt(a, b):
    return lax.dot_general(a, b, (((1,), (1,)), ((), ())), preferred_element_type=F32)


def _params(semantics):
    return pltpu.CompilerParams(dimension_semantics=semantics, vmem_limit_bytes=VMEM_LIMIT)


def _resident(shape, layer=None):
    nd = len(shape)
    if layer is None:
        return pl.BlockSpec(shape, lambda *_: (0,) * nd, pipeline_mode=pl.Buffered(1))
    return pl.BlockSpec((None,) + tuple(shape), lambda *_: (layer,) + (0,) * nd,
                        pipeline_mode=pl.Buffered(1))


def _swap_middle_heads(lo, hi):
    lane = lax.broadcasted_iota(jnp.int32, (1, LANES), 1)
    low = lane < HEAD_DIM
    return (jnp.where(low, lo, pltpu.roll(hi, HEAD_DIM, 1)),
            jnp.where(low, pltpu.roll(lo, HEAD_DIM, 1), hi))


CAST_BLOCK_BYTES = 6 * 1024 * 1024


def _cast_kernel(swap_cols, w_ref, o_ref):
    o_ref[...] = w_ref[...].astype(BF16)
    for c0 in swap_cols:
        lo, hi = _swap_middle_heads(w_ref[:, c0:c0 + LANES], w_ref[:, c0 + LANES:c0 + 2 * LANES])
        o_ref[:, c0:c0 + LANES] = lo.astype(BF16)
        o_ref[:, c0 + LANES:c0 + 2 * LANES] = hi.astype(BF16)


def _to_bf16(w, swap_cols=()):
    depth, r, c = w.shape
    tr = max(t for t in range(16, r + 1, 16) if r % t == 0 and t * c * 4 <= CAST_BLOCK_BYTES)
    spec = pl.BlockSpec((None, tr, c), lambda l, i: (l, i, 0))
    return pl.pallas_call(
        functools.partial(_cast_kernel, tuple(swap_cols)),
        grid=(depth, r // tr),
        in_specs=[spec],
        out_specs=spec,
        out_shape=jax.ShapeDtypeStruct(w.shape, BF16),
        compiler_params=_params(("arbitrary", "arbitrary")),
        name="cast_bf16",
    )(w)


def _mods_kernel(c_ref, w_ref, b_ref, o_ref):
    c = c_ref[...]
    a = c * (1.0 / (1.0 + jnp.exp(-c)))
    o_ref[...] = _dot(a.astype(BF16), w_ref[...].astype(BF16)) + b_ref[...]


def _mods(cc, w_ada, b_ada):
    depth, d, n = w_ada.shape
    rows = cc.shape[0]
    tn = 1152
    return pl.pallas_call(
        _mods_kernel,
        grid=(depth, n // tn),
        in_specs=[
            pl.BlockSpec((rows, d), lambda l, j: (0, 0)),
            pl.BlockSpec((None, d, tn), lambda l, j: (l, 0, j)),
            pl.BlockSpec((None, 1, tn), lambda l, j: (l, 0, j)),
        ],
        out_specs=pl.BlockSpec((None, rows, tn), lambda l, j: (l, 0, j)),
        out_shape=jax.ShapeDtypeStruct((depth, rows, n), F32),
        compiler_params=_params(("arbitrary", "arbitrary")),
        name="adaln_mods",
    )(cc, w_ada, b_ada.reshape(depth, 1, n))


def _modulated(x, mod_ref, k):
    shift = mod_ref[0, k:k + 1, :]
    scale = mod_ref[0, k + 1:k + 2, :]
    ms = jnp.mean(x * x, axis=-1, keepdims=True)
    return (x * lax.rsqrt(ms + NORM_EPS)) * (1.0 + scale) + shift


def _swiglu(h, wg_ref, wu_ref, wd_ref, act_ref):
    hb = h.astype(BF16)
    for j in range(D_FF // MXU_N):
        cols = slice(j * MXU_N, (j + 1) * MXU_N)
        g = _dot(hb, wg_ref[:, cols])
        u = _dot(hb, wu_ref[:, cols])
        act_ref[:, cols] = ((g * (1.0 / (1.0 + jnp.exp(-g)))) * u).astype(BF16)
    return _dot(act_ref[...], wd_ref[...])


def _rope(v, cos, sin_signed, quarter):
    lane = lax.broadcasted_iota(jnp.int32, (1, LANES), 1)
    first = (lane & (2 * quarter - 1)) < quarter
    rot = jnp.where(first, pltpu.roll(v, LANES - quarter, 1), pltpu.roll(v, quarter, 1))
    return v * cos + rot * sin_signed


def _head_rms(v, gain):
    lane = lax.broadcasted_iota(jnp.int32, (1, LANES), 1)
    lo = lane < HEAD_DIM
    sq = v * v
    ms_lo = jnp.sum(jnp.where(lo, sq, 0.0), axis=-1, keepdims=True) * (1.0 / HEAD_DIM)
    ms_hi = jnp.sum(jnp.where(lo, 0.0, sq), axis=-1, keepdims=True) * (1.0 / HEAD_DIM)
    rs = jnp.where(lo, lax.rsqrt(ms_lo + NORM_EPS), lax.rsqrt(ms_hi + NORM_EPS))
    return v * rs * gain


_QK_SCALE = HEAD_DIM ** -0.5 * LOG2E
_SUB_SCALE = D_SUB ** -0.5 * LOG2E
_PROJ_BLOCKS = (
    (16, -1, _QK_SCALE), (16, -1, _QK_SCALE), (16, -1, 1.0), (0, -1, 1.0),
    (16, 0, _QK_SCALE), (16, 0, _QK_SCALE), (16, 1, 1.0), (0, -1, 1.0),
    (0, -1, _QK_SCALE), (0, -1, _QK_SCALE), (0, -1, 1.0), (0, -1, 1.0),
    (0, -1, 1.0), (0, -1, 1.0),
    (8, -1, _SUB_SCALE), (8, -1, _SUB_SCALE), (8, -1, 1.0), (8, -1, 1.0),
    (0, -1, 1.0), (0, -1, 1.0),
)


def _project(h, w_ref, rope_ref, gain_ref, o_ref):
    hb = h.astype(BF16)
    for j in range(IN_WIDTH // MXU_N):
        r = _dot(hb, w_ref[:, j * MXU_N:(j + 1) * MXU_N])
        for half in range(MXU_N // LANES):
            blk = j * (MXU_N // LANES) + half
            quarter, gain_idx, scale = _PROJ_BLOCKS[blk]
            v = r[:, half * LANES:(half + 1) * LANES]
            if gain_idx >= 0:
                v = _head_rms(v, gain_ref[gain_idx:gain_idx + 1, :])
            if quarter:
                t = 0 if quarter == 16 else 2
                v = _rope(v, rope_ref[t], rope_ref[t + 1], quarter)
            if scale != 1.0:
                v = v * scale
            o_ref[:, blk * LANES:(blk + 1) * LANES] = v.astype(BF16)


def _pre_kernel(n_lat_tiles, *refs):
    if n_lat_tiles is None:
        (x_ref, mod_ref, wg_ref, wu_ref, wd_ref, win_ref, rope_ref, gain_ref,
         xo_ref, qkv_ref, act_ref) = refs
        x = x_ref[...]
    else:
        (x_ref, c_ref, mod_ref, wg_ref, wu_ref, wd_ref, win_ref, rope_ref, gain_ref,
         xo_ref, qkv_ref, act_ref) = refs
        x = jnp.where(pl.program_id(0) < n_lat_tiles, x_ref[...], c_ref[...])
    y = _swiglu(_modulated(x, mod_ref, 0), wg_ref, wu_ref, wd_ref, act_ref)
    x = x + (0.5 * mod_ref[0, 2:3, :]) * y
    xo_ref[...] = x
    _project(_modulated(x, mod_ref, 3), win_ref, rope_ref, gain_ref, qkv_ref)


def _post_kernel(n_lat_tiles, final_norm, *refs):
    if n_lat_tiles is None:
        (x_ref, mod_ref, ya_ref, yb_ref, yc_ref, yd_ref, wout_ref, wg_ref, wu_ref, wd_ref,
         fng_ref, xo_ref, y_scr, act_ref) = refs
        y_scr[...] = jnp.concatenate([ya_ref[...], yb_ref[...], yc_ref[...], yd_ref[...]], axis=1)
    else:
        (x_ref, mod_ref, ya_ref, yb_ref, yc_ref, yd_ref, yctx_ref, wout_ref, wg_ref, wu_ref,
         wd_ref, fng_ref, xo_ref, y_scr, act_ref) = refs
        is_latent = pl.program_id(0) < n_lat_tiles

        @pl.when(is_latent)
        def _():
            y_scr[...] = jnp.concatenate(
                [ya_ref[...], yb_ref[...], yc_ref[...], yd_ref[...]], axis=1)

        @pl.when(jnp.logical_not(is_latent))
        def _():
            y_scr[...] = yctx_ref[...]

    x = x_ref[...]
    x = x + mod_ref[0, 5:6, :] * _dot(y_scr[...], wout_ref[...])
    y = _swiglu(_modulated(x, mod_ref, 6), wg_ref, wu_ref, wd_ref, act_ref)
    x = x + (0.5 * mod_ref[0, 8:9, :]) * y
    if final_norm:
        ms = jnp.mean(x * x, axis=-1, keepdims=True)
        x = (x * lax.rsqrt(ms + NORM_EPS)) * fng_ref[...]
    xo_ref[...] = x


def _lane_masks(n, dtype):
    lane = lax.broadcasted_iota(jnp.int32, (1, LANES), 1)
    w = LANES // n
    return [jnp.where((lane >= k * w) & (lane < (k + 1) * w), 1.0, 0.0).astype(dtype)
            for k in range(n)]


def _stack_masked(q, masks):
    return jnp.concatenate([q * m for m in masks], axis=0)


def _fill_ext(ext_ref, v):
    ext_ref[:, :LANES] = v
    ext_ref[:, LANES:] = jnp.ones(v.shape, v.dtype)


def _by_parity(refs):
    half = len(refs) // 2
    return (tuple(refs[:half]), tuple(refs[half:]))


def _zero_at_first_step(refs):
    @pl.when(pl.program_id(0) == 0)
    def _():
        for ref in refs:
            ref[...] = jnp.zeros(ref.shape, ref.dtype)


ROW_BLOCK = 32


def _prime_loop_pipeline(s_refs, p_refs, m_refs=()):
    @pl.when(pl.program_id(0) == 0)
    def _():
        for ref in tuple(s_refs) + tuple(m_refs):
            ref[...] = jnp.zeros(ref.shape, ref.dtype)
        for ref in p_refs:
            ref[...] = jnp.ones(ref.shape, ref.dtype)


def _run_loop_slots(n_tiles, slot):
    def two_slots(j, carry):
        slot(2 * j, 0, 1)
        slot(2 * j + 1, 1, 0)
        return carry

    lax.fori_loop(0, n_tiles // 2 + 1, two_slots, 0)


def _scores(lhs, k_list, s_ref, score_fn=None):
    off = 0
    for idx, k in enumerate(k_list):
        s = _dot_nt(lhs, k)
        if idx == 0 and score_fn is not None:
            s = score_fn(s)
        s_ref[:, off:off + k.shape[0]] = s
        off += k.shape[0]


def _numerators(s_ref, p_ref, floor=None):
    maxima = []
    for r in range(0, s_ref.shape[0], ROW_BLOCK):
        rows = slice(r, r + ROW_BLOCK)
        m = jnp.max(s_ref[rows, :], axis=-1, keepdims=True)
        if floor is not None:
            m = jnp.maximum(m, floor[rows])
        p_ref[rows, :] = jnp.exp2(s_ref[rows, :] - m).astype(BF16)
        maxima.append(m)
    return jnp.concatenate(maxima, axis=0)


def _weighted_values(p_ref, v_list):
    o = None
    off = 0
    for v in v_list:
        part = _dot(p_ref[:, off:off + v.shape[0]], v)
        o = part if o is None else o + part
        off += v.shape[0]
    return o


def _key_chunks(lat_ref, ctx_ref, cols=slice(None)):
    n_lat = lat_ref.shape[0]
    return ([lat_ref[r:r + KEY_CHUNK, cols] for r in range(0, n_lat, KEY_CHUNK)]
            + [ctx_ref[:, cols]])


def _scores_stage(lhs, k_chunks, s_ref, m_ref, score_fn=None, floor=None):
    running = None
    off = 0
    for idx, k in enumerate(k_chunks):
        s = _dot_nt(lhs, k)
        if idx == 0 and score_fn is not None:
            s = score_fn(s)
        n = k.shape[0]
        s_ref[:, off:off + n] = s
        for c in range(0, n, LANES):
            tile = s[:, c:c + LANES]
            running = tile if running is None else jnp.maximum(running, tile)
        off += n
    m = jnp.max(running, axis=-1, keepdims=True)
    if floor is not None:
        m = jnp.maximum(m, floor)
    m_ref[...] = jnp.broadcast_to(m, m_ref.shape)


def _values_stage(s_ref, m_ref, v_chunks):
    m = m_ref[...]
    acc = None
    off = 0
    for v in v_chunks:
        n = v.shape[0]
        p = jnp.concatenate([jnp.exp2(s_ref[:, off + c:off + c + LANES] - m)
                             for c in range(0, n, LANES)], axis=1).astype(BF16)
        part = _dot(p, v)
        acc = part if acc is None else acc + part
        off += n
    return acc


def _pair_select(o, n_q):
    lane = lax.broadcasted_iota(jnp.int32, (1, LANES), 1)
    return jnp.where(lane < HEAD_DIM, o[:n_q], o[n_q:2 * n_q])


def _sub_ln(y, gain, post_scale):
    return _head_rms(y, gain) * post_scale


def _lam_value(lamv_ref, lam_init):
    t1 = jnp.sum(lamv_ref[0:1, :] * lamv_ref[1:2, :], axis=-1, keepdims=True)
    t2 = jnp.sum(lamv_ref[2:3, :] * lamv_ref[3:4, :], axis=-1, keepdims=True)
    return jnp.exp(t1) - jnp.exp(t2) + lam_init


def _sink_rows(sink_ref, pair, n_q):
    row = lax.broadcasted_iota(jnp.int32, (2 * n_q, 1), 0)
    return jnp.where(row < n_q, sink_ref[2 * pair], sink_ref[2 * pair + 1]) * LOG2E


def _diff_combine(o, n_q, lam):
    r = o[:, :LANES] / o[:, LANES:]
    return r[:n_q] - lam * r[n_q:]


def _gqa_output(groups):
    return jnp.concatenate(_swap_middle_heads(*groups), axis=1).astype(BF16)


def _bias_kernel(rel_ref, o_ref):
    h = pl.program_id(0)
    n_dr, n_dc = 2 * NA_KR - 1, 2 * NA_KC - 1
    q_col = lax.broadcasted_iota(jnp.int32, (GRID_W, LANES), 0)
    lane = lax.broadcasted_iota(jnp.int32, (GRID_W, LANES), 1)
    k_col = lane & (GRID_W - 1)
    dc = k_col - q_col + (NA_KC - 1)
    c_start = jnp.clip(q_col - NA_KC // 2, 0, GRID_W - NA_KC)
    in_window = (k_col >= c_start) & (k_col < c_start + NA_KC)
    tiles = []
    for dr in range(n_dr):
        t = jnp.zeros((GRID_W, LANES), F32)
        for d in range(n_dc):
            t = jnp.where(dc == d, rel_ref[(h * n_dr + dr) * n_dc + d], t)
        tiles.append(jnp.where(in_window, t * LOG2E, NEG_INF))
    low = lane < GRID_W
    for off in range(NA_KR):
        for c in range(NA_KR * GRID_W // LANES):
            dr = 2 * c - off + NA_KR - 1
            o_ref[0, off, :, c * LANES:(c + 1) * LANES] = jnp.where(low, tiles[dr], tiles[dr + 1])


def _attn_a_kernel(seq, sink_ref, q_ref, k_ref, v_ref, kc_ref, vc_ref, o_ref,
                   vl_ext, vc_ext, *bufs):
    n_q = BLOCK
    span = BLOCK + 2 * WINDOW
    n_blocks = seq // n_q
    masks = _lane_masks(2, BF16)
    s_refs, p_refs, m_refs = (_by_parity(bufs[i:i + 4]) for i in (0, 4, 8))
    _prime_loop_pipeline(bufs[0:4], bufs[4:8], bufs[8:12])
    _fill_ext(vl_ext, v_ref[...])
    _fill_ext(vc_ext, vc_ref[...])

    def window(n):
        q0 = pl.multiple_of(n * n_q, n_q)
        return q0, pl.multiple_of(jnp.clip(q0 - WINDOW, 0, seq - span), BLOCK)

    def slot(t, new, old):
        q0, start = window(jnp.clip(t - 2, 0, n_blocks - 1))
        v_loc = vl_ext[pl.ds(start, span), :]
        groups = []
        for pair in range(2):
            o = _weighted_values(p_refs[new][pair], [v_loc, vc_ext[...]])
            den = o[:, LANES:] + jnp.exp2(_sink_rows(sink_ref, pair, n_q) - m_refs[new][pair][...])
            groups.append(_pair_select(o[:, :LANES] / den, n_q))
        o_ref[pl.ds(q0, n_q), :] = _gqa_output(groups)
        for pair in range(2):
            m = _numerators(s_refs[old][pair], p_refs[old][pair],
                            floor=_sink_rows(sink_ref, pair, n_q))
            m_refs[old][pair][...] = jnp.broadcast_to(m, (2 * n_q, LANES))
        q0, start = window(jnp.minimum(t, n_blocks - 1))
        q = q_ref[pl.ds(q0, n_q), :]
        k_loc = k_ref[pl.ds(start, span), :]
        q_pos = q0 + (lax.broadcasted_iota(jnp.int32, (2 * n_q, 1), 0) & (n_q - 1))
        k_pos = start + lax.broadcasted_iota(jnp.int32, (1, span), 1)
        valid = jnp.abs(k_pos - q_pos) <= WINDOW
        for pair in range(2):
            lhs = _stack_masked(q[:, pair * LANES:(pair + 1) * LANES], masks)
            _scores(lhs, [k_loc, kc_ref[...]], s_refs[new][pair],
                    score_fn=lambda s: jnp.where(valid, s, NEG_INF))

    _run_loop_slots(n_blocks, slot)


EXT_ROWS = 16


def _fill_ext_t(ext_ref, v):
    row = lax.broadcasted_iota(jnp.int32, (LANES, LANES), 0)
    col = lax.broadcasted_iota(jnp.int32, (LANES, LANES), 1)
    eye = jnp.where(row == col, 1.0, 0.0).astype(BF16)
    ext_ref[:LANES, :] = _dot_nt(eye, v).astype(BF16)
    ext_ref[LANES:, :] = jnp.ones((EXT_ROWS, v.shape[0]), BF16)


def _value_chunks_t(lat_ref, ctx_ref):
    n_lat = lat_ref.shape[1]
    return [lat_ref[:, r:r + KEY_CHUNK] for r in range(0, n_lat, KEY_CHUNK)] + [ctx_ref[...]]


def _scores_stage_t(lhs, k_chunks, s_ref, m_ref):
    running = None
    off = 0
    for k in k_chunks:
        s = _dot_nt(k, lhs)
        n = k.shape[0]
        s_ref[off:off + n, :] = s
        top = jnp.max(s, axis=0, keepdims=True)
        running = top if running is None else jnp.maximum(running, top)
        off += n
        yield
    m_ref[...] = jnp.broadcast_to(running, m_ref.shape)


def _values_stage_t(s_ref, m_ref, vt_chunks, result):
    m = m_ref[0:1, :]
    acc = None
    off = 0
    for vt in vt_chunks:
        n = vt.shape[1]
        p = jnp.exp2(s_ref[off:off + n, :] - m).astype(BF16)
        part = _dot(vt, p)
        acc = part if acc is None else acc + part
        off += n
        yield
    result.append(acc)


def _interleave(*stages):
    stages = list(stages)
    while stages:
        stages = [stage for stage in stages if next(stage, stages) is not stages]


def _dense_slots(n_q, qp_ref, qn_ref, kp_ref, kn_ref, kcp_ref, kcn_ref):
    return ((0, qp_ref, slice(n_q, 2 * n_q), kp_ref, kcp_ref),
            (1, qn_ref, slice(0, n_q), kn_ref, kcn_ref))


def _refill_ext_per_sample(blocks_per_seq, fill):
    pl.when(lax.rem(jnp.maximum(pl.program_id(0) - 1, 0), blocks_per_seq) == 0)(fill)


def _attn_b_kernel(blocks_per_seq, qp_ref, qn_ref, kp_ref, kn_ref, kcp_ref, kcn_ref, v_ref,
                   vc_ref, o_ref, vl_ext, vc_ext, *bufs):
    n_q = o_ref.shape[0] // 2
    s_refs, m_refs = _by_parity(bufs[0:4]), _by_parity(bufs[4:8])
    _zero_at_first_step(bufs)

    def fill():
        _fill_ext_t(vl_ext, v_ref[...])
        _fill_ext_t(vc_ext, vc_ref[...])

    _refill_ext_per_sample(blocks_per_seq, fill)
    masks = _lane_masks(2, BF16)
    for parity, q_ref, q_rows, k_ref, kc_ref in _dense_slots(n_q, qp_ref, qn_ref, kp_ref, kn_ref,
                                                            kcp_ref, kcn_ref):
        done, todo = parity, 1 - parity
        q = q_ref[q_rows, :]
        results, stages = [], []
        for pair in range(2):
            results.append([])
            stages.append(_values_stage_t(s_refs[done][pair], m_refs[done][pair],
                                          _value_chunks_t(vl_ext, vc_ext), results[pair]))
            lhs = _stack_masked(q[:, pair * LANES:(pair + 1) * LANES], masks)
            stages.append(_scores_stage_t(lhs, _key_chunks(k_ref, kc_ref),
                                          s_refs[todo][pair], m_refs[todo][pair]))
        _interleave(*stages)
        groups = []
        for (o_t,) in results:
            inv = 1.0 / o_t[LANES:LANES + 1, :]
            lo_t = o_t[:HEAD_DIM, :n_q] * inv[:, :n_q]
            hi_t = o_t[HEAD_DIM:LANES, n_q:] * inv[:, n_q:]
            groups.append(jnp.concatenate([lo_t, hi_t], axis=0).T)
        o_ref[parity * n_q:(parity + 1) * n_q, :] = _gqa_output(groups)


def _attn_d_kernel(lam_init, blocks_per_seq, lamv_ref, gain_ref, qp_ref, qn_ref, kp_ref, kn_ref,
                   kcp_ref, kcn_ref, v_ref, vc_ref, o_ref, vl_ext, vc_ext, *bufs):
    n_q = o_ref.shape[0] // 2
    s_refs, m_refs = _by_parity(bufs[0:8]), _by_parity(bufs[8:16])
    _zero_at_first_step(bufs)

    def fill():
        for jb in range(2):
            cols = slice(jb * LANES, (jb + 1) * LANES)
            _fill_ext_t(vl_ext.at[jb], v_ref[:, cols])
            _fill_ext_t(vc_ext.at[jb], vc_ref[:, cols])

    _refill_ext_per_sample(blocks_per_seq, fill)
    lam = _lam_value(lamv_ref, lam_init)
    masks = _lane_masks(4, BF16)
    for parity, q_ref, q_rows, k_ref, kc_ref in _dense_slots(n_q, qp_ref, qn_ref, kp_ref, kn_ref,
                                                            kcp_ref, kcn_ref):
        done, todo = parity, 1 - parity
        q = q_ref[q_rows, :]
        results, stages = [], []
        for jb in range(2):
            cols = slice(jb * LANES, (jb + 1) * LANES)
            for head in range(2):
                g = 2 * jb + head
                results.append([])
                stages.append(_values_stage_t(s_refs[done][g], m_refs[done][g],
                                              _value_chunks_t(vl_ext.at[jb], vc_ext.at[jb]),
                                              results[g]))
                lhs = _stack_masked(q[:, cols], masks[2 * head:2 * head + 2])
                stages.append(_scores_stage_t(lhs, _key_chunks(k_ref, kc_ref, cols),
                                              s_refs[todo][g], m_refs[todo][g]))
        _interleave(*stages)
        outs = []
        for jb in range(2):
            ys_t = []
            for head in range(2):
                (o_t,) = results[2 * jb + head]
                dims = slice(head * HEAD_DIM, (head + 1) * HEAD_DIM)
                r = o_t[dims, :] * (1.0 / o_t[LANES:LANES + 1, :])
                ys_t.append(r[:, :n_q] - lam * r[:, n_q:])
            y = jnp.concatenate(ys_t, axis=0).T
            outs.append(_sub_ln(y, gain_ref[...], 1.0 - lam_init))
        o_ref[parity * n_q:(parity + 1) * n_q, :] = jnp.concatenate(outs, axis=1).astype(BF16)


def _attn_c_kernel(n_grid_rows, q_ref, k_ref, v_ref, kc_ref, vc_ref, bias_ref,
                   o_ref, vl_ext, vc_ext, *bufs):
    n_q = GRID_W
    n_loc = NA_KR * GRID_W
    masks = _lane_masks(2, BF16)
    s_refs, p_refs = _by_parity(bufs[0:4]), _by_parity(bufs[4:8])
    _prime_loop_pipeline(bufs[0:4], bufs[4:8])
    for jb in range(2):
        cols = slice(jb * LANES, (jb + 1) * LANES)
        _fill_ext(vl_ext.at[jb], v_ref[:, cols])
        _fill_ext(vc_ext.at[jb], vc_ref[:, cols])

    def geometry(r):
        r_start = jnp.clip(r - NA_KR // 2, 0, n_grid_rows - NA_KR)
        return (pl.multiple_of(r * n_q, n_q), pl.multiple_of(r_start * GRID_W, GRID_W),
                r - r_start)

    def slot(t, new, old):
        q0, k0, _ = geometry(jnp.clip(t - 2, 0, n_grid_rows - 1))
        outs = []
        for jb in range(2):
            o = _weighted_values(p_refs[new][jb], [vl_ext[jb, pl.ds(k0, n_loc), :], vc_ext[jb]])
            outs.append(_pair_select(o[:, :LANES] / o[:, LANES:], n_q))
        o_ref[pl.ds(q0, n_q), :] = jnp.concatenate(outs, axis=1).astype(BF16)
        for jb in range(2):
            _numerators(s_refs[old][jb], p_refs[old][jb])
        q0, k0, off = geometry(jnp.minimum(t, n_grid_rows - 1))
        q = q_ref[pl.ds(q0, n_q), :]
        for jb in range(2):
            cols = slice(jb * LANES, (jb + 1) * LANES)
            lhs = _stack_masked(q[:, cols], masks)
            bias = jnp.concatenate([bias_ref[2 * jb, off], bias_ref[2 * jb + 1, off]], axis=0)
            _scores(lhs, [k_ref[pl.ds(k0, n_loc), cols], kc_ref[:, cols]], s_refs[new][jb],
                    score_fn=lambda s: s + bias)

    _run_loop_slots(n_grid_rows, slot)


def _attn_ctx_kernel(lam_init, sink_ref, lamv_ref, gain_ref, aq_ref, ak_ref, av_ref, bq_ref, bk_ref,
                     bv_ref, cq_ref, ck_ref, cv_ref, dq_ref, dk_ref, dv_ref, o_ref,
                     s0, s1, m0, m1):
    n_q = aq_ref.shape[0]
    m2 = _lane_masks(2, BF16)
    m4 = _lane_masks(4, BF16)
    lane = lax.broadcasted_iota(jnp.int32, (1, LANES), 1)
    sets = ((s0, m0), (s1, m1))

    def ext(v):
        return jnp.concatenate([v, jnp.ones(v.shape, v.dtype)], axis=1)

    def attend(lhs, k, v_ext, s_ref, m_ref, floor=None):
        _scores_stage(lhs, [k], s_ref, m_ref, floor=floor)
        return _values_stage(s_ref, m_ref, [v_ext])

    def out_cols(group, pair):
        c0 = group * GROUP_WIDTH + pair * LANES
        return slice(c0, c0 + LANES)

    for group, (q_ref_, k_ref_, v_ref_) in enumerate(((aq_ref, ak_ref, av_ref),
                                                      (bq_ref, bk_ref, bv_ref))):
        q = q_ref_[...]
        v_ext = ext(v_ref_[...])
        groups = []
        for pair, (s_ref, m_ref) in enumerate(sets):
            lhs = _stack_masked(q[:, pair * LANES:(pair + 1) * LANES], m2)
            sink = _sink_rows(sink_ref, pair, n_q) if group == 0 else None
            o = attend(lhs, k_ref_[...], v_ext, s_ref, m_ref, floor=sink)
            den = o[:, LANES:]
            if group == 0:
                den = den + jnp.exp2(sink - m_ref[...])
            groups.append(_pair_select(o[:, :LANES] / den, n_q))
        o_ref[:, group * GROUP_WIDTH:(group + 1) * GROUP_WIDTH] = _gqa_output(groups)

    q = cq_ref[...]
    for jb, (s_ref, m_ref) in enumerate(sets):
        cols = slice(jb * LANES, (jb + 1) * LANES)
        o = attend(_stack_masked(q[:, cols], m2), ck_ref[:, cols], ext(cv_ref[:, cols]),
                   s_ref, m_ref)
        o_ref[:, out_cols(2, jb)] = _pair_select(o[:, :LANES] / o[:, LANES:], n_q).astype(BF16)

    lam = _lam_value(lamv_ref, lam_init)
    q = dq_ref[...]
    for jb in range(2):
        cols = slice(jb * LANES, (jb + 1) * LANES)
        v_ext = ext(dv_ref[:, cols])
        ys = []
        for head, (s_ref, m_ref) in enumerate(sets):
            lhs = _stack_masked(q[:, cols], m4[2 * head:2 * head + 2])
            ys.append(_diff_combine(attend(lhs, dk_ref[:, cols], v_ext, s_ref, m_ref), n_q, lam))
        y = jnp.where(lane < HEAD_DIM, ys[0], ys[1])
        o_ref[:, out_cols(3, jb)] = _sub_ln(y, gain_ref[...], 1.0 - lam_init).astype(BF16)


def _rope_tables(seq, n_extra):
    t = np.arange(seq)
    row = (t // GRID_W).astype(np.float64)
    col = (t % GRID_W).astype(np.float64)
    tables = []
    for dim in (HEAD_DIM, D_SUB):
        half = dim // 2
        freqs = ROPE_BASE ** (-np.arange(0, half, 2, dtype=np.float64) / half)
        ang_r = row[:, None] * freqs[None, :]
        ang_c = col[:, None] * freqs[None, :]
        ang = np.concatenate([ang_r, ang_r, ang_c, ang_c], axis=-1)
        quarter = dim // 4
        sign = np.where((np.arange(dim) % (2 * quarter)) < quarter, -1.0, 1.0)
        cos = np.tile(np.cos(ang), (1, LANES // dim))
        sin = np.tile(np.sin(ang) * sign[None, :], (1, LANES // dim))
        cos = np.concatenate([cos, np.ones((n_extra, LANES))], axis=0)
        sin = np.concatenate([sin, np.zeros((n_extra, LANES))], axis=0)
        tables += [cos, sin]
    return jnp.asarray(np.stack(tables), dtype=F32)


def _neighbourhood_bias(rel_bias):
    heads = rel_bias.shape[0]
    return pl.pallas_call(
        _bias_kernel,
        grid=(heads,),
        in_specs=[pl.BlockSpec(memory_space=pltpu.SMEM)],
        out_specs=pl.BlockSpec((1, NA_KR, GRID_W, NA_KR * GRID_W), lambda h: (h, 0, 0, 0)),
        out_shape=jax.ShapeDtypeStruct((heads, NA_KR, GRID_W, NA_KR * GRID_W), F32),
        compiler_params=_params(("arbitrary",)),
        name="neighbourhood_bias",
    )(rel_bias.astype(F32).reshape(-1))


def _stage_scratch(n_rows, n_keys, n_groups=2):
    return ([pltpu.VMEM((n_rows, n_keys), F32)] * (2 * n_groups)
            + [pltpu.VMEM((n_rows, LANES), F32)] * (2 * n_groups))


def _stage_scratch_t(n_rows, n_keys, n_groups=2):
    return ([pltpu.VMEM((n_keys, n_rows), F32)] * (2 * n_groups)
            + [pltpu.VMEM((8, n_rows), F32)] * (2 * n_groups))


def _loop_scratch(n_rows, n_keys, with_maxima=False):
    maxima = [pltpu.VMEM((n_rows, LANES), F32)] * 4 if with_maxima else []
    return ([pltpu.VMEM((n_rows, n_keys), F32)] * 4 + [pltpu.VMEM((n_rows, n_keys), BF16)] * 4
            + maxima)


def kernel(x, c, ctx, c_ctx, w_ada, b_ada, w_ffn1_gate, w_ffn1_up, w_ffn1_down, w_in, w_out,
           sink_logit, q_norm_g, k_norm_g, rel_pos_bias, lam_q1, lam_k1, lam_q2, lam_k2, subln_g,
           w_ffn2_gate, w_ffn2_up, w_ffn2_down, final_norm_g):
    batch, seq, d = x.shape
    n_ctx = ctx.shape[1]
    depth = w_ada.shape[0]
    assert d == D_MODEL and seq % TOKEN_TILE == 0 and (batch * n_ctx) % TOKEN_TILE == 0
    assert seq % GRID_W == 0 and w_in.shape[-1] == IN_WIDTH and seq % KEY_CHUNK == 0
    assert seq % (2 * GQA_TILE) == 0 and seq % (2 * DIFF_TILE) == 0
    assert seq % (2 * BLOCK) == 0 and seq % (2 * GRID_W) == 0
    t_lat, t_ctx = batch * seq, batch * n_ctx
    t_all = t_lat + t_ctx
    tm = TOKEN_TILE
    n_lat_tiles, n_ctx_tiles = t_lat // tm, t_ctx // tm
    tiles_per_seq = seq // tm
    grid_rows = seq // GRID_W
    assert grid_rows >= NA_KR and rel_pos_bias.shape[1:] == (GROUP_HEADS, 2 * NA_KR - 1, 2 * NA_KC - 1)

    cc = jnp.concatenate([c, c_ctx[None, :], jnp.zeros((16 - batch - 1, d), F32)], axis=0)
    mods = _mods(cc, w_ada, b_ada).reshape(depth, 16, N_MOD, d)

    def group(i):
        return jnp.minimum(i // tiles_per_seq, batch)

    rope = _rope_tables(seq, tm)

    def rope_block(i):
        return jnp.where(i < n_lat_tiles, i % tiles_per_seq, tiles_per_seq)

    lane_gain = lambda g: jnp.tile(g.astype(F32), LANES // HEAD_DIM)[None, :]

    tile_spec = pl.BlockSpec((tm, d), lambda i: (i, 0))

    ffn1 = tuple(_to_bf16(w) for w in (w_ffn1_gate, w_ffn1_up, w_ffn1_down))
    ffn2 = tuple(_to_bf16(w) for w in (w_ffn2_gate, w_ffn2_up, w_ffn2_down))
    w_in_b = _to_bf16(w_in, swap_cols=(COL_AQ * LANES, COL_BQ * LANES))
    w_out_b = _to_bf16(w_out)

    for l in range(depth):
        last = l == depth - 1
        lam_init = 0.8 - 0.6 * math.exp(-0.3 * l)
        mod_spec = pl.BlockSpec((1, N_MOD, d), lambda i: (group(i), 0, 0))

        gains = jnp.concatenate([lane_gain(q_norm_g[l]), lane_gain(k_norm_g[l]),
                                 jnp.zeros((6, LANES), F32)], axis=0)
        if l == 0:
            streams = (x.reshape(t_lat, d), ctx.reshape(t_ctx, d))
            stream_specs = [
                pl.BlockSpec((tm, d), lambda i: (jnp.minimum(i, n_lat_tiles - 1), 0)),
                pl.BlockSpec((tm, d), lambda i: (jnp.maximum(i - n_lat_tiles, 0), 0))]
        else:
            streams, stream_specs = (xs,), [tile_spec]
        xs, qkv = pl.pallas_call(
            functools.partial(_pre_kernel, n_lat_tiles if l == 0 else None),
            grid=(n_lat_tiles + n_ctx_tiles,),
            in_specs=stream_specs + [
                mod_spec,
                _resident((d, D_FF), l), _resident((d, D_FF), l), _resident((D_FF, d), l),
                _resident((d, IN_WIDTH), l),
                pl.BlockSpec((4, tm, LANES), lambda i: (0, rope_block(i), 0)),
                _resident((8, LANES)),
            ],
            out_specs=[tile_spec, pl.BlockSpec((tm, IN_WIDTH), lambda i: (i, 0))],
            out_shape=[jax.ShapeDtypeStruct((t_all, d), F32),
                       jax.ShapeDtypeStruct((t_all, IN_WIDTH), BF16)],
            scratch_shapes=[pltpu.VMEM((tm, D_FF), BF16)],
            compiler_params=_params(("arbitrary",)),
            name=f"pre_l{l}",
        )(*streams, mods[l], *ffn1, w_in_b, rope, gains)

        ctx_row = t_lat // n_ctx
        sink_perm = sink_logit[l].astype(F32)[jnp.array(GQA_HEAD_ORDER)]
        lamv = jnp.pad(jnp.stack([lam_q1[l], lam_k1[l], lam_q2[l], lam_k2[l]]).astype(F32),
                       ((0, 4), (0, LANES - D_SUB)))
        subln = lane_gain(subln_g[l])
        smem = pl.BlockSpec(memory_space=pltpu.SMEM)

        def lat(cols, col):
            return pl.BlockSpec((seq, cols), lambda b, *_: (b, col))

        def ctxb(cols, col):
            return pl.BlockSpec((n_ctx, cols), lambda b, *_: (ctx_row + b, col))

        def v_ext_scratch(n_blocks):
            lead = () if n_blocks == 1 else (n_blocks,)
            return [pltpu.VMEM(lead + (seq, 2 * LANES), BF16),
                    pltpu.VMEM(lead + (n_ctx, 2 * LANES), BF16)]

        y_a = pl.pallas_call(
            functools.partial(_attn_a_kernel, seq),
            grid=(batch,),
            in_specs=[smem, lat(2 * LANES, COL_AQ // 2), lat(LANES, COL_AK), lat(LANES, COL_AV),
                      ctxb(LANES, COL_AK), ctxb(LANES, COL_AV)],
            out_specs=pl.BlockSpec((seq, GROUP_WIDTH), lambda b: (b, 0)),
            out_shape=jax.ShapeDtypeStruct((t_lat, GROUP_WIDTH), BF16),
            scratch_shapes=(v_ext_scratch(1)
                            + _loop_scratch(2 * BLOCK, BLOCK + 2 * WINDOW + n_ctx,
                                            with_maxima=True)),
            compiler_params=_params(("arbitrary",)),
            name=f"attn_a_l{l}",
        )(sink_perm, qkv, qkv, qkv, qkv, qkv)

        def v_ext_t_scratch(n_blocks):
            lead = () if n_blocks == 1 else (n_blocks,)
            return [pltpu.VMEM(lead + (LANES + EXT_ROWS, seq), BF16),
                    pltpu.VMEM(lead + (LANES + EXT_ROWS, n_ctx), BF16)]

        def dense_mixer(body, name, tq, n_groups, q_col, k_cols, k_col, v_col, params=(),
                        ext_blocks=0):
            blocks_per_seq = seq // (2 * tq)
            n_blocks = batch * blocks_per_seq

            def cur(j):
                return jnp.minimum(j, n_blocks - 1)

            def prev(j):
                return jnp.maximum(j - 1, 0)

            def sample_of(block):
                return lambda j: block(j) // blocks_per_seq

            specs = [pl.BlockSpec((2 * tq, 2 * LANES), lambda j: (prev(j), q_col)),
                     pl.BlockSpec((2 * tq, 2 * LANES), lambda j: (cur(j), q_col))]
            for rows, row0 in ((seq, 0), (n_ctx, ctx_row)):
                for block in (prev, cur):
                    specs.append(pl.BlockSpec(
                        (rows, k_cols), lambda j, b=sample_of(block), r=row0: (r + b(j), k_col)))
            for rows, row0 in ((seq, 0), (n_ctx, ctx_row)):
                specs.append(pl.BlockSpec(
                    (rows, k_cols), lambda j, b=sample_of(prev), r=row0: (r + b(j), v_col)))
            return pl.pallas_call(
                functools.partial(body, blocks_per_seq),
                grid=(n_blocks + 1,),
                in_specs=[_resident(p.shape) for p in params] + specs,
                out_specs=pl.BlockSpec((2 * tq, GROUP_WIDTH), lambda j: (prev(j), 0)),
                out_shape=jax.ShapeDtypeStruct((t_lat, GROUP_WIDTH), BF16),
                scratch_shapes=(v_ext_t_scratch(ext_blocks)
                                + _stage_scratch_t(2 * tq, seq + n_ctx, n_groups=n_groups)),
                compiler_params=_params(("arbitrary",)),
                name=name,
            )(*params, *([qkv] * 8))

        y_b = dense_mixer(_attn_b_kernel, f"attn_b_l{l}", GQA_TILE, 2,
                          COL_BQ // 2, LANES, COL_BK, COL_BV, ext_blocks=1)
        y_d = dense_mixer(functools.partial(_attn_d_kernel, lam_init), f"attn_d_l{l}", DIFF_TILE, 4,
                          COL_DQ // 2, 2 * LANES, COL_DK // 2, COL_DV // 2, params=(lamv, subln),
                          ext_blocks=2)

        bias = _neighbourhood_bias(rel_pos_bias[l])
        y_c = pl.pallas_call(
            functools.partial(_attn_c_kernel, grid_rows),
            grid=(batch,),
            in_specs=[lat(2 * LANES, COL_CQ // 2),
                      lat(2 * LANES, COL_CK // 2), lat(2 * LANES, COL_CV // 2),
                      ctxb(2 * LANES, COL_CK // 2), ctxb(2 * LANES, COL_CV // 2),
                      _resident(bias.shape)],
            out_specs=pl.BlockSpec((seq, GROUP_WIDTH), lambda b: (b, 0)),
            out_shape=jax.ShapeDtypeStruct((t_lat, GROUP_WIDTH), BF16),
            scratch_shapes=(v_ext_scratch(2)
                            + _loop_scratch(2 * GRID_W, NA_KR * GRID_W + n_ctx)),
            compiler_params=_params(("arbitrary",)),
            name=f"attn_c_l{l}",
        )(qkv, qkv, qkv, qkv, qkv, bias)

        ffn2_specs = [_resident((d, D_FF), l), _resident((d, D_FF), l), _resident((D_FF, d), l)]
        fng = final_norm_g.astype(F32)[None, :]
        y_spec = pl.BlockSpec((tm, GROUP_WIDTH), lambda i: (jnp.minimum(i, n_lat_tiles - 1), 0))
        post_scratch = [pltpu.VMEM((tm, d), BF16), pltpu.VMEM((tm, D_FF), BF16)]

        if not last:
            y_ctx = pl.pallas_call(
                functools.partial(_attn_ctx_kernel, lam_init),
                grid=(batch,),
                in_specs=[smem, _resident((8, LANES)), _resident((1, LANES)),
                          ctxb(2 * LANES, COL_AQ // 2), ctxb(LANES, COL_AK), ctxb(LANES, COL_AV),
                          ctxb(2 * LANES, COL_BQ // 2), ctxb(LANES, COL_BK), ctxb(LANES, COL_BV),
                          ctxb(2 * LANES, COL_CQ // 2), ctxb(2 * LANES, COL_CK // 2),
                          ctxb(2 * LANES, COL_CV // 2),
                          ctxb(2 * LANES, COL_DQ // 2), ctxb(2 * LANES, COL_DK // 2),
                          ctxb(2 * LANES, COL_DV // 2)],
                out_specs=pl.BlockSpec((n_ctx, d), lambda b: (b, 0)),
                out_shape=jax.ShapeDtypeStruct((t_ctx, d), BF16),
                scratch_shapes=_stage_scratch(2 * n_ctx, n_ctx, n_groups=1),
                compiler_params=_params(("arbitrary",)),
                name=f"attn_ctx_l{l}",
            )(sink_perm, lamv, subln, *([qkv] * 12))

            xs = pl.pallas_call(
                functools.partial(_post_kernel, n_lat_tiles, False),
                grid=(n_lat_tiles + n_ctx_tiles,),
                in_specs=[tile_spec, mod_spec, y_spec, y_spec, y_spec, y_spec,
                          pl.BlockSpec((tm, d), lambda i: (jnp.maximum(i - n_lat_tiles, 0), 0)),
                          _resident((d, d), l)] + ffn2_specs + [_resident((1, d))],
                out_specs=tile_spec,
                out_shape=jax.ShapeDtypeStruct((t_all, d), F32),
                scratch_shapes=post_scratch,
                compiler_params=_params(("arbitrary",)),
                name=f"post_l{l}",
            )(xs, mods[l], y_a, y_b, y_c, y_d, y_ctx, w_out_b, *ffn2, fng)
        else:
            xs = pl.pallas_call(
                functools.partial(_post_kernel, None, True),
                grid=(n_lat_tiles,),
                in_specs=[tile_spec, mod_spec, y_spec, y_spec, y_spec, y_spec,
                          _resident((d, d), l)] + ffn2_specs + [_resident((1, d))],
                out_specs=tile_spec,
                out_shape=jax.ShapeDtypeStruct((t_lat, d), F32),
                scratch_shapes=post_scratch,
                compiler_params=_params(("arbitrary",)),
                name=f"post_l{l}",
            )(xs, mods[l], y_a, y_b, y_c, y_d, w_out_b, *ffn2, fng)

    return xs.reshape(batch, seq, d)
```

```python
import functools
import math

import numpy as np
import jax
import jax.numpy as jnp
from jax import lax
from jax.experimental import pallas as pl
from jax.experimental.pallas import tpu as pltpu

D_MODEL = 1024
GRID_W = 64
HEAD_DIM = 64
N_GROUPS = 4
GROUP_HEADS = D_MODEL // (N_GROUPS * HEAD_DIM)
GROUP_WIDTH = GROUP_HEADS * HEAD_DIM
D_SUB = HEAD_DIM // 2
WINDOW = 128
BLOCK = 128
NA_KR = 8
NA_KC = 16
D_FF = 2816
ROPE_BASE = 10000.0
NORM_EPS = 1e-6
N_MOD = 9
NEG_INF = -1e30
IN_WIDTH = 2560

LANES = 128
MXU_N = 256
TOKEN_TILE = 512
GQA_TILE = 256
DIFF_TILE = 128
KEY_CHUNK = 512
VMEM_LIMIT = 56 * 1024 * 1024
LOG2E = 1.4426950408889634

BF16 = jnp.bfloat16
F32 = jnp.float32

COL_AQ, COL_AK, COL_AV = 0, 2, 3
COL_BQ, COL_BK, COL_BV = 4, 6, 7
COL_CQ, COL_CK, COL_CV = 8, 10, 12
COL_DQ, COL_DK, COL_DV = 14, 16, 18

GQA_HEAD_ORDER = (0, 2, 1, 3)


def _dot(a, b):
    return jnp.dot(a, b, preferred_element_type=F32)


def _dot_nt(a, b):
    return lax.dot_general(a, b, (((1,), (1,)), ((), ())), preferred_element_type=F32)


def _params(semantics):
    return pltpu.CompilerParams(dimension_semantics=semantics, vmem_limit_bytes=VMEM_LIMIT)


def _resident(shape, layer=None):
    nd = len(shape)
    if layer is None:
        return pl.BlockSpec(shape, lambda *_: (0,) * nd, pipeline_mode=pl.Buffered(1))
    return pl.BlockSpec((None,) + tuple(shape), lambda *_: (layer,) + (0,) * nd,
                        pipeline_mode=pl.Buffered(1))


def _swap_middle_heads(lo, hi):
    lane = lax.broadcasted_iota(jnp.int32, (1, LANES), 1)
    low = lane < HEAD_DIM
    return (jnp.where(low, lo, pltpu.roll(hi, HEAD_DIM, 1)),
            jnp.where(low, pltpu.roll(lo, HEAD_DIM, 1), hi))


CAST_BLOCK_BYTES = 6 * 1024 * 1024


def _cast_kernel(swap_cols, w_ref, o_ref):
    o_ref[...] = w_ref[...].astype(BF16)
    for c0 in swap_cols:
        lo, hi = _swap_middle_heads(w_ref[:, c0:c0 + LANES], w_ref[:, c0 + LANES:c0 + 2 * LANES])
        o_ref[:, c0:c0 + LANES] = lo.astype(BF16)
        o_ref[:, c0 + LANES:c0 + 2 * LANES] = hi.astype(BF16)


def _to_bf16(w, swap_cols=()):
    depth, r, c = w.shape
    tr = max(t for t in range(16, r + 1, 16) if r % t == 0 and t * c * 4 <= CAST_BLOCK_BYTES)
    spec = pl.BlockSpec((None, tr, c), lambda l, i: (l, i, 0))
    return pl.pallas_call(
        functools.partial(_cast_kernel, tuple(swap_cols)),
        grid=(depth, r // tr),
        in_specs=[spec],
        out_specs=spec,
        out_shape=jax.ShapeDtypeStruct(w.shape, BF16),
        compiler_params=_params(("arbitrary", "arbitrary")),
        name="cast_bf16",
    )(w)


def _mods_kernel(c_ref, w_ref, b_ref, o_ref):
    c = c_ref[...]
    a = c * (1.0 / (1.0 + jnp.exp(-c)))
    o_ref[...] = _dot(a.astype(BF16), w_ref[...].astype(BF16)) + b_ref[...]


def _mods(cc, w_ada, b_ada):
    depth, d, n = w_ada.shape
    rows = cc.shape[0]
    tn = 1152
    return pl.pallas_call(
        _mods_kernel,
        grid=(depth, n // tn),
        in_specs=[
            pl.BlockSpec((rows, d), lambda l, j: (0, 0)),
            pl.BlockSpec((None, d, tn), lambda l, j: (l, 0, j)),
            pl.BlockSpec((None, 1, tn), lambda l, j: (l, 0, j)),
        ],
        out_specs=pl.BlockSpec((None, rows, tn), lambda l, j: (l, 0, j)),
        out_shape=jax.ShapeDtypeStruct((depth, rows, n), F32),
        compiler_params=_params(("arbitrary", "arbitrary")),
        name="adaln_mods",
    )(cc, w_ada, b_ada.reshape(depth, 1, n))


def _modulated(x, mod_ref, k):
    shift = mod_ref[0, k:k + 1, :]
    scale = mod_ref[0, k + 1:k + 2, :]
    ms = jnp.mean(x * x, axis=-1, keepdims=True)
    return (x * lax.rsqrt(ms + NORM_EPS)) * (1.0 + scale) + shift


def _swiglu(h, wg_ref, wu_ref, wd_ref, act_ref):
    hb = h.astype(BF16)
    for j in range(D_FF // MXU_N):
        cols = slice(j * MXU_N, (j + 1) * MXU_N)
        g = _dot(hb, wg_ref[:, cols])
        u = _dot(hb, wu_ref[:, cols])
        act_ref[:, cols] = ((g * (1.0 / (1.0 + jnp.exp(-g)))) * u).astype(BF16)
    return _dot(act_ref[...], wd_ref[...])


def _rope(v, cos, sin_signed, quarter):
    lane = lax.broadcasted_iota(jnp.int32, (1, LANES), 1)
    first = (lane & (2 * quarter - 1)) < quarter
    rot = jnp.where(first, pltpu.roll(v, LANES - quarter, 1), pltpu.roll(v, quarter, 1))
    return v * cos + rot * sin_signed


def _head_rms(v, gain):
    lane = lax.broadcasted_iota(jnp.int32, (1, LANES), 1)
    lo = lane < HEAD_DIM
    sq = v * v
    ms_lo = jnp.sum(jnp.where(lo, sq, 0.0), axis=-1, keepdims=True) * (1.0 / HEAD_DIM)
    ms_hi = jnp.sum(jnp.where(lo, 0.0, sq), axis=-1, keepdims=True) * (1.0 / HEAD_DIM)
    rs = jnp.where(lo, lax.rsqrt(ms_lo + NORM_EPS), lax.rsqrt(ms_hi + NORM_EPS))
    return v * rs * gain


_QK_SCALE = HEAD_DIM ** -0.5 * LOG2E
_SUB_SCALE = D_SUB ** -0.5 * LOG2E
_PROJ_BLOCKS = (
    (16, -1, _QK_SCALE), (16, -1, _QK_SCALE), (16, -1, 1.0), (0, -1, 1.0),
    (16, 0, _QK_SCALE), (16, 0, _QK_SCALE), (16, 1, 1.0), (0, -1, 1.0),
    (0, -1, _QK_SCALE), (0, -1, _QK_SCALE), (0, -1, 1.0), (0, -1, 1.0),
    (0, -1, 1.0), (0, -1, 1.0),
    (8, -1, _SUB_SCALE), (8, -1, _SUB_SCALE), (8, -1, 1.0), (8, -1, 1.0),
    (0, -1, 1.0), (0, -1, 1.0),
)


def _project(h, w_ref, rope_ref, gain_ref, o_ref):
    hb = h.astype(BF16)
    groups = sorted(range(IN_WIDTH // MXU_N),
                    key=lambda j: -sum((q > 0) + 2 * (g >= 0) for q, g, _ in _PROJ_BLOCKS[2 * j:2 * j + 2]))
    for j in groups:
        r = _dot(hb, w_ref[:, j * MXU_N:(j + 1) * MXU_N])
        for half in range(MXU_N // LANES):
            blk = j * (MXU_N // LANES) + half
            quarter, gain_idx, scale = _PROJ_BLOCKS[blk]
            v = r[:, half * LANES:(half + 1) * LANES]
            if gain_idx >= 0:
                v = _head_rms(v, gain_ref[gain_idx:gain_idx + 1, :])
            if quarter:
                t = 0 if quarter == 16 else 2
                v = _rope(v, rope_ref[t], rope_ref[t + 1], quarter)
            if scale != 1.0:
                v = v * scale
            o_ref[:, blk * LANES:(blk + 1) * LANES] = v.astype(BF16)


def _pre_kernel(n_lat_tiles, *refs):
    if n_lat_tiles is None:
        (x_ref, mod_ref, wg_ref, wu_ref, wd_ref, win_ref, rope_ref, gain_ref,
         xo_ref, qkv_ref, act_ref) = refs
        x = x_ref[...]
    else:
        (x_ref, c_ref, mod_ref, wg_ref, wu_ref, wd_ref, win_ref, rope_ref, gain_ref,
         xo_ref, qkv_ref, act_ref) = refs
        x = jnp.where(pl.program_id(0) < n_lat_tiles, x_ref[...], c_ref[...])
    y = _swiglu(_modulated(x, mod_ref, 0), wg_ref, wu_ref, wd_ref, act_ref)
    x = x + (0.5 * mod_ref[0, 2:3, :]) * y
    xo_ref[...] = x
    _project(_modulated(x, mod_ref, 3), win_ref, rope_ref, gain_ref, qkv_ref)


def _post_kernel(n_lat_tiles, final_norm, *refs):
    if n_lat_tiles is None:
        (x_ref, mod_ref, ya_ref, yb_ref, yc_ref, yd_ref, wout_ref, wg_ref, wu_ref, wd_ref,
         fng_ref, xo_ref, y_scr, act_ref) = refs
        y_scr[...] = jnp.concatenate([ya_ref[...], yb_ref[...], yc_ref[...], yd_ref[...]], axis=1)
    else:
        (x_ref, mod_ref, ya_ref, yb_ref, yc_ref, yd_ref, yctx_ref, wout_ref, wg_ref, wu_ref,
         wd_ref, fng_ref, xo_ref, y_scr, act_ref) = refs
        is_latent = pl.program_id(0) < n_lat_tiles

        @pl.when(is_latent)
        def _():
            y_scr[...] = jnp.concatenate(
                [ya_ref[...], yb_ref[...], yc_ref[...], yd_ref[...]], axis=1)

        @pl.when(jnp.logical_not(is_latent))
        def _():
            y_scr[...] = yctx_ref[...]

    x = x_ref[...]
    x = x + mod_ref[0, 5:6, :] * _dot(y_scr[...], wout_ref[...])
    y = _swiglu(_modulated(x, mod_ref, 6), wg_ref, wu_ref, wd_ref, act_ref)
    x = x + (0.5 * mod_ref[0, 8:9, :]) * y
    if final_norm:
        ms = jnp.mean(x * x, axis=-1, keepdims=True)
        x = (x * lax.rsqrt(ms + NORM_EPS)) * fng_ref[...]
    xo_ref[...] = x


def _lane_masks(n, dtype):
    lane = lax.broadcasted_iota(jnp.int32, (1, LANES), 1)
    w = LANES // n
    return [jnp.where((lane >= k * w) & (lane < (k + 1) * w), 1.0, 0.0).astype(dtype)
            for k in range(n)]


def _stack_masked(q, masks):
    return jnp.concatenate([q * m for m in masks], axis=0)


def _fill_ext(ext_ref, v):
    ext_ref[:, :LANES] = v
    ext_ref[:, LANES:] = jnp.ones(v.shape, v.dtype)


def _by_parity(refs):
    half = len(refs) // 2
    return (tuple(refs[:half]), tuple(refs[half:]))


def _zero_at_first_step(refs):
    @pl.when(pl.program_id(0) == 0)
    def _():
        for ref in refs:
            ref[...] = jnp.zeros(ref.shape, ref.dtype)


ROW_BLOCK = 32


def _scores(lhs, k_list, s_ref, score_fn=None):
    off = 0
    for idx, k in enumerate(k_list):
        s = _dot_nt(lhs, k)
        if idx == 0 and score_fn is not None:
            s = score_fn(s)
        s_ref[:, off:off + k.shape[0]] = s
        off += k.shape[0]


def _numerators(s_ref, p_ref, floor=None):
    maxima = []
    for r in range(0, s_ref.shape[0], ROW_BLOCK):
        rows = slice(r, r + ROW_BLOCK)
        m = jnp.max(s_ref[rows, :], axis=-1, keepdims=True)
        if floor is not None:
            m = jnp.maximum(m, floor[rows])
        p_ref[rows, :] = jnp.exp2(s_ref[rows, :] - m).astype(BF16)
        maxima.append(m)
    return jnp.concatenate(maxima, axis=0)


def _weighted_values(p_ref, v_list):
    o = None
    off = 0
    for v in v_list:
        part = _dot(p_ref[:, off:off + v.shape[0]], v)
        o = part if o is None else o + part
        off += v.shape[0]
    return o


def _key_chunks(lat_ref, ctx_ref, cols=slice(None)):
    n_lat = lat_ref.shape[0]
    return ([lat_ref[r:r + KEY_CHUNK, cols] for r in range(0, n_lat, KEY_CHUNK)]
            + [ctx_ref[:, cols]])


def _scores_stage(lhs, k_chunks, s_ref, m_ref, score_fn=None, floor=None):
    running = None
    off = 0
    for idx, k in enumerate(k_chunks):
        s = _dot_nt(lhs, k)
        if idx == 0 and score_fn is not None:
            s = score_fn(s)
        n = k.shape[0]
        s_ref[:, off:off + n] = s
        for c in range(0, n, LANES):
            tile = s[:, c:c + LANES]
            running = tile if running is None else jnp.maximum(running, tile)
        off += n
    m = jnp.max(running, axis=-1, keepdims=True)
    if floor is not None:
        m = jnp.maximum(m, floor)
    m_ref[...] = jnp.broadcast_to(m, m_ref.shape)


def _values_stage(s_ref, m_ref, v_chunks):
    m = m_ref[...]
    acc = None
    off = 0
    for v in v_chunks:
        n = v.shape[0]
        p = jnp.concatenate([jnp.exp2(s_ref[:, off + c:off + c + LANES] - m)
                             for c in range(0, n, LANES)], axis=1).astype(BF16)
        part = _dot(p, v)
        acc = part if acc is None else acc + part
        off += n
    return acc


def _pair_select(o, n_q):
    lane = lax.broadcasted_iota(jnp.int32, (1, LANES), 1)
    return jnp.where(lane < HEAD_DIM, o[:n_q], o[n_q:2 * n_q])


def _sub_ln(y, gain, post_scale):
    return _head_rms(y, gain) * post_scale


def _lam_value(lamv_ref, lam_init):
    t1 = jnp.sum(lamv_ref[0:1, :] * lamv_ref[1:2, :], axis=-1, keepdims=True)
    t2 = jnp.sum(lamv_ref[2:3, :] * lamv_ref[3:4, :], axis=-1, keepdims=True)
    return jnp.exp(t1) - jnp.exp(t2) + lam_init


def _sink_rows(sink_ref, pair, n_q):
    row = lax.broadcasted_iota(jnp.int32, (2 * n_q, 1), 0)
    return jnp.where(row < n_q, sink_ref[2 * pair], sink_ref[2 * pair + 1]) * LOG2E


def _diff_combine(o, n_q, lam):
    r = o[:, :LANES] / o[:, LANES:]
    return r[:n_q] - lam * r[n_q:]


def _gqa_output(groups):
    return jnp.concatenate(_swap_middle_heads(*groups), axis=1).astype(BF16)


def _bias_kernel(rel_ref, o_ref):
    h = pl.program_id(0)
    n_dr, n_dc = 2 * NA_KR - 1, 2 * NA_KC - 1
    q_col = lax.broadcasted_iota(jnp.int32, (GRID_W, LANES), 0)
    lane = lax.broadcasted_iota(jnp.int32, (GRID_W, LANES), 1)
    k_col = lane & (GRID_W - 1)
    dc = k_col - q_col + (NA_KC - 1)
    c_start = jnp.clip(q_col - NA_KC // 2, 0, GRID_W - NA_KC)
    in_window = (k_col >= c_start) & (k_col < c_start + NA_KC)
    tiles = []
    for dr in range(n_dr):
        t = jnp.zeros((GRID_W, LANES), F32)
        for d in range(n_dc):
            t = jnp.where(dc == d, rel_ref[(h * n_dr + dr) * n_dc + d], t)
        tiles.append(jnp.where(in_window, t * LOG2E, NEG_INF))
    low = lane < GRID_W
    for off in range(NA_KR):
        for c in range(NA_KR * GRID_W // LANES):
            dr = 2 * c - off + NA_KR - 1
            o_ref[0, off, :, c * LANES:(c + 1) * LANES] = jnp.where(low, tiles[dr], tiles[dr + 1])


def _attn_a_kernel(seq, sink_ref, q_ref, k_ref, v_ref, kc_ref, vc_ref, o_ref,
                   vl_ext, vc_ext, *bufs):
    n_q = BLOCK
    span = BLOCK + 2 * WINDOW
    n_blocks = seq // n_q
    masks = _lane_masks(2, BF16)
    s_refs, p_refs, m_refs = (_by_parity(bufs[i:i + 4]) for i in (0, 4, 8))
    _fill_ext(vl_ext, v_ref[...])
    _fill_ext(vc_ext, vc_ref[...])

    def window(n):
        start = min(max(n * n_q - WINDOW, 0), seq - span)
        return n * n_q, start

    for t in range(n_blocks + 2):
        new, old = t % 2, 1 - t % 2
        if 0 <= t - 2:
            q0, start = window(t - 2)
            v_loc = vl_ext[start:start + span, :]
            groups = []
            for pair in range(2):
                o = _weighted_values(p_refs[new][pair], [v_loc, vc_ext[...]])
                den = o[:, LANES:] + jnp.exp2(_sink_rows(sink_ref, pair, n_q)
                                              - m_refs[new][pair][...])
                groups.append(_pair_select(o[:, :LANES] / den, n_q))
            o_ref[q0:q0 + n_q, :] = _gqa_output(groups)
        if 0 <= t - 1 < n_blocks:
            for pair in range(2):
                m = _numerators(s_refs[old][pair], p_refs[old][pair],
                                floor=_sink_rows(sink_ref, pair, n_q))
                m_refs[old][pair][...] = jnp.broadcast_to(m, (2 * n_q, LANES))
        if t < n_blocks:
            q0, start = window(t)
            q = q_ref[q0:q0 + n_q, :]
            k_loc = k_ref[start:start + span, :]
            q_pos = q0 + (lax.broadcasted_iota(jnp.int32, (2 * n_q, 1), 0) & (n_q - 1))
            k_pos = start + lax.broadcasted_iota(jnp.int32, (1, span), 1)
            valid = jnp.abs(k_pos - q_pos) <= WINDOW
            for pair in range(2):
                lhs = _stack_masked(q[:, pair * LANES:(pair + 1) * LANES], masks)
                _scores(lhs, [k_loc, kc_ref[...]], s_refs[new][pair],
                        score_fn=lambda s, valid=valid: jnp.where(valid, s, NEG_INF))


EXT_ROWS = 16


def _fill_ext_t(ext_ref, v):
    row = lax.broadcasted_iota(jnp.int32, (LANES, LANES), 0)
    col = lax.broadcasted_iota(jnp.int32, (LANES, LANES), 1)
    eye = jnp.where(row == col, 1.0, 0.0).astype(BF16)
    ext_ref[:LANES, :] = _dot_nt(eye, v).astype(BF16)
    ext_ref[LANES:, :] = jnp.ones((EXT_ROWS, v.shape[0]), BF16)


def _value_chunks_t(lat_ref, ctx_ref):
    n_lat = lat_ref.shape[1]
    return [lat_ref[:, r:r + KEY_CHUNK] for r in range(0, n_lat, KEY_CHUNK)] + [ctx_ref[...]]


def _scores_stage_t(lhs, k_chunks, s_ref, m_ref):
    running = None
    off = 0
    for k in k_chunks:
        s = _dot_nt(k, lhs)
        n = k.shape[0]
        s_ref[off:off + n, :] = s
        top = jnp.max(s, axis=0, keepdims=True)
        running = top if running is None else jnp.maximum(running, top)
        off += n
        yield
    m_ref[...] = jnp.broadcast_to(running, m_ref.shape)


def _values_stage_t(s_ref, m_ref, vt_chunks, result):
    m = m_ref[0:1, :]
    acc = None
    off = 0
    for vt in vt_chunks:
        n = vt.shape[1]
        p = jnp.exp2(s_ref[off:off + n, :] - m).astype(BF16)
        part = _dot(vt, p)
        acc = part if acc is None else acc + part
        off += n
        yield
    result.append(acc)


def _interleave(*stages):
    stages = list(stages)
    while stages:
        stages = [stage for stage in stages if next(stage, stages) is not stages]


def _dense_slots(n_q, qp_ref, qn_ref, kp_ref, kn_ref, kcp_ref, kcn_ref):
    return ((0, qp_ref, slice(n_q, 2 * n_q), kp_ref, kcp_ref),
            (1, qn_ref, slice(0, n_q), kn_ref, kcn_ref))


def _refill_ext_per_sample(blocks_per_seq, fill):
    pl.when(lax.rem(jnp.maximum(pl.program_id(0) - 1, 0), blocks_per_seq) == 0)(fill)


def _attn_b_kernel(blocks_per_seq, qp_ref, qn_ref, kp_ref, kn_ref, kcp_ref, kcn_ref, v_ref,
                   vc_ref, o_ref, vl_ext, vc_ext, *bufs):
    n_q = o_ref.shape[0] // 2
    s_refs, m_refs = _by_parity(bufs[0:4]), _by_parity(bufs[4:8])
    _zero_at_first_step(bufs)

    def fill():
        _fill_ext_t(vl_ext, v_ref[...])
        _fill_ext_t(vc_ext, vc_ref[...])

    _refill_ext_per_sample(blocks_per_seq, fill)
    masks = _lane_masks(2, BF16)
    for parity, q_ref, q_rows, k_ref, kc_ref in _dense_slots(n_q, qp_ref, qn_ref, kp_ref, kn_ref,
                                                            kcp_ref, kcn_ref):
        done, todo = parity, 1 - parity
        q = q_ref[q_rows, :]
        results, stages = [], []
        for pair in range(2):
            results.append([])
            stages.append(_values_stage_t(s_refs[done][pair], m_refs[done][pair],
                                          _value_chunks_t(vl_ext, vc_ext), results[pair]))
            lhs = _stack_masked(q[:, pair * LANES:(pair + 1) * LANES], masks)
            stages.append(_scores_stage_t(lhs, _key_chunks(k_ref, kc_ref),
                                          s_refs[todo][pair], m_refs[todo][pair]))
        _interleave(*stages)
        groups = []
        for (o_t,) in results:
            inv = 1.0 / o_t[LANES:LANES + 1, :]
            lo_t = o_t[:HEAD_DIM, :n_q] * inv[:, :n_q]
            hi_t = o_t[HEAD_DIM:LANES, n_q:] * inv[:, n_q:]
            groups.append(jnp.concatenate([lo_t, hi_t], axis=0).T)
        o_ref[parity * n_q:(parity + 1) * n_q, :] = _gqa_output(groups)


def _attn_d_kernel(lam_init, blocks_per_seq, lamv_ref, gain_ref, qp_ref, qn_ref, kp_ref, kn_ref,
                   kcp_ref, kcn_ref, v_ref, vc_ref, o_ref, vl_ext, vc_ext, *bufs):
    n_q = o_ref.shape[0] // 2
    s_refs, m_refs = _by_parity(bufs[0:8]), _by_parity(bufs[8:16])
    _zero_at_first_step(bufs)

    def fill():
        for jb in range(2):
            cols = slice(jb * LANES, (jb + 1) * LANES)
            _fill_ext_t(vl_ext.at[jb], v_ref[:, cols])
            _fill_ext_t(vc_ext.at[jb], vc_ref[:, cols])

    _refill_ext_per_sample(blocks_per_seq, fill)
    lam = _lam_value(lamv_ref, lam_init)
    masks = _lane_masks(4, BF16)
    for parity, q_ref, q_rows, k_ref, kc_ref in _dense_slots(n_q, qp_ref, qn_ref, kp_ref, kn_ref,
                                                            kcp_ref, kcn_ref):
        done, todo = parity, 1 - parity
        q = q_ref[q_rows, :]
        results, stages = [], []
        for jb in range(2):
            cols = slice(jb * LANES, (jb + 1) * LANES)
            for head in range(2):
                g = 2 * jb + head
                results.append([])
                stages.append(_values_stage_t(s_refs[done][g], m_refs[done][g],
                                              _value_chunks_t(vl_ext.at[jb], vc_ext.at[jb]),
                                              results[g]))
                lhs = _stack_masked(q[:, cols], masks[2 * head:2 * head + 2])
                stages.append(_scores_stage_t(lhs, _key_chunks(k_ref, kc_ref, cols),
                                              s_refs[todo][g], m_refs[todo][g]))
        _interleave(*stages)
        outs = []
        for jb in range(2):
            ys_t = []
            for head in range(2):
                (o_t,) = results[2 * jb + head]
                dims = slice(head * HEAD_DIM, (head + 1) * HEAD_DIM)
                r = o_t[dims, :] * (1.0 / o_t[LANES:LANES + 1, :])
                ys_t.append(r[:, :n_q] - lam * r[:, n_q:])
            y = jnp.concatenate(ys_t, axis=0).T
            outs.append(_sub_ln(y, gain_ref[...], 1.0 - lam_init))
        o_ref[parity * n_q:(parity + 1) * n_q, :] = jnp.concatenate(outs, axis=1).astype(BF16)


def _attn_c_kernel(n_grid_rows, q_ref, k_ref, v_ref, kc_ref, vc_ref, bias_ref,
                   o_ref, vl_ext, vc_ext, *bufs):
    n_q = GRID_W
    n_loc = NA_KR * GRID_W
    masks = _lane_masks(2, BF16)
    s_refs, p_refs = _by_parity(bufs[0:4]), _by_parity(bufs[4:8])
    for jb in range(2):
        cols = slice(jb * LANES, (jb + 1) * LANES)
        _fill_ext(vl_ext.at[jb], v_ref[:, cols])
        _fill_ext(vc_ext.at[jb], vc_ref[:, cols])

    def geometry(r):
        r_start = min(max(r - NA_KR // 2, 0), n_grid_rows - NA_KR)
        return slice(r * n_q, (r + 1) * n_q), slice(r_start * GRID_W, r_start * GRID_W + n_loc), \
            r - r_start

    for t in range(n_grid_rows + 2):
        new, old = t % 2, 1 - t % 2
        if 0 <= t - 2:
            q_rows, k_rows, _ = geometry(t - 2)
            outs = []
            for jb in range(2):
                o = _weighted_values(p_refs[new][jb], [vl_ext[jb, k_rows, :], vc_ext[jb]])
                outs.append(_pair_select(o[:, :LANES] / o[:, LANES:], n_q))
            o_ref[q_rows, :] = jnp.concatenate(outs, axis=1).astype(BF16)
        if 0 <= t - 1 < n_grid_rows:
            for jb in range(2):
                _numerators(s_refs[old][jb], p_refs[old][jb])
        if t < n_grid_rows:
            q_rows, k_rows, off = geometry(t)
            q = q_ref[q_rows, :]
            for jb in range(2):
                cols = slice(jb * LANES, (jb + 1) * LANES)
                lhs = _stack_masked(q[:, cols], masks)
                bias = jnp.concatenate([bias_ref[2 * jb, off], bias_ref[2 * jb + 1, off]], axis=0)
                _scores(lhs, [k_ref[k_rows, cols], kc_ref[:, cols]], s_refs[new][jb],
                        score_fn=lambda s: s + bias)


def _attn_ctx_kernel(lam_init, sink_ref, lamv_ref, gain_ref, aq_ref, ak_ref, av_ref, bq_ref, bk_ref,
                     bv_ref, cq_ref, ck_ref, cv_ref, dq_ref, dk_ref, dv_ref, o_ref,
                     s0, s1, m0, m1):
    n_q = aq_ref.shape[0]
    m2 = _lane_masks(2, BF16)
    m4 = _lane_masks(4, BF16)
    lane = lax.broadcasted_iota(jnp.int32, (1, LANES), 1)
    sets = ((s0, m0), (s1, m1))

    def ext(v):
        return jnp.concatenate([v, jnp.ones(v.shape, v.dtype)], axis=1)

    def attend(lhs, k, v_ext, s_ref, m_ref, floor=None):
        _scores_stage(lhs, [k], s_ref, m_ref, floor=floor)
        return _values_stage(s_ref, m_ref, [v_ext])

    def out_cols(group, pair):
        c0 = group * GROUP_WIDTH + pair * LANES
        return slice(c0, c0 + LANES)

    for group, (q_ref_, k_ref_, v_ref_) in enumerate(((aq_ref, ak_ref, av_ref),
                                                      (bq_ref, bk_ref, bv_ref))):
        q = q_ref_[...]
        v_ext = ext(v_ref_[...])
        groups = []
        for pair, (s_ref, m_ref) in enumerate(sets):
            lhs = _stack_masked(q[:, pair * LANES:(pair + 1) * LANES], m2)
            sink = _sink_rows(sink_ref, pair, n_q) if group == 0 else None
            o = attend(lhs, k_ref_[...], v_ext, s_ref, m_ref, floor=sink)
            den = o[:, LANES:]
            if group == 0:
                den = den + jnp.exp2(sink - m_ref[...])
            groups.append(_pair_select(o[:, :LANES] / den, n_q))
        o_ref[:, group * GROUP_WIDTH:(group + 1) * GROUP_WIDTH] = _gqa_output(groups)

    q = cq_ref[...]
    for jb, (s_ref, m_ref) in enumerate(sets):
        cols = slice(jb * LANES, (jb + 1) * LANES)
        o = attend(_stack_masked(q[:, cols], m2), ck_ref[:, cols], ext(cv_ref[:, cols]),
                   s_ref, m_ref)
        o_ref[:, out_cols(2, jb)] = _pair_select(o[:, :LANES] / o[:, LANES:], n_q).astype(BF16)

    lam = _lam_value(lamv_ref, lam_init)
    q = dq_ref[...]
    for jb in range(2):
        cols = slice(jb * LANES, (jb + 1) * LANES)
        v_ext = ext(dv_ref[:, cols])
        ys = []
        for head, (s_ref, m_ref) in enumerate(sets):
            lhs = _stack_masked(q[:, cols], m4[2 * head:2 * head + 2])
            ys.append(_diff_combine(attend(lhs, dk_ref[:, cols], v_ext, s_ref, m_ref), n_q, lam))
        y = jnp.where(lane < HEAD_DIM, ys[0], ys[1])
        o_ref[:, out_cols(3, jb)] = _sub_ln(y, gain_ref[...], 1.0 - lam_init).astype(BF16)


def _rope_tables(seq, n_extra):
    t = np.arange(seq)
    row = (t // GRID_W).astype(np.float64)
    col = (t % GRID_W).astype(np.float64)
    tables = []
    for dim in (HEAD_DIM, D_SUB):
        half = dim // 2
        freqs = ROPE_BASE ** (-np.arange(0, half, 2, dtype=np.float64) / half)
        ang_r = row[:, None] * freqs[None, :]
        ang_c = col[:, None] * freqs[None, :]
        ang = np.concatenate([ang_r, ang_r, ang_c, ang_c], axis=-1)
        quarter = dim // 4
        sign = np.where((np.arange(dim) % (2 * quarter)) < quarter, -1.0, 1.0)
        cos = np.tile(np.cos(ang), (1, LANES // dim))
        sin = np.tile(np.sin(ang) * sign[None, :], (1, LANES // dim))
        cos = np.concatenate([cos, np.ones((n_extra, LANES))], axis=0)
        sin = np.concatenate([sin, np.zeros((n_extra, LANES))], axis=0)
        tables += [cos, sin]
    return jnp.asarray(np.stack(tables), dtype=F32)


def _neighbourhood_bias(rel_bias):
    heads = rel_bias.shape[0]
    return pl.pallas_call(
        _bias_kernel,
        grid=(heads,),
        in_specs=[pl.BlockSpec(memory_space=pltpu.SMEM)],
        out_specs=pl.BlockSpec((1, NA_KR, GRID_W, NA_KR * GRID_W), lambda h: (h, 0, 0, 0)),
        out_shape=jax.ShapeDtypeStruct((heads, NA_KR, GRID_W, NA_KR * GRID_W), F32),
        compiler_params=_params(("arbitrary",)),
        name="neighbourhood_bias",
    )(rel_bias.astype(F32).reshape(-1))


def _stage_scratch(n_rows, n_keys, n_groups=2):
    return ([pltpu.VMEM((n_rows, n_keys), F32)] * (2 * n_groups)
            + [pltpu.VMEM((n_rows, LANES), F32)] * (2 * n_groups))


def _stage_scratch_t(n_rows, n_keys, n_groups=2):
    return ([pltpu.VMEM((n_keys, n_rows), F32)] * (2 * n_groups)
            + [pltpu.VMEM((8, n_rows), F32)] * (2 * n_groups))


def _loop_scratch(n_rows, n_keys, with_maxima=False):
    maxima = [pltpu.VMEM((n_rows, LANES), F32)] * 4 if with_maxima else []
    return ([pltpu.VMEM((n_rows, n_keys), F32)] * 4 + [pltpu.VMEM((n_rows, n_keys), BF16)] * 4
            + maxima)


def kernel(x, c, ctx, c_ctx, w_ada, b_ada, w_ffn1_gate, w_ffn1_up, w_ffn1_down, w_in, w_out,
           sink_logit, q_norm_g, k_norm_g, rel_pos_bias, lam_q1, lam_k1, lam_q2, lam_k2, subln_g,
           w_ffn2_gate, w_ffn2_up, w_ffn2_down, final_norm_g):
    batch, seq, d = x.shape
    n_ctx = ctx.shape[1]
    depth = w_ada.shape[0]
    assert d == D_MODEL and seq % TOKEN_TILE == 0 and (batch * n_ctx) % TOKEN_TILE == 0
    assert seq % GRID_W == 0 and w_in.shape[-1] == IN_WIDTH and seq % KEY_CHUNK == 0
    assert seq % (2 * GQA_TILE) == 0 and seq % (2 * DIFF_TILE) == 0
    assert seq % (2 * BLOCK) == 0 and seq % (2 * GRID_W) == 0
    t_lat, t_ctx = batch * seq, batch * n_ctx
    t_all = t_lat + t_ctx
    tm = TOKEN_TILE
    n_lat_tiles, n_ctx_tiles = t_lat // tm, t_ctx // tm
    tiles_per_seq = seq // tm
    grid_rows = seq // GRID_W
    assert grid_rows >= NA_KR and rel_pos_bias.shape[1:] == (GROUP_HEADS, 2 * NA_KR - 1, 2 * NA_KC - 1)

    cc = jnp.concatenate([c, c_ctx[None, :], jnp.zeros((16 - batch - 1, d), F32)], axis=0)
    mods = _mods(cc, w_ada, b_ada).reshape(depth, 16, N_MOD, d)

    def group(i):
        return jnp.minimum(i // tiles_per_seq, batch)

    rope = _rope_tables(seq, tm)

    def rope_block(i):
        return jnp.where(i < n_lat_tiles, i % tiles_per_seq, tiles_per_seq)

    lane_gain = lambda g: jnp.tile(g.astype(F32), LANES // HEAD_DIM)[None, :]

    tile_spec = pl.BlockSpec((tm, d), lambda i: (i, 0))

    ffn1 = tuple(_to_bf16(w) for w in (w_ffn1_gate, w_ffn1_up, w_ffn1_down))
    ffn2 = tuple(_to_bf16(w) for w in (w_ffn2_gate, w_ffn2_up, w_ffn2_down))
    w_in_b = _to_bf16(w_in, swap_cols=(COL_AQ * LANES, COL_BQ * LANES))
    w_out_b = _to_bf16(w_out)

    for l in range(depth):
        last = l == depth - 1
        lam_init = 0.8 - 0.6 * math.exp(-0.3 * l)
        mod_spec = pl.BlockSpec((1, N_MOD, d), lambda i: (group(i), 0, 0))

        gains = jnp.concatenate([lane_gain(q_norm_g[l]), lane_gain(k_norm_g[l]),
                                 jnp.zeros((6, LANES), F32)], axis=0)
        if l == 0:
            streams = (x.reshape(t_lat, d), ctx.reshape(t_ctx, d))
            stream_specs = [
                pl.BlockSpec((tm, d), lambda i: (jnp.minimum(i, n_lat_tiles - 1), 0)),
                pl.BlockSpec((tm, d), lambda i: (jnp.maximum(i - n_lat_tiles, 0), 0))]
        else:
            streams, stream_specs = (xs,), [tile_spec]
        xs, qkv = pl.pallas_call(
            functools.partial(_pre_kernel, n_lat_tiles if l == 0 else None),
            grid=(n_lat_tiles + n_ctx_tiles,),
            in_specs=stream_specs + [
                mod_spec,
                _resident((d, D_FF), l), _resident((d, D_FF), l), _resident((D_FF, d), l),
                _resident((d, IN_WIDTH), l),
                pl.BlockSpec((4, tm, LANES), lambda i: (0, rope_block(i), 0)),
                _resident((8, LANES)),
            ],
            out_specs=[tile_spec, pl.BlockSpec((tm, IN_WIDTH), lambda i: (i, 0))],
            out_shape=[jax.ShapeDtypeStruct((t_all, d), F32),
                       jax.ShapeDtypeStruct((t_all, IN_WIDTH), BF16)],
            scratch_shapes=[pltpu.VMEM((tm, D_FF), BF16)],
            compiler_params=_params(("arbitrary",)),
            name=f"pre_l{l}",
        )(*streams, mods[l], *ffn1, w_in_b, rope, gains)

        ctx_row = t_lat // n_ctx
        sink_perm = sink_logit[l].astype(F32)[jnp.array(GQA_HEAD_ORDER)]
        lamv = jnp.pad(jnp.stack([lam_q1[l], lam_k1[l], lam_q2[l], lam_k2[l]]).astype(F32),
                       ((0, 4), (0, LANES - D_SUB)))
        subln = lane_gain(subln_g[l])
        smem = pl.BlockSpec(memory_space=pltpu.SMEM)

        def lat(cols, col):
            return pl.BlockSpec((seq, cols), lambda b, *_: (b, col))

        def ctxb(cols, col):
            return pl.BlockSpec((n_ctx, cols), lambda b, *_: (ctx_row + b, col))

        def v_ext_scratch(n_blocks):
            lead = () if n_blocks == 1 else (n_blocks,)
            return [pltpu.VMEM(lead + (seq, 2 * LANES), BF16),
                    pltpu.VMEM(lead + (n_ctx, 2 * LANES), BF16)]

        y_a = pl.pallas_call(
            functools.partial(_attn_a_kernel, seq),
            grid=(batch,),
            in_specs=[smem, lat(2 * LANES, COL_AQ // 2), lat(LANES, COL_AK), lat(LANES, COL_AV),
                      ctxb(LANES, COL_AK), ctxb(LANES, COL_AV)],
            out_specs=pl.BlockSpec((seq, GROUP_WIDTH), lambda b: (b, 0)),
            out_shape=jax.ShapeDtypeStruct((t_lat, GROUP_WIDTH), BF16),
            scratch_shapes=(v_ext_scratch(1)
                            + _loop_scratch(2 * BLOCK, BLOCK + 2 * WINDOW + n_ctx,
                                            with_maxima=True)),
            compiler_params=_params(("arbitrary",)),
            name=f"attn_a_l{l}",
        )(sink_perm, qkv, qkv, qkv, qkv, qkv)

        def v_ext_t_scratch(n_blocks):
            lead = () if n_blocks == 1 else (n_blocks,)
            return [pltpu.VMEM(lead + (LANES + EXT_ROWS, seq), BF16),
                    pltpu.VMEM(lead + (LANES + EXT_ROWS, n_ctx), BF16)]

        def dense_mixer(body, name, tq, n_groups, q_col, k_cols, k_col, v_col, params=(),
                        ext_blocks=0):
            blocks_per_seq = seq // (2 * tq)
            n_blocks = batch * blocks_per_seq

            def cur(j):
                return jnp.minimum(j, n_blocks - 1)

            def prev(j):
                return jnp.maximum(j - 1, 0)

            def sample_of(block):
                return lambda j: block(j) // blocks_per_seq

            specs = [pl.BlockSpec((2 * tq, 2 * LANES), lambda j: (prev(j), q_col)),
                     pl.BlockSpec((2 * tq, 2 * LANES), lambda j: (cur(j), q_col))]
            for rows, row0 in ((seq, 0), (n_ctx, ctx_row)):
                for block in (prev, cur):
                    specs.append(pl.BlockSpec(
                        (rows, k_cols), lambda j, b=sample_of(block), r=row0: (r + b(j), k_col)))
            for rows, row0 in ((seq, 0), (n_ctx, ctx_row)):
                specs.append(pl.BlockSpec(
                    (rows, k_cols), lambda j, b=sample_of(prev), r=row0: (r + b(j), v_col)))
            return pl.pallas_call(
                functools.partial(body, blocks_per_seq),
                grid=(n_blocks + 1,),
                in_specs=[_resident(p.shape) for p in params] + specs,
                out_specs=pl.BlockSpec((2 * tq, GROUP_WIDTH), lambda j: (prev(j), 0)),
                out_shape=jax.ShapeDtypeStruct((t_lat, GROUP_WIDTH), BF16),
                scratch_shapes=(v_ext_t_scratch(ext_blocks)
                                + _stage_scratch_t(2 * tq, seq + n_ctx, n_groups=n_groups)),
                compiler_params=_params(("arbitrary",)),
                name=name,
            )(*params, *([qkv] * 8))

        y_b = dense_mixer(_attn_b_kernel, f"attn_b_l{l}", GQA_TILE, 2,
                          COL_BQ // 2, LANES, COL_BK, COL_BV, ext_blocks=1)
        y_d = dense_mixer(functools.partial(_attn_d_kernel, lam_init), f"attn_d_l{l}", DIFF_TILE, 4,
                          COL_DQ // 2, 2 * LANES, COL_DK // 2, COL_DV // 2, params=(lamv, subln),
                          ext_blocks=2)

        bias = _neighbourhood_bias(rel_pos_bias[l])
        y_c = pl.pallas_call(
            functools.partial(_attn_c_kernel, grid_rows),
            grid=(batch,),
            in_specs=[lat(2 * LANES, COL_CQ // 2),
                      lat(2 * LANES, COL_CK // 2), lat(2 * LANES, COL_CV // 2),
                      ctxb(2 * LANES, COL_CK // 2), ctxb(2 * LANES, COL_CV // 2),
                      _resident(bias.shape)],
            out_specs=pl.BlockSpec((seq, GROUP_WIDTH), lambda b: (b, 0)),
            out_shape=jax.ShapeDtypeStruct((t_lat, GROUP_WIDTH), BF16),
            scratch_shapes=(v_ext_scratch(2)
                            + _loop_scratch(2 * GRID_W, NA_KR * GRID_W + n_ctx)),
            compiler_params=_params(("arbitrary",)),
            name=f"attn_c_l{l}",
        )(qkv, qkv, qkv, qkv, qkv, bias)

        ffn2_specs = [_resident((d, D_FF), l), _resident((d, D_FF), l), _resident((D_FF, d), l)]
        fng = final_norm_g.astype(F32)[None, :]
        y_spec = pl.BlockSpec((tm, GROUP_WIDTH), lambda i: (jnp.minimum(i, n_lat_tiles - 1), 0))
        post_scratch = [pltpu.VMEM((tm, d), BF16), pltpu.VMEM((tm, D_FF), BF16)]

        if not last:
            y_ctx = pl.pallas_call(
                functools.partial(_attn_ctx_kernel, lam_init),
                grid=(batch,),
                in_specs=[smem, _resident((8, LANES)), _resident((1, LANES)),
                          ctxb(2 * LANES, COL_AQ // 2), ctxb(LANES, COL_AK), ctxb(LANES, COL_AV),
                          ctxb(2 * LANES, COL_BQ // 2), ctxb(LANES, COL_BK), ctxb(LANES, COL_BV),
                          ctxb(2 * LANES, COL_CQ // 2), ctxb(2 * LANES, COL_CK // 2),
                          ctxb(2 * LANES, COL_CV // 2),
                          ctxb(2 * LANES, COL_DQ // 2), ctxb(2 * LANES, COL_DK // 2),
                          ctxb(2 * LANES, COL_DV // 2)],
                out_specs=pl.BlockSpec((n_ctx, d), lambda b: (b, 0)),
                out_shape=jax.ShapeDtypeStruct((t_ctx, d), BF16),
                scratch_shapes=_stage_scratch(2 * n_ctx, n_ctx, n_groups=1),
                compiler_params=_params(("arbitrary",)),
                name=f"attn_ctx_l{l}",
            )(sink_perm, lamv, subln, *([qkv] * 12))

            xs = pl.pallas_call(
                functools.partial(_post_kernel, n_lat_tiles, False),
                grid=(n_lat_tiles + n_ctx_tiles,),
                in_specs=[tile_spec, mod_spec, y_spec, y_spec, y_spec, y_spec,
                          pl.BlockSpec((tm, d), lambda i: (jnp.maximum(i - n_lat_tiles, 0), 0)),
                          _resident((d, d), l)] + ffn2_specs + [_resident((1, d))],
                out_specs=tile_spec,
                out_shape=jax.ShapeDtypeStruct((t_all, d), F32),
                scratch_shapes=post_scratch,
                compiler_params=_params(("arbitrary",)),
                name=f"post_l{l}",
            )(xs, mods[l], y_a, y_b, y_c, y_d, y_ctx, w_out_b, *ffn2, fng)
        else:
            xs = pl.pallas_call(
                functools.partial(_post_kernel, None, True),
                grid=(n_lat_tiles,),
                in_specs=[tile_spec, mod_spec, y_spec, y_spec, y_spec, y_spec,
                          _resident((d, d), l)] + ffn2_specs + [_resident((1, d))],
                out_specs=tile_spec,
                out_shape=jax.ShapeDtypeStruct((t_lat, d), F32),
                scratch_shapes=post_scratch,
                compiler_params=_params(("arbitrary",)),
                name=f"post_l{l}",
            )(xs, mods[l], y_a, y_b, y_c, y_d, w_out_b, *ffn2, fng)

    return xs.reshape(batch, seq, d)
```

```python
import functools
import math

import numpy as np
import jax
import jax.numpy as jnp
from jax import lax
from jax.experimental import pallas as pl
from jax.experimental.pallas import tpu as pltpu

D_MODEL = 1024
GRID_W = 64
HEAD_DIM = 64
N_GROUPS = 4
GROUP_HEADS = D_MODEL // (N_GROUPS * HEAD_DIM)
GROUP_WIDTH = GROUP_HEADS * HEAD_DIM
D_SUB = HEAD_DIM // 2
WINDOW = 128
BLOCK = 128
NA_KR = 8
NA_KC = 16
D_FF = 2816
ROPE_BASE = 10000.0
NORM_EPS = 1e-6
N_MOD = 9
NEG_INF = -1e30
IN_WIDTH = 2560

LANES = 128
MXU_N = 256
TOKEN_TILE = 512
GQA_TILE = 256
DIFF_TILE = 256
KEY_CHUNK = 512
VMEM_LIMIT = 56 * 1024 * 1024
LOG2E = 1.4426950408889634

BF16 = jnp.bfloat16
F32 = jnp.float32

COL_AQ, COL_AK, COL_AV = 0, 2, 3
COL_BQ, COL_BK, COL_BV = 4, 6, 7
COL_CQ, COL_CK, COL_CV = 8, 10, 12
COL_DQ, COL_DK, COL_DV = 14, 16, 18

GQA_HEAD_ORDER = (0, 2, 1, 3)


def _dot(a, b):
    return jnp.dot(a, b, preferred_element_type=F32)


def _dot_nt(a, b):
    return lax.dot_general(a, b, (((1,), (1,)), ((), ())), preferred_element_type=F32)


def _params(semantics):
    return pltpu.CompilerParams(dimension_semantics=semantics, vmem_limit_bytes=VMEM_LIMIT)


def _resident(shape, layer=None):
    nd = len(shape)
    if layer is None:
        return pl.BlockSpec(shape, lambda *_: (0,) * nd, pipeline_mode=pl.Buffered(1))
    return pl.BlockSpec((None,) + tuple(shape), lambda *_: (layer,) + (0,) * nd,
                        pipeline_mode=pl.Buffered(1))


def _swap_middle_heads(lo, hi):
    lane = lax.broadcasted_iota(jnp.int32, (1, LANES), 1)
    low = lane < HEAD_DIM
    return (jnp.where(low, lo, pltpu.roll(hi, HEAD_DIM, 1)),
            jnp.where(low, pltpu.roll(lo, HEAD_DIM, 1), hi))


CAST_BLOCK_BYTES = 6 * 1024 * 1024


def _cast_kernel(swap_cols, w_ref, o_ref):
    o_ref[...] = w_ref[...].astype(BF16)
    for c0 in swap_cols:
        lo, hi = _swap_middle_heads(w_ref[:, c0:c0 + LANES], w_ref[:, c0 + LANES:c0 + 2 * LANES])
        o_ref[:, c0:c0 + LANES] = lo.astype(BF16)
        o_ref[:, c0 + LANES:c0 + 2 * LANES] = hi.astype(BF16)


def _to_bf16(w, swap_cols=()):
    depth, r, c = w.shape
    tr = max(t for t in range(16, r + 1, 16) if r % t == 0 and t * c * 4 <= CAST_BLOCK_BYTES)
    spec = pl.BlockSpec((None, tr, c), lambda l, i: (l, i, 0))
    return pl.pallas_call(
        functools.partial(_cast_kernel, tuple(swap_cols)),
        grid=(depth, r // tr),
        in_specs=[spec],
        out_specs=spec,
        out_shape=jax.ShapeDtypeStruct(w.shape, BF16),
        compiler_params=_params(("arbitrary", "arbitrary")),
        name="cast_bf16",
    )(w)


def _mods_kernel(c_ref, w_ref, b_ref, o_ref):
    c = c_ref[...]
    a = c * (1.0 / (1.0 + jnp.exp(-c)))
    o_ref[...] = _dot(a.astype(BF16), w_ref[...].astype(BF16)) + b_ref[...]


def _mods(cc, w_ada, b_ada):
    depth, d, n = w_ada.shape
    rows = cc.shape[0]
    tn = 1152
    return pl.pallas_call(
        _mods_kernel,
        grid=(depth, n // tn),
        in_specs=[
            pl.BlockSpec((rows, d), lambda l, j: (0, 0)),
            pl.BlockSpec((None, d, tn), lambda l, j: (l, 0, j)),
            pl.BlockSpec((None, 1, tn), lambda l, j: (l, 0, j)),
        ],
        out_specs=pl.BlockSpec((None, rows, tn), lambda l, j: (l, 0, j)),
        out_shape=jax.ShapeDtypeStruct((depth, rows, n), F32),
        compiler_params=_params(("arbitrary", "arbitrary")),
        name="adaln_mods",
    )(cc, w_ada, b_ada.reshape(depth, 1, n))


def _modulated(x, mod_ref, k):
    shift = mod_ref[0, k:k + 1, :]
    scale = mod_ref[0, k + 1:k + 2, :]
    ms = jnp.mean(x * x, axis=-1, keepdims=True)
    return (x * lax.rsqrt(ms + NORM_EPS)) * (1.0 + scale) + shift


def _swiglu(h, wg_ref, wu_ref, wd_ref, act_ref):
    hb = h.astype(BF16)
    for j in range(D_FF // MXU_N):
        cols = slice(j * MXU_N, (j + 1) * MXU_N)
        g = _dot(hb, wg_ref[:, cols])
        u = _dot(hb, wu_ref[:, cols])
        act_ref[:, cols] = ((g * (1.0 / (1.0 + jnp.exp(-g)))) * u).astype(BF16)
    return _dot(act_ref[...], wd_ref[...])


def _rope(v, cos, sin_signed, quarter):
    lane = lax.broadcasted_iota(jnp.int32, (1, LANES), 1)
    first = (lane & (2 * quarter - 1)) < quarter
    rot = jnp.where(first, pltpu.roll(v, LANES - quarter, 1), pltpu.roll(v, quarter, 1))
    return v * cos + rot * sin_signed


def _head_rms(v, gain):
    lane = lax.broadcasted_iota(jnp.int32, (1, LANES), 1)
    lo = lane < HEAD_DIM
    sq = v * v
    ms_lo = jnp.sum(jnp.where(lo, sq, 0.0), axis=-1, keepdims=True) * (1.0 / HEAD_DIM)
    ms_hi = jnp.sum(jnp.where(lo, 0.0, sq), axis=-1, keepdims=True) * (1.0 / HEAD_DIM)
    rs = jnp.where(lo, lax.rsqrt(ms_lo + NORM_EPS), lax.rsqrt(ms_hi + NORM_EPS))
    return v * rs * gain


_QK_SCALE = HEAD_DIM ** -0.5 * LOG2E
_SUB_SCALE = D_SUB ** -0.5 * LOG2E
_PROJ_BLOCKS = (
    (16, -1, _QK_SCALE), (16, -1, _QK_SCALE), (16, -1, 1.0), (0, -1, 1.0),
    (16, 0, _QK_SCALE), (16, 0, _QK_SCALE), (16, 1, 1.0), (0, -1, 1.0),
    (0, -1, _QK_SCALE), (0, -1, _QK_SCALE), (0, -1, 1.0), (0, -1, 1.0),
    (0, -1, 1.0), (0, -1, 1.0),
    (8, -1, _SUB_SCALE), (8, -1, _SUB_SCALE), (8, -1, 1.0), (8, -1, 1.0),
    (0, -1, 1.0), (0, -1, 1.0),
)


def _project(h, w_ref, rope_ref, gain_ref, o_ref):
    hb = h.astype(BF16)
    groups = sorted(range(IN_WIDTH // MXU_N),
                    key=lambda j: -sum((q > 0) + 2 * (g >= 0) for q, g, _ in _PROJ_BLOCKS[2 * j:2 * j + 2]))
    for j in groups:
        r = _dot(hb, w_ref[:, j * MXU_N:(j + 1) * MXU_N])
        for half in range(MXU_N // LANES):
            blk = j * (MXU_N // LANES) + half
            quarter, gain_idx, scale = _PROJ_BLOCKS[blk]
            v = r[:, half * LANES:(half + 1) * LANES]
            if gain_idx >= 0:
                v = _head_rms(v, gain_ref[gain_idx:gain_idx + 1, :])
            if quarter:
                t = 0 if quarter == 16 else 2
                v = _rope(v, rope_ref[t], rope_ref[t + 1], quarter)
            if scale != 1.0:
                v = v * scale
            o_ref[:, blk * LANES:(blk + 1) * LANES] = v.astype(BF16)


def _pre_kernel(n_lat_tiles, *refs):
    if n_lat_tiles is None:
        (x_ref, mod_ref, wg_ref, wu_ref, wd_ref, win_ref, rope_ref, gain_ref,
         xo_ref, qkv_ref, act_ref) = refs
        x = x_ref[...]
    else:
        (x_ref, c_ref, mod_ref, wg_ref, wu_ref, wd_ref, win_ref, rope_ref, gain_ref,
         xo_ref, qkv_ref, act_ref) = refs
        x = jnp.where(pl.program_id(0) < n_lat_tiles, x_ref[...], c_ref[...])
    y = _swiglu(_modulated(x, mod_ref, 0), wg_ref, wu_ref, wd_ref, act_ref)
    x = x + (0.5 * mod_ref[0, 2:3, :]) * y
    xo_ref[...] = x
    _project(_modulated(x, mod_ref, 3), win_ref, rope_ref, gain_ref, qkv_ref)


def _post_kernel(n_lat_tiles, final_norm, *refs):
    if n_lat_tiles is None:
        (x_ref, mod_ref, ya_ref, yb_ref, yc_ref, yd_ref, wout_ref, wg_ref, wu_ref, wd_ref,
         fng_ref, xo_ref, y_scr, act_ref) = refs
        y_scr[...] = jnp.concatenate([ya_ref[...], yb_ref[...], yc_ref[...], yd_ref[...]], axis=1)
    else:
        (x_ref, mod_ref, ya_ref, yb_ref, yc_ref, yd_ref, yctx_ref, wout_ref, wg_ref, wu_ref,
         wd_ref, fng_ref, xo_ref, y_scr, act_ref) = refs
        is_latent = pl.program_id(0) < n_lat_tiles

        @pl.when(is_latent)
        def _():
            y_scr[...] = jnp.concatenate(
                [ya_ref[...], yb_ref[...], yc_ref[...], yd_ref[...]], axis=1)

        @pl.when(jnp.logical_not(is_latent))
        def _():
            y_scr[...] = yctx_ref[...]

    x = x_ref[...]
    x = x + mod_ref[0, 5:6, :] * _dot(y_scr[...], wout_ref[...])
    y = _swiglu(_modulated(x, mod_ref, 6), wg_ref, wu_ref, wd_ref, act_ref)
    x = x + (0.5 * mod_ref[0, 8:9, :]) * y
    if final_norm:
        ms = jnp.mean(x * x, axis=-1, keepdims=True)
        x = (x * lax.rsqrt(ms + NORM_EPS)) * fng_ref[...]
    xo_ref[...] = x


def _lane_masks(n, dtype):
    lane = lax.broadcasted_iota(jnp.int32, (1, LANES), 1)
    w = LANES // n
    return [jnp.where((lane >= k * w) & (lane < (k + 1) * w), 1.0, 0.0).astype(dtype)
            for k in range(n)]


def _stack_masked(q, masks):
    return jnp.concatenate([q * m for m in masks], axis=0)


def _fill_ext(ext_ref, v):
    ext_ref[:, :LANES] = v
    ext_ref[:, LANES:] = jnp.ones(v.shape, v.dtype)


def _by_parity(refs):
    half = len(refs) // 2
    return (tuple(refs[:half]), tuple(refs[half:]))


def _zero_at_first_step(refs):
    @pl.when(pl.program_id(0) == 0)
    def _():
        for ref in refs:
            ref[...] = jnp.zeros(ref.shape, ref.dtype)


ROW_BLOCK = 32


def _scores(lhs, k_list, s_ref, score_fn=None):
    off = 0
    for idx, k in enumerate(k_list):
        s = _dot_nt(lhs, k)
        if idx == 0 and score_fn is not None:
            s = score_fn(s)
        s_ref[:, off:off + k.shape[0]] = s
        off += k.shape[0]


def _numerators(s_ref, p_ref, floor=None):
    maxima = []
    for r in range(0, s_ref.shape[0], ROW_BLOCK):
        rows = slice(r, r + ROW_BLOCK)
        m = jnp.max(s_ref[rows, :], axis=-1, keepdims=True)
        if floor is not None:
            m = jnp.maximum(m, floor[rows])
        p_ref[rows, :] = jnp.exp2(s_ref[rows, :] - m).astype(BF16)
        maxima.append(m)
    return jnp.concatenate(maxima, axis=0)


def _weighted_values(p_ref, v_list):
    o = None
    off = 0
    for v in v_list:
        part = _dot(p_ref[:, off:off + v.shape[0]], v)
        o = part if o is None else o + part
        off += v.shape[0]
    return o


def _key_chunks(lat_ref, ctx_ref, cols=slice(None)):
    n_lat = lat_ref.shape[0]
    return ([lat_ref[r:r + KEY_CHUNK, cols] for r in range(0, n_lat, KEY_CHUNK)]
            + [ctx_ref[:, cols]])


def _scores_stage(lhs, k_chunks, s_ref, m_ref, score_fn=None, floor=None):
    running = None
    off = 0
    for idx, k in enumerate(k_chunks):
        s = _dot_nt(lhs, k)
        if idx == 0 and score_fn is not None:
            s = score_fn(s)
        n = k.shape[0]
        s_ref[:, off:off + n] = s
        for c in range(0, n, LANES):
            tile = s[:, c:c + LANES]
            running = tile if running is None else jnp.maximum(running, tile)
        off += n
    m = jnp.max(running, axis=-1, keepdims=True)
    if floor is not None:
        m = jnp.maximum(m, floor)
    m_ref[...] = jnp.broadcast_to(m, m_ref.shape)


def _values_stage(s_ref, m_ref, v_chunks):
    m = m_ref[...]
    acc = None
    off = 0
    for v in v_chunks:
        n = v.shape[0]
        p = jnp.concatenate([jnp.exp2(s_ref[:, off + c:off + c + LANES] - m)
                             for c in range(0, n, LANES)], axis=1).astype(BF16)
        part = _dot(p, v)
        acc = part if acc is None else acc + part
        off += n
    return acc


def _pair_select(o, n_q):
    lane = lax.broadcasted_iota(jnp.int32, (1, LANES), 1)
    return jnp.where(lane < HEAD_DIM, o[:n_q], o[n_q:2 * n_q])


def _sub_ln(y, gain, post_scale):
    return _head_rms(y, gain) * post_scale


def _lam_value(lamv_ref, lam_init):
    t1 = jnp.sum(lamv_ref[0:1, :] * lamv_ref[1:2, :], axis=-1, keepdims=True)
    t2 = jnp.sum(lamv_ref[2:3, :] * lamv_ref[3:4, :], axis=-1, keepdims=True)
    return jnp.exp(t1) - jnp.exp(t2) + lam_init


def _sink_rows(sink_ref, pair, n_q):
    row = lax.broadcasted_iota(jnp.int32, (2 * n_q, 1), 0)
    return jnp.where(row < n_q, sink_ref[2 * pair], sink_ref[2 * pair + 1]) * LOG2E


def _diff_combine(o, n_q, lam):
    r = o[:, :LANES] / o[:, LANES:]
    return r[:n_q] - lam * r[n_q:]


def _gqa_output(groups):
    return jnp.concatenate(_swap_middle_heads(*groups), axis=1).astype(BF16)


def _bias_kernel(rel_ref, o_ref):
    h = pl.program_id(0)
    n_dr, n_dc = 2 * NA_KR - 1, 2 * NA_KC - 1
    q_col = lax.broadcasted_iota(jnp.int32, (GRID_W, LANES), 0)
    lane = lax.broadcasted_iota(jnp.int32, (GRID_W, LANES), 1)
    k_col = lane & (GRID_W - 1)
    dc = k_col - q_col + (NA_KC - 1)
    c_start = jnp.clip(q_col - NA_KC // 2, 0, GRID_W - NA_KC)
    in_window = (k_col >= c_start) & (k_col < c_start + NA_KC)
    tiles = []
    for dr in range(n_dr):
        t = jnp.zeros((GRID_W, LANES), F32)
        for d in range(n_dc):
            t = jnp.where(dc == d, rel_ref[(h * n_dr + dr) * n_dc + d], t)
        tiles.append(jnp.where(in_window, t * LOG2E, NEG_INF))
    low = lane < GRID_W
    for off in range(NA_KR):
        for c in range(NA_KR * GRID_W // LANES):
            dr = 2 * c - off + NA_KR - 1
            o_ref[0, off, :, c * LANES:(c + 1) * LANES] = jnp.where(low, tiles[dr], tiles[dr + 1])


def _attn_a_kernel(seq, sink_ref, q_ref, k_ref, v_ref, kc_ref, vc_ref, o_ref,
                   vl_ext, vc_ext, *bufs):
    n_q = BLOCK
    span = BLOCK + 2 * WINDOW
    n_blocks = seq // n_q
    masks = _lane_masks(2, BF16)
    s_refs, p_refs, m_refs = (_by_parity(bufs[i:i + 4]) for i in (0, 4, 8))
    _fill_ext(vl_ext, v_ref[...])
    _fill_ext(vc_ext, vc_ref[...])

    def window(n):
        start = min(max(n * n_q - WINDOW, 0), seq - span)
        return n * n_q, start

    for t in range(n_blocks + 2):
        new, old = t % 2, 1 - t % 2
        if 0 <= t - 2:
            q0, start = window(t - 2)
            v_loc = vl_ext[start:start + span, :]
            groups = []
            for pair in range(2):
                o = _weighted_values(p_refs[new][pair], [v_loc, vc_ext[...]])
                den = o[:, LANES:] + jnp.exp2(_sink_rows(sink_ref, pair, n_q)
                                              - m_refs[new][pair][...])
                groups.append(_pair_select(o[:, :LANES] / den, n_q))
            o_ref[q0:q0 + n_q, :] = _gqa_output(groups)
        if 0 <= t - 1 < n_blocks:
            for pair in range(2):
                m = _numerators(s_refs[old][pair], p_refs[old][pair],
                                floor=_sink_rows(sink_ref, pair, n_q))
                m_refs[old][pair][...] = jnp.broadcast_to(m, (2 * n_q, LANES))
        if t < n_blocks:
            q0, start = window(t)
            q = q_ref[q0:q0 + n_q, :]
            k_loc = k_ref[start:start + span, :]
            q_pos = q0 + (lax.broadcasted_iota(jnp.int32, (2 * n_q, 1), 0) & (n_q - 1))
            k_pos = start + lax.broadcasted_iota(jnp.int32, (1, span), 1)
            valid = jnp.abs(k_pos - q_pos) <= WINDOW
            for pair in range(2):
                lhs = _stack_masked(q[:, pair * LANES:(pair + 1) * LANES], masks)
                _scores(lhs, [k_loc, kc_ref[...]], s_refs[new][pair],
                        score_fn=lambda s, valid=valid: jnp.where(valid, s, NEG_INF))


EXT_ROWS = 16


def _fill_ext_t(ext_ref, v):
    row = lax.broadcasted_iota(jnp.int32, (LANES, LANES), 0)
    col = lax.broadcasted_iota(jnp.int32, (LANES, LANES), 1)
    eye = jnp.where(row == col, 1.0, 0.0).astype(BF16)
    ext_ref[:LANES, :] = _dot_nt(eye, v).astype(BF16)
    ext_ref[LANES:, :] = jnp.ones((EXT_ROWS, v.shape[0]), BF16)


def _value_chunks_t(lat_ref, ctx_ref):
    n_lat = lat_ref.shape[1]
    return [lat_ref[:, r:r + KEY_CHUNK] for r in range(0, n_lat, KEY_CHUNK)] + [ctx_ref[...]]


def _scores_stage_t(lhs, k_chunks, s_ref, m_ref):
    running = None
    off = 0
    for k in k_chunks:
        s = _dot_nt(k, lhs)
        n = k.shape[0]
        s_ref[off:off + n, :] = s
        top = jnp.max(s, axis=0, keepdims=True)
        running = top if running is None else jnp.maximum(running, top)
        off += n
        yield
    m_ref[...] = jnp.broadcast_to(running, m_ref.shape)


def _values_stage_t(s_ref, m_ref, vt_chunks, result):
    m = m_ref[0:1, :]
    acc = None
    off = 0
    for vt in vt_chunks:
        n = vt.shape[1]
        p = jnp.exp2(s_ref[off:off + n, :] - m).astype(BF16)
        part = _dot(vt, p)
        acc = part if acc is None else acc + part
        off += n
        yield
    result.append(acc)


def _interleave(*stages):
    stages = list(stages)
    while stages:
        stages = [stage for stage in stages if next(stage, stages) is not stages]


def _dense_slots(n_q, qp_ref, qn_ref, kp_ref, kn_ref, kcp_ref, kcn_ref):
    return ((0, qp_ref, slice(n_q, 2 * n_q), kp_ref, kcp_ref),
            (1, qn_ref, slice(0, n_q), kn_ref, kcn_ref))


def _refill_ext_per_sample(blocks_per_seq, fill):
    pl.when(lax.rem(jnp.maximum(pl.program_id(0) - 1, 0), blocks_per_seq) == 0)(fill)


def _attn_b_kernel(blocks_per_seq, qp_ref, qn_ref, kp_ref, kn_ref, kcp_ref, kcn_ref, v_ref,
                   vc_ref, o_ref, vl_ext, vc_ext, *bufs):
    n_q = o_ref.shape[0] // 2
    s_refs, m_refs = _by_parity(bufs[0:4]), _by_parity(bufs[4:8])
    _zero_at_first_step(bufs)

    def fill():
        _fill_ext_t(vl_ext, v_ref[...])
        _fill_ext_t(vc_ext, vc_ref[...])

    _refill_ext_per_sample(blocks_per_seq, fill)
    masks = _lane_masks(2, BF16)
    for parity, q_ref, q_rows, k_ref, kc_ref in _dense_slots(n_q, qp_ref, qn_ref, kp_ref, kn_ref,
                                                            kcp_ref, kcn_ref):
        done, todo = parity, 1 - parity
        q = q_ref[q_rows, :]
        results, stages = [], []
        for pair in range(2):
            results.append([])
            stages.append(_values_stage_t(s_refs[done][pair], m_refs[done][pair],
                                          _value_chunks_t(vl_ext, vc_ext), results[pair]))
            lhs = _stack_masked(q[:, pair * LANES:(pair + 1) * LANES], masks)
            stages.append(_scores_stage_t(lhs, _key_chunks(k_ref, kc_ref),
                                          s_refs[todo][pair], m_refs[todo][pair]))
        _interleave(*stages)
        groups = []
        for (o_t,) in results:
            inv = 1.0 / o_t[LANES:LANES + 1, :]
            lo_t = o_t[:HEAD_DIM, :n_q] * inv[:, :n_q]
            hi_t = o_t[HEAD_DIM:LANES, n_q:] * inv[:, n_q:]
            groups.append(jnp.concatenate([lo_t, hi_t], axis=0).T)
        o_ref[parity * n_q:(parity + 1) * n_q, :] = _gqa_output(groups)


def _attn_d_kernel(lam_init, blocks_per_seq, lamv_ref, gain_ref, qp_ref, qn_ref, kp_ref, kn_ref,
                   kcp_ref, kcn_ref, v_ref, vc_ref, o_ref, vl_ext, vc_ext, *bufs):
    n_q = o_ref.shape[0] // 2
    s_refs, m_refs = _by_parity(bufs[0:8]), _by_parity(bufs[8:16])
    _zero_at_first_step(bufs)

    def fill():
        for jb in range(2):
            cols = slice(jb * LANES, (jb + 1) * LANES)
            _fill_ext_t(vl_ext.at[jb], v_ref[:, cols])
            _fill_ext_t(vc_ext.at[jb], vc_ref[:, cols])

    _refill_ext_per_sample(blocks_per_seq, fill)
    lam = _lam_value(lamv_ref, lam_init)
    masks = _lane_masks(4, BF16)
    for parity, q_ref, q_rows, k_ref, kc_ref in _dense_slots(n_q, qp_ref, qn_ref, kp_ref, kn_ref,
                                                            kcp_ref, kcn_ref):
        done, todo = parity, 1 - parity
        q = q_ref[q_rows, :]
        results, stages = [], []
        for jb in range(2):
            cols = slice(jb * LANES, (jb + 1) * LANES)
            for head in range(2):
                g = 2 * jb + head
                results.append([])
                stages.append(_values_stage_t(s_refs[done][g], m_refs[done][g],
                                              _value_chunks_t(vl_ext.at[jb], vc_ext.at[jb]),
                                              results[g]))
                lhs = _stack_masked(q[:, cols], masks[2 * head:2 * head + 2])
                stages.append(_scores_stage_t(lhs, _key_chunks(k_ref, kc_ref, cols),
                                              s_refs[todo][g], m_refs[todo][g]))
        _interleave(*stages)
        outs = []
        for jb in range(2):
            ys_t = []
            for head in range(2):
                (o_t,) = results[2 * jb + head]
                dims = slice(head * HEAD_DIM, (head + 1) * HEAD_DIM)
                r = o_t[dims, :] * (1.0 / o_t[LANES:LANES + 1, :])
                ys_t.append(r[:, :n_q] - lam * r[:, n_q:])
            y = jnp.concatenate(ys_t, axis=0).T
            outs.append(_sub_ln(y, gain_ref[...], 1.0 - lam_init))
        o_ref[parity * n_q:(parity + 1) * n_q, :] = jnp.concatenate(outs, axis=1).astype(BF16)


def _attn_c_kernel(n_grid_rows, q_ref, k_ref, v_ref, kc_ref, vc_ref, bias_ref,
                   o_ref, vl_ext, vc_ext, *bufs):
    n_q = GRID_W
    n_loc = NA_KR * GRID_W
    masks = _lane_masks(2, BF16)
    s_refs, p_refs = _by_parity(bufs[0:4]), _by_parity(bufs[4:8])
    for jb in range(2):
        cols = slice(jb * LANES, (jb + 1) * LANES)
        _fill_ext(vl_ext.at[jb], v_ref[:, cols])
        _fill_ext(vc_ext.at[jb], vc_ref[:, cols])

    def geometry(r):
        r_start = min(max(r - NA_KR // 2, 0), n_grid_rows - NA_KR)
        return slice(r * n_q, (r + 1) * n_q), slice(r_start * GRID_W, r_start * GRID_W + n_loc), \
            r - r_start

    for t in range(n_grid_rows + 2):
        new, old = t % 2, 1 - t % 2
        if 0 <= t - 2:
            q_rows, k_rows, _ = geometry(t - 2)
            outs = []
            for jb in range(2):
                o = _weighted_values(p_refs[new][jb], [vl_ext[jb, k_rows, :], vc_ext[jb]])
                outs.append(_pair_select(o[:, :LANES] / o[:, LANES:], n_q))
            o_ref[q_rows, :] = jnp.concatenate(outs, axis=1).astype(BF16)
        if 0 <= t - 1 < n_grid_rows:
            for jb in range(2):
                _numerators(s_refs[old][jb], p_refs[old][jb])
        if t < n_grid_rows:
            q_rows, k_rows, off = geometry(t)
            q = q_ref[q_rows, :]
            for jb in range(2):
                cols = slice(jb * LANES, (jb + 1) * LANES)
                lhs = _stack_masked(q[:, cols], masks)
                bias = jnp.concatenate([bias_ref[2 * jb, off], bias_ref[2 * jb + 1, off]], axis=0)
                _scores(lhs, [k_ref[k_rows, cols], kc_ref[:, cols]], s_refs[new][jb],
                        score_fn=lambda s: s + bias)


def _attn_ctx_kernel(lam_init, sink_ref, lamv_ref, gain_ref, aq_ref, ak_ref, av_ref, bq_ref, bk_ref,
                     bv_ref, cq_ref, ck_ref, cv_ref, dq_ref, dk_ref, dv_ref, o_ref,
                     s0, s1, m0, m1):
    n_q = aq_ref.shape[0]
    m2 = _lane_masks(2, BF16)
    m4 = _lane_masks(4, BF16)
    lane = lax.broadcasted_iota(jnp.int32, (1, LANES), 1)
    sets = ((s0, m0), (s1, m1))

    def ext(v):
        return jnp.concatenate([v, jnp.ones(v.shape, v.dtype)], axis=1)

    def attend(lhs, k, v_ext, s_ref, m_ref, floor=None):
        _scores_stage(lhs, [k], s_ref, m_ref, floor=floor)
        return _values_stage(s_ref, m_ref, [v_ext])

    def out_cols(group, pair):
        c0 = group * GROUP_WIDTH + pair * LANES
        return slice(c0, c0 + LANES)

    for group, (q_ref_, k_ref_, v_ref_) in enumerate(((aq_ref, ak_ref, av_ref),
                                                      (bq_ref, bk_ref, bv_ref))):
        q = q_ref_[...]
        v_ext = ext(v_ref_[...])
        groups = []
        for pair, (s_ref, m_ref) in enumerate(sets):
            lhs = _stack_masked(q[:, pair * LANES:(pair + 1) * LANES], m2)
            sink = _sink_rows(sink_ref, pair, n_q) if group == 0 else None
            o = attend(lhs, k_ref_[...], v_ext, s_ref, m_ref, floor=sink)
            den = o[:, LANES:]
            if group == 0:
                den = den + jnp.exp2(sink - m_ref[...])
            groups.append(_pair_select(o[:, :LANES] / den, n_q))
        o_ref[:, group * GROUP_WIDTH:(group + 1) * GROUP_WIDTH] = _gqa_output(groups)

    q = cq_ref[...]
    for jb, (s_ref, m_ref) in enumerate(sets):
        cols = slice(jb * LANES, (jb + 1) * LANES)
        o = attend(_stack_masked(q[:, cols], m2), ck_ref[:, cols], ext(cv_ref[:, cols]),
                   s_ref, m_ref)
        o_ref[:, out_cols(2, jb)] = _pair_select(o[:, :LANES] / o[:, LANES:], n_q).astype(BF16)

    lam = _lam_value(lamv_ref, lam_init)
    q = dq_ref[...]
    for jb in range(2):
        cols = slice(jb * LANES, (jb + 1) * LANES)
        v_ext = ext(dv_ref[:, cols])
        ys = []
        for head, (s_ref, m_ref) in enumerate(sets):
            lhs = _stack_masked(q[:, cols], m4[2 * head:2 * head + 2])
            ys.append(_diff_combine(attend(lhs, dk_ref[:, cols], v_ext, s_ref, m_ref), n_q, lam))
        y = jnp.where(lane < HEAD_DIM, ys[0], ys[1])
        o_ref[:, out_cols(3, jb)] = _sub_ln(y, gain_ref[...], 1.0 - lam_init).astype(BF16)


def _rope_tables(seq, n_extra):
    t = np.arange(seq)
    row = (t // GRID_W).astype(np.float64)
    col = (t % GRID_W).astype(np.float64)
    tables = []
    for dim in (HEAD_DIM, D_SUB):
        half = dim // 2
        freqs = ROPE_BASE ** (-np.arange(0, half, 2, dtype=np.float64) / half)
        ang_r = row[:, None] * freqs[None, :]
        ang_c = col[:, None] * freqs[None, :]
        ang = np.concatenate([ang_r, ang_r, ang_c, ang_c], axis=-1)
        quarter = dim // 4
        sign = np.where((np.arange(dim) % (2 * quarter)) < quarter, -1.0, 1.0)
        cos = np.tile(np.cos(ang), (1, LANES // dim))
        sin = np.tile(np.sin(ang) * sign[None, :], (1, LANES // dim))
        cos = np.concatenate([cos, np.ones((n_extra, LANES))], axis=0)
        sin = np.concatenate([sin, np.zeros((n_extra, LANES))], axis=0)
        tables += [cos, sin]
    return jnp.asarray(np.stack(tables), dtype=F32)


def _neighbourhood_bias(rel_bias):
    heads = rel_bias.shape[0]
    return pl.pallas_call(
        _bias_kernel,
        grid=(heads,),
        in_specs=[pl.BlockSpec(memory_space=pltpu.SMEM)],
        out_specs=pl.BlockSpec((1, NA_KR, GRID_W, NA_KR * GRID_W), lambda h: (h, 0, 0, 0)),
        out_shape=jax.ShapeDtypeStruct((heads, NA_KR, GRID_W, NA_KR * GRID_W), F32),
        compiler_params=_params(("arbitrary",)),
        name="neighbourhood_bias",
    )(rel_bias.astype(F32).reshape(-1))


def _stage_scratch(n_rows, n_keys, n_groups=2):
    return ([pltpu.VMEM((n_rows, n_keys), F32)] * (2 * n_groups)
            + [pltpu.VMEM((n_rows, LANES), F32)] * (2 * n_groups))


def _stage_scratch_t(n_rows, n_keys, n_groups=2):
    return ([pltpu.VMEM((n_keys, n_rows), F32)] * (2 * n_groups)
            + [pltpu.VMEM((8, n_rows), F32)] * (2 * n_groups))


def _loop_scratch(n_rows, n_keys, with_maxima=False):
    maxima = [pltpu.VMEM((n_rows, LANES), F32)] * 4 if with_maxima else []
    return ([pltpu.VMEM((n_rows, n_keys), F32)] * 4 + [pltpu.VMEM((n_rows, n_keys), BF16)] * 4
            + maxima)


def kernel(x, c, ctx, c_ctx, w_ada, b_ada, w_ffn1_gate, w_ffn1_up, w_ffn1_down, w_in, w_out,
           sink_logit, q_norm_g, k_norm_g, rel_pos_bias, lam_q1, lam_k1, lam_q2, lam_k2, subln_g,
           w_ffn2_gate, w_ffn2_up, w_ffn2_down, final_norm_g):
    batch, seq, d = x.shape
    n_ctx = ctx.shape[1]
    depth = w_ada.shape[0]
    assert d == D_MODEL and seq % TOKEN_TILE == 0 and (batch * n_ctx) % TOKEN_TILE == 0
    assert seq % GRID_W == 0 and w_in.shape[-1] == IN_WIDTH and seq % KEY_CHUNK == 0
    assert seq % (2 * GQA_TILE) == 0 and seq % (2 * DIFF_TILE) == 0
    assert seq % (2 * BLOCK) == 0 and seq % (2 * GRID_W) == 0
    t_lat, t_ctx = batch * seq, batch * n_ctx
    t_all = t_lat + t_ctx
    tm = TOKEN_TILE
    n_lat_tiles, n_ctx_tiles = t_lat // tm, t_ctx // tm
    tiles_per_seq = seq // tm
    grid_rows = seq // GRID_W
    assert grid_rows >= NA_KR and rel_pos_bias.shape[1:] == (GROUP_HEADS, 2 * NA_KR - 1, 2 * NA_KC - 1)

    cc = jnp.concatenate([c, c_ctx[None, :], jnp.zeros((16 - batch - 1, d), F32)], axis=0)
    mods = _mods(cc, w_ada, b_ada).reshape(depth, 16, N_MOD, d)

    def group(i):
        return jnp.minimum(i // tiles_per_seq, batch)

    rope = _rope_tables(seq, tm)

    def rope_block(i):
        return jnp.where(i < n_lat_tiles, i % tiles_per_seq, tiles_per_seq)

    lane_gain = lambda g: jnp.tile(g.astype(F32), LANES // HEAD_DIM)[None, :]

    tile_spec = pl.BlockSpec((tm, d), lambda i: (i, 0))

    ffn1 = tuple(_to_bf16(w) for w in (w_ffn1_gate, w_ffn1_up, w_ffn1_down))
    ffn2 = tuple(_to_bf16(w) for w in (w_ffn2_gate, w_ffn2_up, w_ffn2_down))
    w_in_b = _to_bf16(w_in, swap_cols=(COL_AQ * LANES, COL_BQ * LANES))
    w_out_b = _to_bf16(w_out)

    for l in range(depth):
        last = l == depth - 1
        lam_init = 0.8 - 0.6 * math.exp(-0.3 * l)
        mod_spec = pl.BlockSpec((1, N_MOD, d), lambda i: (group(i), 0, 0))

        gains = jnp.concatenate([lane_gain(q_norm_g[l]), lane_gain(k_norm_g[l]),
                                 jnp.zeros((6, LANES), F32)], axis=0)
        if l == 0:
            streams = (x.reshape(t_lat, d), ctx.reshape(t_ctx, d))
            stream_specs = [
                pl.BlockSpec((tm, d), lambda i: (jnp.minimum(i, n_lat_tiles - 1), 0)),
                pl.BlockSpec((tm, d), lambda i: (jnp.maximum(i - n_lat_tiles, 0), 0))]
        else:
            streams, stream_specs = (xs,), [tile_spec]
        xs, qkv = pl.pallas_call(
            functools.partial(_pre_kernel, n_lat_tiles if l == 0 else None),
            grid=(n_lat_tiles + n_ctx_tiles,),
            in_specs=stream_specs + [
                mod_spec,
                _resident((d, D_FF), l), _resident((d, D_FF), l), _resident((D_FF, d), l),
                _resident((d, IN_WIDTH), l),
                pl.BlockSpec((4, tm, LANES), lambda i: (0, rope_block(i), 0)),
                _resident((8, LANES)),
            ],
            out_specs=[tile_spec, pl.BlockSpec((tm, IN_WIDTH), lambda i: (i, 0))],
            out_shape=[jax.ShapeDtypeStruct((t_all, d), F32),
                       jax.ShapeDtypeStruct((t_all, IN_WIDTH), BF16)],
            scratch_shapes=[pltpu.VMEM((tm, D_FF), BF16)],
            compiler_params=_params(("arbitrary",)),
            name=f"pre_l{l}",
        )(*streams, mods[l], *ffn1, w_in_b, rope, gains)

        ctx_row = t_lat // n_ctx
        sink_perm = sink_logit[l].astype(F32)[jnp.array(GQA_HEAD_ORDER)]
        lamv = jnp.pad(jnp.stack([lam_q1[l], lam_k1[l], lam_q2[l], lam_k2[l]]).astype(F32),
                       ((0, 4), (0, LANES - D_SUB)))
        subln = lane_gain(subln_g[l])
        smem = pl.BlockSpec(memory_space=pltpu.SMEM)

        def lat(cols, col):
            return pl.BlockSpec((seq, cols), lambda b, *_: (b, col))

        def ctxb(cols, col):
            return pl.BlockSpec((n_ctx, cols), lambda b, *_: (ctx_row + b, col))

        def v_ext_scratch(n_blocks):
            lead = () if n_blocks == 1 else (n_blocks,)
            return [pltpu.VMEM(lead + (seq, 2 * LANES), BF16),
                    pltpu.VMEM(lead + (n_ctx, 2 * LANES), BF16)]

        y_a = pl.pallas_call(
            functools.partial(_attn_a_kernel, seq),
            grid=(batch,),
            in_specs=[smem, lat(2 * LANES, COL_AQ // 2), lat(LANES, COL_AK), lat(LANES, COL_AV),
                      ctxb(LANES, COL_AK), ctxb(LANES, COL_AV)],
            out_specs=pl.BlockSpec((seq, GROUP_WIDTH), lambda b: (b, 0)),
            out_shape=jax.ShapeDtypeStruct((t_lat, GROUP_WIDTH), BF16),
            scratch_shapes=(v_ext_scratch(1)
                            + _loop_scratch(2 * BLOCK, BLOCK + 2 * WINDOW + n_ctx,
                                            with_maxima=True)),
            compiler_params=_params(("arbitrary",)),
            name=f"attn_a_l{l}",
        )(sink_perm, qkv, qkv, qkv, qkv, qkv)

        def v_ext_t_scratch(n_blocks):
            lead = () if n_blocks == 1 else (n_blocks,)
            return [pltpu.VMEM(lead + (LANES + EXT_ROWS, seq), BF16),
                    pltpu.VMEM(lead + (LANES + EXT_ROWS, n_ctx), BF16)]

        def dense_mixer(body, name, tq, n_groups, q_col, k_cols, k_col, v_col, params=(),
                        ext_blocks=0):
            blocks_per_seq = seq // (2 * tq)
            n_blocks = batch * blocks_per_seq

            def cur(j):
                return jnp.minimum(j, n_blocks - 1)

            def prev(j):
                return jnp.maximum(j - 1, 0)

            def sample_of(block):
                return lambda j: block(j) // blocks_per_seq

            specs = [pl.BlockSpec((2 * tq, 2 * LANES), lambda j: (prev(j), q_col)),
                     pl.BlockSpec((2 * tq, 2 * LANES), lambda j: (cur(j), q_col))]
            for rows, row0 in ((seq, 0), (n_ctx, ctx_row)):
                for block in (prev, cur):
                    specs.append(pl.BlockSpec(
                        (rows, k_cols), lambda j, b=sample_of(block), r=row0: (r + b(j), k_col)))
            for rows, row0 in ((seq, 0), (n_ctx, ctx_row)):
                specs.append(pl.BlockSpec(
                    (rows, k_cols), lambda j, b=sample_of(prev), r=row0: (r + b(j), v_col)))
            return pl.pallas_call(
                functools.partial(body, blocks_per_seq),
                grid=(n_blocks + 1,),
                in_specs=[_resident(p.shape) for p in params] + specs,
                out_specs=pl.BlockSpec((2 * tq, GROUP_WIDTH), lambda j: (prev(j), 0)),
                out_shape=jax.ShapeDtypeStruct((t_lat, GROUP_WIDTH), BF16),
                scratch_shapes=(v_ext_t_scratch(ext_blocks)
                                + _stage_scratch_t(2 * tq, seq + n_ctx, n_groups=n_groups)),
                compiler_params=_params(("arbitrary",)),
                name=name,
            )(*params, *([qkv] * 8))

        y_b = dense_mixer(_attn_b_kernel, f"attn_b_l{l}", GQA_TILE, 2,
                          COL_BQ // 2, LANES, COL_BK, COL_BV, ext_blocks=1)
        y_d = dense_mixer(functools.partial(_attn_d_kernel, lam_init), f"attn_d_l{l}", DIFF_TILE, 4,
                          COL_DQ // 2, 2 * LANES, COL_DK // 2, COL_DV // 2, params=(lamv, subln),
                          ext_blocks=2)

        bias = _neighbourhood_bias(rel_pos_bias[l])
        y_c = pl.pallas_call(
            functools.partial(_attn_c_kernel, grid_rows),
            grid=(batch,),
            in_specs=[lat(2 * LANES, COL_CQ // 2),
                      lat(2 * LANES, COL_CK // 2), lat(2 * LANES, COL_CV // 2),
                      ctxb(2 * LANES, COL_CK // 2), ctxb(2 * LANES, COL_CV // 2),
                      _resident(bias.shape)],
            out_specs=pl.BlockSpec((seq, GROUP_WIDTH), lambda b: (b, 0)),
            out_shape=jax.ShapeDtypeStruct((t_lat, GROUP_WIDTH), BF16),
            scratch_shapes=(v_ext_scratch(2)
                            + _loop_scratch(2 * GRID_W, NA_KR * GRID_W + n_ctx)),
            compiler_params=_params(("arbitrary",)),
            name=f"attn_c_l{l}",
        )(qkv, qkv, qkv, qkv, qkv, bias)

        ffn2_specs = [_resident((d, D_FF), l), _resident((d, D_FF), l), _resident((D_FF, d), l)]
        fng = final_norm_g.astype(F32)[None, :]
        y_spec = pl.BlockSpec((tm, GROUP_WIDTH), lambda i: (jnp.minimum(i, n_lat_tiles - 1), 0))
        post_scratch = [pltpu.VMEM((tm, d), BF16), pltpu.VMEM((tm, D_FF), BF16)]

        if not last:
            y_ctx = pl.pallas_call(
                functools.partial(_attn_ctx_kernel, lam_init),
                grid=(batch,),
                in_specs=[smem, _resident((8, LANES)), _resident((1, LANES)),
                          ctxb(2 * LANES, COL_AQ // 2), ctxb(LANES, COL_AK), ctxb(LANES, COL_AV),
                          ctxb(2 * LANES, COL_BQ // 2), ctxb(LANES, COL_BK), ctxb(LANES, COL_BV),
                          ctxb(2 * LANES, COL_CQ // 2), ctxb(2 * LANES, COL_CK // 2),
                          ctxb(2 * LANES, COL_CV // 2),
                          ctxb(2 * LANES, COL_DQ // 2), ctxb(2 * LANES, COL_DK // 2),
                          ctxb(2 * LANES, COL_DV // 2)],
                out_specs=pl.BlockSpec((n_ctx, d), lambda b: (b, 0)),
                out_shape=jax.ShapeDtypeStruct((t_ctx, d), BF16),
                scratch_shapes=_stage_scratch(2 * n_ctx, n_ctx, n_groups=1),
                compiler_params=_params(("arbitrary",)),
                name=f"attn_ctx_l{l}",
            )(sink_perm, lamv, subln, *([qkv] * 12))

            xs = pl.pallas_call(
                functools.partial(_post_kernel, n_lat_tiles, False),
                grid=(n_lat_tiles + n_ctx_tiles,),
                in_specs=[tile_spec, mod_spec, y_spec, y_spec, y_spec, y_spec,
                          pl.BlockSpec((tm, d), lambda i: (jnp.maximum(i - n_lat_tiles, 0), 0)),
                          _resident((d, d), l)] + ffn2_specs + [_resident((1, d))],
                out_specs=tile_spec,
                out_shape=jax.ShapeDtypeStruct((t_all, d), F32),
                scratch_shapes=post_scratch,
                compiler_params=_params(("arbitrary",)),
                name=f"post_l{l}",
            )(xs, mods[l], y_a, y_b, y_c, y_d, y_ctx, w_out_b, *ffn2, fng)
        else:
            xs = pl.pallas_call(
                functools.partial(_post_kernel, None, True),
                grid=(n_lat_tiles,),
                in_specs=[tile_spec, mod_spec, y_spec, y_spec, y_spec, y_spec,
                          _resident((d, d), l)] + ffn2_specs + [_resident((1, d))],
                out_specs=tile_spec,
                out_shape=jax.ShapeDtypeStruct((t_lat, d), F32),
                scratch_shapes=post_scratch,
                compiler_params=_params(("arbitrary",)),
                name=f"post_l{l}",
            )(xs, mods[l], y_a, y_b, y_c, y_d, w_out_b, *ffn2, fng)

    return xs.reshape(batch, seq, d)
```

```python
import functools
import math

import numpy as np
import jax
import jax.numpy as jnp
from jax import lax
from jax.experimental import pallas as pl
from jax.experimental.pallas import tpu as pltpu

D_MODEL = 1024
GRID_W = 64
HEAD_DIM = 64
N_GROUPS = 4
GROUP_HEADS = D_MODEL // (N_GROUPS * HEAD_DIM)
GROUP_WIDTH = GROUP_HEADS * HEAD_DIM
D_SUB = HEAD_DIM // 2
WINDOW = 128
BLOCK = 128
NA_KR = 8
NA_KC = 16
D_FF = 2816
ROPE_BASE = 10000.0
NORM_EPS = 1e-6
N_MOD = 9
NEG_INF = -1e30
IN_WIDTH = 2560

LANES = 128
MXU_N = 256
TOKEN_TILE = 512
GQA_TILE = 256
DIFF_TILE = 128
KEY_CHUNK = 256
VMEM_LIMIT = 56 * 1024 * 1024
LOG2E = 1.4426950408889634

BF16 = jnp.bfloat16
F32 = jnp.float32

COL_AQ, COL_AK, COL_AV = 0, 2, 3
COL_BQ, COL_BK, COL_BV = 4, 6, 7
COL_CQ, COL_CK, COL_CV = 8, 10, 12
COL_DQ, COL_DK, COL_DV = 14, 16, 18

GQA_HEAD_ORDER = (0, 2, 1, 3)


def _dot(a, b):
    return jnp.dot(a, b, preferred_element_type=F32)


def _dot_nt(a, b):
    return lax.dot_general(a, b, (((1,), (1,)), ((), ())), preferred_element_type=F32)


def _params(semantics):
    return pltpu.CompilerParams(dimension_semantics=semantics, vmem_limit_bytes=VMEM_LIMIT)


def _resident(shape, layer=None):
    nd = len(shape)
    if layer is None:
        return pl.BlockSpec(shape, lambda *_: (0,) * nd, pipeline_mode=pl.Buffered(1))
    return pl.BlockSpec((None,) + tuple(shape), lambda *_: (layer,) + (0,) * nd,
                        pipeline_mode=pl.Buffered(1))


def _swap_middle_heads(lo, hi):
    lane = lax.broadcasted_iota(jnp.int32, (1, LANES), 1)
    low = lane < HEAD_DIM
    return (jnp.where(low, lo, pltpu.roll(hi, HEAD_DIM, 1)),
            jnp.where(low, pltpu.roll(lo, HEAD_DIM, 1), hi))


CAST_BLOCK_BYTES = 6 * 1024 * 1024


def _cast_kernel(swap_cols, w_ref, o_ref):
    o_ref[...] = w_ref[...].astype(BF16)
    for c0 in swap_cols:
        lo, hi = _swap_middle_heads(w_ref[:, c0:c0 + LANES], w_ref[:, c0 + LANES:c0 + 2 * LANES])
        o_ref[:, c0:c0 + LANES] = lo.astype(BF16)
        o_ref[:, c0 + LANES:c0 + 2 * LANES] = hi.astype(BF16)


def _to_bf16(w, swap_cols=()):
    depth, r, c = w.shape
    tr = max(t for t in range(16, r + 1, 16) if r % t == 0 and t * c * 4 <= CAST_BLOCK_BYTES)
    spec = pl.BlockSpec((None, tr, c), lambda l, i: (l, i, 0))
    return pl.pallas_call(
        functools.partial(_cast_kernel, tuple(swap_cols)),
        grid=(depth, r // tr),
        in_specs=[spec],
        out_specs=spec,
        out_shape=jax.ShapeDtypeStruct(w.shape, BF16),
        compiler_params=_params(("arbitrary", "arbitrary")),
        name="cast_bf16",
    )(w)


def _mods_kernel(c_ref, w_ref, b_ref, o_ref):
    c = c_ref[...]
    a = c * (1.0 / (1.0 + jnp.exp(-c)))
    o_ref[...] = _dot(a.astype(BF16), w_ref[...].astype(BF16)) + b_ref[...]


def _mods(cc, w_ada, b_ada):
    depth, d, n = w_ada.shape
    rows = cc.shape[0]
    tn = 1152
    return pl.pallas_call(
        _mods_kernel,
        grid=(depth, n // tn),
        in_specs=[
            pl.BlockSpec((rows, d), lambda l, j: (0, 0)),
            pl.BlockSpec((None, d, tn), lambda l, j: (l, 0, j)),
            pl.BlockSpec((None, 1, tn), lambda l, j: (l, 0, j)),
        ],
        out_specs=pl.BlockSpec((None, rows, tn), lambda l, j: (l, 0, j)),
        out_shape=jax.ShapeDtypeStruct((depth, rows, n), F32),
        compiler_params=_params(("arbitrary", "arbitrary")),
        name="adaln_mods",
    )(cc, w_ada, b_ada.reshape(depth, 1, n))


def _modulated(x, mod_ref, k):
    shift = mod_ref[0, k:k + 1, :]
    scale = mod_ref[0, k + 1:k + 2, :]
    ms = jnp.mean(x * x, axis=-1, keepdims=True)
    return (x * lax.rsqrt(ms + NORM_EPS)) * (1.0 + scale) + shift


def _swiglu(h, wg_ref, wu_ref, wd_ref, act_ref):
    hb = h.astype(BF16)
    for j in range(D_FF // MXU_N):
        cols = slice(j * MXU_N, (j + 1) * MXU_N)
        g = _dot(hb, wg_ref[:, cols])
        u = _dot(hb, wu_ref[:, cols])
        act_ref[:, cols] = ((g * (1.0 / (1.0 + jnp.exp(-g)))) * u).astype(BF16)
    return _dot(act_ref[...], wd_ref[...])


def _rope(v, cos, sin_signed, quarter):
    lane = lax.broadcasted_iota(jnp.int32, (1, LANES), 1)
    first = (lane & (2 * quarter - 1)) < quarter
    rot = jnp.where(first, pltpu.roll(v, LANES - quarter, 1), pltpu.roll(v, quarter, 1))
    return v * cos + rot * sin_signed


def _head_rms(v, gain):
    lane = lax.broadcasted_iota(jnp.int32, (1, LANES), 1)
    lo = lane < HEAD_DIM
    sq = v * v
    ms_lo = jnp.sum(jnp.where(lo, sq, 0.0), axis=-1, keepdims=True) * (1.0 / HEAD_DIM)
    ms_hi = jnp.sum(jnp.where(lo, 0.0, sq), axis=-1, keepdims=True) * (1.0 / HEAD_DIM)
    rs = jnp.where(lo, lax.rsqrt(ms_lo + NORM_EPS), lax.rsqrt(ms_hi + NORM_EPS))
    return v * rs * gain


_QK_SCALE = HEAD_DIM ** -0.5 * LOG2E
_SUB_SCALE = D_SUB ** -0.5 * LOG2E
_PROJ_BLOCKS = (
    (16, -1, _QK_SCALE), (16, -1, _QK_SCALE), (16, -1, 1.0), (0, -1, 1.0),
    (16, 0, _QK_SCALE), (16, 0, _QK_SCALE), (16, 1, 1.0), (0, -1, 1.0),
    (0, -1, _QK_SCALE), (0, -1, _QK_SCALE), (0, -1, 1.0), (0, -1, 1.0),
    (0, -1, 1.0), (0, -1, 1.0),
    (8, -1, _SUB_SCALE), (8, -1, _SUB_SCALE), (8, -1, 1.0), (8, -1, 1.0),
    (0, -1, 1.0), (0, -1, 1.0),
)


def _project(h, w_ref, rope_ref, gain_ref, o_ref):
    hb = h.astype(BF16)
    groups = sorted(range(IN_WIDTH // MXU_N),
                    key=lambda j: -sum((q > 0) + 2 * (g >= 0) for q, g, _ in _PROJ_BLOCKS[2 * j:2 * j + 2]))
    for j in groups:
        r = _dot(hb, w_ref[:, j * MXU_N:(j + 1) * MXU_N])
        for half in range(MXU_N // LANES):
            blk = j * (MXU_N // LANES) + half
            quarter, gain_idx, scale = _PROJ_BLOCKS[blk]
            v = r[:, half * LANES:(half + 1) * LANES]
            if gain_idx >= 0:
                v = _head_rms(v, gain_ref[gain_idx:gain_idx + 1, :])
            if quarter:
                t = 0 if quarter == 16 else 2
                v = _rope(v, rope_ref[t], rope_ref[t + 1], quarter)
            if scale != 1.0:
                v = v * scale
            o_ref[:, blk * LANES:(blk + 1) * LANES] = v.astype(BF16)


def _pre_kernel(n_lat_tiles, *refs):
    if n_lat_tiles is None:
        (x_ref, mod_ref, wg_ref, wu_ref, wd_ref, win_ref, rope_ref, gain_ref,
         xo_ref, qkv_ref, act_ref) = refs
        x = x_ref[...]
    else:
        (x_ref, c_ref, mod_ref, wg_ref, wu_ref, wd_ref, win_ref, rope_ref, gain_ref,
         xo_ref, qkv_ref, act_ref) = refs
        x = jnp.where(pl.program_id(0) < n_lat_tiles, x_ref[...], c_ref[...])
    y = _swiglu(_modulated(x, mod_ref, 0), wg_ref, wu_ref, wd_ref, act_ref)
    x = x + (0.5 * mod_ref[0, 2:3, :]) * y
    xo_ref[...] = x
    _project(_modulated(x, mod_ref, 3), win_ref, rope_ref, gain_ref, qkv_ref)


def _post_kernel(n_lat_tiles, final_norm, *refs):
    if n_lat_tiles is None:
        (x_ref, mod_ref, ya_ref, yb_ref, yc_ref, yd_ref, wout_ref, wg_ref, wu_ref, wd_ref,
         fng_ref, xo_ref, y_scr, act_ref) = refs
        y_scr[...] = jnp.concatenate([ya_ref[...], yb_ref[...], yc_ref[...], yd_ref[...]], axis=1)
    else:
        (x_ref, mod_ref, ya_ref, yb_ref, yc_ref, yd_ref, yctx_ref, wout_ref, wg_ref, wu_ref,
         wd_ref, fng_ref, xo_ref, y_scr, act_ref) = refs
        is_latent = pl.program_id(0) < n_lat_tiles

        @pl.when(is_latent)
        def _():
            y_scr[...] = jnp.concatenate(
                [ya_ref[...], yb_ref[...], yc_ref[...], yd_ref[...]], axis=1)

        @pl.when(jnp.logical_not(is_latent))
        def _():
            y_scr[...] = yctx_ref[...]

    x = x_ref[...]
    x = x + mod_ref[0, 5:6, :] * _dot(y_scr[...], wout_ref[...])
    y = _swiglu(_modulated(x, mod_ref, 6), wg_ref, wu_ref, wd_ref, act_ref)
    x = x + (0.5 * mod_ref[0, 8:9, :]) * y
    if final_norm:
        ms = jnp.mean(x * x, axis=-1, keepdims=True)
        x = (x * lax.rsqrt(ms + NORM_EPS)) * fng_ref[...]
    xo_ref[...] = x


def _lane_masks(n, dtype):
    lane = lax.broadcasted_iota(jnp.int32, (1, LANES), 1)
    w = LANES // n
    return [jnp.where((lane >= k * w) & (lane < (k + 1) * w), 1.0, 0.0).astype(dtype)
            for k in range(n)]


def _stack_masked(q, masks):
    return jnp.concatenate([q * m for m in masks], axis=0)


def _fill_ext(ext_ref, v):
    ext_ref[:, :LANES] = v
    ext_ref[:, LANES:] = jnp.ones(v.shape, v.dtype)


def _by_parity(refs):
    half = len(refs) // 2
    return (tuple(refs[:half]), tuple(refs[half:]))


def _zero_at_first_step(refs):
    @pl.when(pl.program_id(0) == 0)
    def _():
        for ref in refs:
            ref[...] = jnp.zeros(ref.shape, ref.dtype)


ROW_BLOCK = 32


def _scores(lhs, k_list, s_ref, score_fn=None):
    off = 0
    for idx, k in enumerate(k_list):
        s = _dot_nt(lhs, k)
        if idx == 0 and score_fn is not None:
            s = score_fn(s)
        s_ref[:, off:off + k.shape[0]] = s
        off += k.shape[0]


def _numerators(s_ref, p_ref, floor=None):
    maxima = []
    for r in range(0, s_ref.shape[0], ROW_BLOCK):
        rows = slice(r, r + ROW_BLOCK)
        m = jnp.max(s_ref[rows, :], axis=-1, keepdims=True)
        if floor is not None:
            m = jnp.maximum(m, floor[rows])
        p_ref[rows, :] = jnp.exp2(s_ref[rows, :] - m).astype(BF16)
        maxima.append(m)
    return jnp.concatenate(maxima, axis=0)


def _weighted_values(p_ref, v_list):
    o = None
    off = 0
    for v in v_list:
        part = _dot(p_ref[:, off:off + v.shape[0]], v)
        o = part if o is None else o + part
        off += v.shape[0]
    return o


def _key_chunks(lat_ref, ctx_ref, cols=slice(None)):
    n_lat = lat_ref.shape[0]
    return ([lat_ref[r:r + KEY_CHUNK, cols] for r in range(0, n_lat, KEY_CHUNK)]
            + [ctx_ref[:, cols]])


def _scores_stage(lhs, k_chunks, s_ref, m_ref, score_fn=None, floor=None):
    running = None
    off = 0
    for idx, k in enumerate(k_chunks):
        s = _dot_nt(lhs, k)
        if idx == 0 and score_fn is not None:
            s = score_fn(s)
        n = k.shape[0]
        s_ref[:, off:off + n] = s
        for c in range(0, n, LANES):
            tile = s[:, c:c + LANES]
            running = tile if running is None else jnp.maximum(running, tile)
        off += n
    m = jnp.max(running, axis=-1, keepdims=True)
    if floor is not None:
        m = jnp.maximum(m, floor)
    m_ref[...] = jnp.broadcast_to(m, m_ref.shape)


def _values_stage(s_ref, m_ref, v_chunks):
    m = m_ref[...]
    acc = None
    off = 0
    for v in v_chunks:
        n = v.shape[0]
        p = jnp.concatenate([jnp.exp2(s_ref[:, off + c:off + c + LANES] - m)
                             for c in range(0, n, LANES)], axis=1).astype(BF16)
        part = _dot(p, v)
        acc = part if acc is None else acc + part
        off += n
    return acc


def _pair_select(o, n_q):
    lane = lax.broadcasted_iota(jnp.int32, (1, LANES), 1)
    return jnp.where(lane < HEAD_DIM, o[:n_q], o[n_q:2 * n_q])


def _sub_ln(y, gain, post_scale):
    return _head_rms(y, gain) * post_scale


def _lam_value(lamv_ref, lam_init):
    t1 = jnp.sum(lamv_ref[0:1, :] * lamv_ref[1:2, :], axis=-1, keepdims=True)
    t2 = jnp.sum(lamv_ref[2:3, :] * lamv_ref[3:4, :], axis=-1, keepdims=True)
    return jnp.exp(t1) - jnp.exp(t2) + lam_init


def _sink_rows(sink_ref, pair, n_q):
    row = lax.broadcasted_iota(jnp.int32, (2 * n_q, 1), 0)
    return jnp.where(row < n_q, sink_ref[2 * pair], sink_ref[2 * pair + 1]) * LOG2E


def _diff_combine(o, n_q, lam):
    r = o[:, :LANES] / o[:, LANES:]
    return r[:n_q] - lam * r[n_q:]


def _gqa_output(groups):
    return jnp.concatenate(_swap_middle_heads(*groups), axis=1).astype(BF16)


def _bias_kernel(rel_ref, o_ref):
    h = pl.program_id(0)
    n_dr, n_dc = 2 * NA_KR - 1, 2 * NA_KC - 1
    q_col = lax.broadcasted_iota(jnp.int32, (GRID_W, LANES), 0)
    lane = lax.broadcasted_iota(jnp.int32, (GRID_W, LANES), 1)
    k_col = lane & (GRID_W - 1)
    dc = k_col - q_col + (NA_KC - 1)
    c_start = jnp.clip(q_col - NA_KC // 2, 0, GRID_W - NA_KC)
    in_window = (k_col >= c_start) & (k_col < c_start + NA_KC)
    tiles = []
    for dr in range(n_dr):
        t = jnp.zeros((GRID_W, LANES), F32)
        for d in range(n_dc):
            t = jnp.where(dc == d, rel_ref[(h * n_dr + dr) * n_dc + d], t)
        tiles.append(jnp.where(in_window, t * LOG2E, NEG_INF))
    low = lane < GRID_W
    for off in range(NA_KR):
        for c in range(NA_KR * GRID_W // LANES):
            dr = 2 * c - off + NA_KR - 1
            o_ref[0, off, :, c * LANES:(c + 1) * LANES] = jnp.where(low, tiles[dr], tiles[dr + 1])


def _attn_a_kernel(seq, sink_ref, q_ref, k_ref, v_ref, kc_ref, vc_ref, o_ref,
                   vl_ext, vc_ext, *bufs):
    n_q = BLOCK
    span = BLOCK + 2 * WINDOW
    n_blocks = seq // n_q
    masks = _lane_masks(2, BF16)
    s_refs, p_refs, m_refs = (_by_parity(bufs[i:i + 4]) for i in (0, 4, 8))
    _fill_ext(vl_ext, v_ref[...])
    _fill_ext(vc_ext, vc_ref[...])

    def window(n):
        start = min(max(n * n_q - WINDOW, 0), seq - span)
        return n * n_q, start

    for t in range(n_blocks + 2):
        new, old = t % 2, 1 - t % 2
        if 0 <= t - 2:
            q0, start = window(t - 2)
            v_loc = vl_ext[start:start + span, :]
            groups = []
            for pair in range(2):
                o = _weighted_values(p_refs[new][pair], [v_loc, vc_ext[...]])
                den = o[:, LANES:] + jnp.exp2(_sink_rows(sink_ref, pair, n_q)
                                              - m_refs[new][pair][...])
                groups.append(_pair_select(o[:, :LANES] / den, n_q))
            o_ref[q0:q0 + n_q, :] = _gqa_output(groups)
        if 0 <= t - 1 < n_blocks:
            for pair in range(2):
                m = _numerators(s_refs[old][pair], p_refs[old][pair],
                                floor=_sink_rows(sink_ref, pair, n_q))
                m_refs[old][pair][...] = jnp.broadcast_to(m, (2 * n_q, LANES))
        if t < n_blocks:
            q0, start = window(t)
            q = q_ref[q0:q0 + n_q, :]
            k_loc = k_ref[start:start + span, :]
            q_pos = q0 + (lax.broadcasted_iota(jnp.int32, (2 * n_q, 1), 0) & (n_q - 1))
            k_pos = start + lax.broadcasted_iota(jnp.int32, (1, span), 1)
            valid = jnp.abs(k_pos - q_pos) <= WINDOW
            for pair in range(2):
                lhs = _stack_masked(q[:, pair * LANES:(pair + 1) * LANES], masks)
                _scores(lhs, [k_loc, kc_ref[...]], s_refs[new][pair],
                        score_fn=lambda s, valid=valid: jnp.where(valid, s, NEG_INF))


EXT_ROWS = 16


def _fill_ext_t(ext_ref, v):
    row = lax.broadcasted_iota(jnp.int32, (LANES, LANES), 0)
    col = lax.broadcasted_iota(jnp.int32, (LANES, LANES), 1)
    eye = jnp.where(row == col, 1.0, 0.0).astype(BF16)
    ext_ref[:LANES, :] = _dot_nt(eye, v).astype(BF16)
    ext_ref[LANES:, :] = jnp.ones((EXT_ROWS, v.shape[0]), BF16)


def _value_chunks_t(lat_ref, ctx_ref):
    n_lat = lat_ref.shape[1]
    return [lat_ref[:, r:r + KEY_CHUNK] for r in range(0, n_lat, KEY_CHUNK)] + [ctx_ref[...]]


def _scores_stage_t(lhs, k_chunks, s_ref, m_ref):
    running = None
    off = 0
    for k in k_chunks:
        s = _dot_nt(k, lhs)
        n = k.shape[0]
        s_ref[off:off + n, :] = s
        top = jnp.max(s, axis=0, keepdims=True)
        running = top if running is None else jnp.maximum(running, top)
        off += n
        yield
    m_ref[...] = jnp.broadcast_to(running, m_ref.shape)


def _values_stage_t(s_ref, m_ref, vt_chunks, result):
    m = m_ref[0:1, :]
    acc = None
    off = 0
    for vt in vt_chunks:
        n = vt.shape[1]
        p = jnp.exp2(s_ref[off:off + n, :] - m).astype(BF16)
        part = _dot(vt, p)
        acc = part if acc is None else acc + part
        off += n
        yield
    result.append(acc)


def _interleave(*stages):
    stages = list(stages)
    while stages:
        stages = [stage for stage in stages if next(stage, stages) is not stages]


def _dense_slots(n_q, qp_ref, qn_ref, kp_ref, kn_ref, kcp_ref, kcn_ref):
    return ((0, qp_ref, slice(n_q, 2 * n_q), kp_ref, kcp_ref),
            (1, qn_ref, slice(0, n_q), kn_ref, kcn_ref))


def _refill_ext_per_sample(blocks_per_seq, fill):
    pl.when(lax.rem(jnp.maximum(pl.program_id(0) - 1, 0), blocks_per_seq) == 0)(fill)


def _attn_b_kernel(blocks_per_seq, qp_ref, qn_ref, kp_ref, kn_ref, kcp_ref, kcn_ref, v_ref,
                   vc_ref, o_ref, vl_ext, vc_ext, *bufs):
    n_q = o_ref.shape[0] // 2
    s_refs, m_refs = _by_parity(bufs[0:4]), _by_parity(bufs[4:8])
    _zero_at_first_step(bufs)

    def fill():
        _fill_ext_t(vl_ext, v_ref[...])
        _fill_ext_t(vc_ext, vc_ref[...])

    _refill_ext_per_sample(blocks_per_seq, fill)
    masks = _lane_masks(2, BF16)
    for parity, q_ref, q_rows, k_ref, kc_ref in _dense_slots(n_q, qp_ref, qn_ref, kp_ref, kn_ref,
                                                            kcp_ref, kcn_ref):
        done, todo = parity, 1 - parity
        q = q_ref[q_rows, :]
        results, stages = [], []
        for pair in range(2):
            results.append([])
            stages.append(_values_stage_t(s_refs[done][pair], m_refs[done][pair],
                                          _value_chunks_t(vl_ext, vc_ext), results[pair]))
            lhs = _stack_masked(q[:, pair * LANES:(pair + 1) * LANES], masks)
            stages.append(_scores_stage_t(lhs, _key_chunks(k_ref, kc_ref),
                                          s_refs[todo][pair], m_refs[todo][pair]))
        _interleave(*stages)
        groups = []
        for (o_t,) in results:
            inv = 1.0 / o_t[LANES:LANES + 1, :]
            lo_t = o_t[:HEAD_DIM, :n_q] * inv[:, :n_q]
            hi_t = o_t[HEAD_DIM:LANES, n_q:] * inv[:, n_q:]
            groups.append(jnp.concatenate([lo_t, hi_t], axis=0).T)
        o_ref[parity * n_q:(parity + 1) * n_q, :] = _gqa_output(groups)


def _attn_d_kernel(lam_init, blocks_per_seq, lamv_ref, gain_ref, qp_ref, qn_ref, kp_ref, kn_ref,
                   kcp_ref, kcn_ref, v_ref, vc_ref, o_ref, vl_ext, vc_ext, *bufs):
    n_q = o_ref.shape[0] // 2
    s_refs, m_refs = _by_parity(bufs[0:8]), _by_parity(bufs[8:16])
    _zero_at_first_step(bufs)

    def fill():
        for jb in range(2):
            cols = slice(jb * LANES, (jb + 1) * LANES)
            _fill_ext_t(vl_ext.at[jb], v_ref[:, cols])
            _fill_ext_t(vc_ext.at[jb], vc_ref[:, cols])

    _refill_ext_per_sample(blocks_per_seq, fill)
    lam = _lam_value(lamv_ref, lam_init)
    masks = _lane_masks(4, BF16)
    for parity, q_ref, q_rows, k_ref, kc_ref in _dense_slots(n_q, qp_ref, qn_ref, kp_ref, kn_ref,
                                                            kcp_ref, kcn_ref):
        done, todo = parity, 1 - parity
        q = q_ref[q_rows, :]
        results, stages = [], []
        for jb in range(2):
            cols = slice(jb * LANES, (jb + 1) * LANES)
            for head in range(2):
                g = 2 * jb + head
                results.append([])
                stages.append(_values_stage_t(s_refs[done][g], m_refs[done][g],
                                              _value_chunks_t(vl_ext.at[jb], vc_ext.at[jb]),
                                              results[g]))
                lhs = _stack_masked(q[:, cols], masks[2 * head:2 * head + 2])
                stages.append(_scores_stage_t(lhs, _key_chunks(k_ref, kc_ref, cols),
                                              s_refs[todo][g], m_refs[todo][g]))
        _interleave(*stages)
        outs = []
        for jb in range(2):
            ys_t = []
            for head in range(2):
                (o_t,) = results[2 * jb + head]
                dims = slice(head * HEAD_DIM, (head + 1) * HEAD_DIM)
                r = o_t[dims, :] * (1.0 / o_t[LANES:LANES + 1, :])
                y_t = r[:, :n_q] - lam * r[:, n_q:]
                ms = jnp.mean(y_t * y_t, axis=0, keepdims=True)
                ys_t.append(y_t * lax.rsqrt(ms + NORM_EPS) * gain_ref[dims, :])
            outs.append(jnp.concatenate(ys_t, axis=0).T * (1.0 - lam_init))
        o_ref[parity * n_q:(parity + 1) * n_q, :] = jnp.concatenate(outs, axis=1).astype(BF16)


def _attn_c_kernel(n_grid_rows, q_ref, k_ref, v_ref, kc_ref, vc_ref, bias_ref,
                   o_ref, vl_ext, vc_ext, *bufs):
    n_q = GRID_W
    n_loc = NA_KR * GRID_W
    masks = _lane_masks(2, BF16)
    s_refs, p_refs = _by_parity(bufs[0:4]), _by_parity(bufs[4:8])
    for jb in range(2):
        cols = slice(jb * LANES, (jb + 1) * LANES)
        _fill_ext(vl_ext.at[jb], v_ref[:, cols])
        _fill_ext(vc_ext.at[jb], vc_ref[:, cols])

    def geometry(r):
        r_start = min(max(r - NA_KR // 2, 0), n_grid_rows - NA_KR)
        return slice(r * n_q, (r + 1) * n_q), slice(r_start * GRID_W, r_start * GRID_W + n_loc), \
            r - r_start

    for t in range(n_grid_rows + 2):
        new, old = t % 2, 1 - t % 2
        if 0 <= t - 2:
            q_rows, k_rows, _ = geometry(t - 2)
            outs = []
            for jb in range(2):
                o = _weighted_values(p_refs[new][jb], [vl_ext[jb, k_rows, :], vc_ext[jb]])
                outs.append(_pair_select(o[:, :LANES] / o[:, LANES:], n_q))
            o_ref[q_rows, :] = jnp.concatenate(outs, axis=1).astype(BF16)
        if 0 <= t - 1 < n_grid_rows:
            for jb in range(2):
                _numerators(s_refs[old][jb], p_refs[old][jb])
        if t < n_grid_rows:
            q_rows, k_rows, off = geometry(t)
            q = q_ref[q_rows, :]
            for jb in range(2):
                cols = slice(jb * LANES, (jb + 1) * LANES)
                lhs = _stack_masked(q[:, cols], masks)
                bias = jnp.concatenate([bias_ref[2 * jb, off], bias_ref[2 * jb + 1, off]], axis=0)
                _scores(lhs, [k_ref[k_rows, cols], kc_ref[:, cols]], s_refs[new][jb],
                        score_fn=lambda s: s + bias)


def _attn_ctx_kernel(lam_init, sink_ref, lamv_ref, gain_ref, aq_ref, ak_ref, av_ref, bq_ref, bk_ref,
                     bv_ref, cq_ref, ck_ref, cv_ref, dq_ref, dk_ref, dv_ref, o_ref,
                     s0, s1, m0, m1):
    n_q = aq_ref.shape[0]
    m2 = _lane_masks(2, BF16)
    m4 = _lane_masks(4, BF16)
    lane = lax.broadcasted_iota(jnp.int32, (1, LANES), 1)
    sets = ((s0, m0), (s1, m1))

    def ext(v):
        return jnp.concatenate([v, jnp.ones(v.shape, v.dtype)], axis=1)

    def attend(lhs, k, v_ext, s_ref, m_ref, floor=None):
        _scores_stage(lhs, [k], s_ref, m_ref, floor=floor)
        return _values_stage(s_ref, m_ref, [v_ext])

    def out_cols(group, pair):
        c0 = group * GROUP_WIDTH + pair * LANES
        return slice(c0, c0 + LANES)

    for group, (q_ref_, k_ref_, v_ref_) in enumerate(((aq_ref, ak_ref, av_ref),
                                                      (bq_ref, bk_ref, bv_ref))):
        q = q_ref_[...]
        v_ext = ext(v_ref_[...])
        groups = []
        for pair, (s_ref, m_ref) in enumerate(sets):
            lhs = _stack_masked(q[:, pair * LANES:(pair + 1) * LANES], m2)
            sink = _sink_rows(sink_ref, pair, n_q) if group == 0 else None
            o = attend(lhs, k_ref_[...], v_ext, s_ref, m_ref, floor=sink)
            den = o[:, LANES:]
            if group == 0:
                den = den + jnp.exp2(sink - m_ref[...])
            groups.append(_pair_select(o[:, :LANES] / den, n_q))
        o_ref[:, group * GROUP_WIDTH:(group + 1) * GROUP_WIDTH] = _gqa_output(groups)

    q = cq_ref[...]
    for jb, (s_ref, m_ref) in enumerate(sets):
        cols = slice(jb * LANES, (jb + 1) * LANES)
        o = attend(_stack_masked(q[:, cols], m2), ck_ref[:, cols], ext(cv_ref[:, cols]),
                   s_ref, m_ref)
        o_ref[:, out_cols(2, jb)] = _pair_select(o[:, :LANES] / o[:, LANES:], n_q).astype(BF16)

    lam = _lam_value(lamv_ref, lam_init)
    q = dq_ref[...]
    for jb in range(2):
        cols = slice(jb * LANES, (jb + 1) * LANES)
        v_ext = ext(dv_ref[:, cols])
        ys = []
        for head, (s_ref, m_ref) in enumerate(sets):
            lhs = _stack_masked(q[:, cols], m4[2 * head:2 * head + 2])
            ys.append(_diff_combine(attend(lhs, dk_ref[:, cols], v_ext, s_ref, m_ref), n_q, lam))
        y = jnp.where(lane < HEAD_DIM, ys[0], ys[1])
        o_ref[:, out_cols(3, jb)] = _sub_ln(y, gain_ref[...], 1.0 - lam_init).astype(BF16)


def _rope_tables(seq, n_extra):
    t = np.arange(seq)
    row = (t // GRID_W).astype(np.float64)
    col = (t % GRID_W).astype(np.float64)
    tables = []
    for dim in (HEAD_DIM, D_SUB):
        half = dim // 2
        freqs = ROPE_BASE ** (-np.arange(0, half, 2, dtype=np.float64) / half)
        ang_r = row[:, None] * freqs[None, :]
        ang_c = col[:, None] * freqs[None, :]
        ang = np.concatenate([ang_r, ang_r, ang_c, ang_c], axis=-1)
        quarter = dim // 4
        sign = np.where((np.arange(dim) % (2 * quarter)) < quarter, -1.0, 1.0)
        cos = np.tile(np.cos(ang), (1, LANES // dim))
        sin = np.tile(np.sin(ang) * sign[None, :], (1, LANES // dim))
        cos = np.concatenate([cos, np.ones((n_extra, LANES))], axis=0)
        sin = np.concatenate([sin, np.zeros((n_extra, LANES))], axis=0)
        tables += [cos, sin]
    return jnp.asarray(np.stack(tables), dtype=F32)


def _neighbourhood_bias(rel_bias):
    heads = rel_bias.shape[0]
    return pl.pallas_call(
        _bias_kernel,
        grid=(heads,),
        in_specs=[pl.BlockSpec(memory_space=pltpu.SMEM)],
        out_specs=pl.BlockSpec((1, NA_KR, GRID_W, NA_KR * GRID_W), lambda h: (h, 0, 0, 0)),
        out_shape=jax.ShapeDtypeStruct((heads, NA_KR, GRID_W, NA_KR * GRID_W), F32),
        compiler_params=_params(("arbitrary",)),
        name="neighbourhood_bias",
    )(rel_bias.astype(F32).reshape(-1))


def _stage_scratch(n_rows, n_keys, n_groups=2):
    return ([pltpu.VMEM((n_rows, n_keys), F32)] * (2 * n_groups)
            + [pltpu.VMEM((n_rows, LANES), F32)] * (2 * n_groups))


def _stage_scratch_t(n_rows, n_keys, n_groups=2):
    return ([pltpu.VMEM((n_keys, n_rows), F32)] * (2 * n_groups)
            + [pltpu.VMEM((8, n_rows), F32)] * (2 * n_groups))


def _loop_scratch(n_rows, n_keys, with_maxima=False):
    maxima = [pltpu.VMEM((n_rows, LANES), F32)] * 4 if with_maxima else []
    return ([pltpu.VMEM((n_rows, n_keys), F32)] * 4 + [pltpu.VMEM((n_rows, n_keys), BF16)] * 4
            + maxima)


def kernel(x, c, ctx, c_ctx, w_ada, b_ada, w_ffn1_gate, w_ffn1_up, w_ffn1_down, w_in, w_out,
           sink_logit, q_norm_g, k_norm_g, rel_pos_bias, lam_q1, lam_k1, lam_q2, lam_k2, subln_g,
           w_ffn2_gate, w_ffn2_up, w_ffn2_down, final_norm_g):
    batch, seq, d = x.shape
    n_ctx = ctx.shape[1]
    depth = w_ada.shape[0]
    assert d == D_MODEL and seq % TOKEN_TILE == 0 and (batch * n_ctx) % TOKEN_TILE == 0
    assert seq % GRID_W == 0 and w_in.shape[-1] == IN_WIDTH and seq % KEY_CHUNK == 0
    assert seq % (2 * GQA_TILE) == 0 and seq % (2 * DIFF_TILE) == 0
    assert seq % (2 * BLOCK) == 0 and seq % (2 * GRID_W) == 0
    t_lat, t_ctx = batch * seq, batch * n_ctx
    t_all = t_lat + t_ctx
    tm = TOKEN_TILE
    n_lat_tiles, n_ctx_tiles = t_lat // tm, t_ctx // tm
    tiles_per_seq = seq // tm
    grid_rows = seq // GRID_W
    assert grid_rows >= NA_KR and rel_pos_bias.shape[1:] == (GROUP_HEADS, 2 * NA_KR - 1, 2 * NA_KC - 1)

    cc = jnp.concatenate([c, c_ctx[None, :], jnp.zeros((16 - batch - 1, d), F32)], axis=0)
    mods = _mods(cc, w_ada, b_ada).reshape(depth, 16, N_MOD, d)

    def group(i):
        return jnp.minimum(i // tiles_per_seq, batch)

    rope = _rope_tables(seq, tm)

    def rope_block(i):
        return jnp.where(i < n_lat_tiles, i % tiles_per_seq, tiles_per_seq)

    lane_gain = lambda g: jnp.tile(g.astype(F32), LANES // HEAD_DIM)[None, :]

    tile_spec = pl.BlockSpec((tm, d), lambda i: (i, 0))

    ffn1 = tuple(_to_bf16(w) for w in (w_ffn1_gate, w_ffn1_up, w_ffn1_down))
    ffn2 = tuple(_to_bf16(w) for w in (w_ffn2_gate, w_ffn2_up, w_ffn2_down))
    w_in_b = _to_bf16(w_in, swap_cols=(COL_AQ * LANES, COL_BQ * LANES))
    w_out_b = _to_bf16(w_out)

    for l in range(depth):
        last = l == depth - 1
        lam_init = 0.8 - 0.6 * math.exp(-0.3 * l)
        mod_spec = pl.BlockSpec((1, N_MOD, d), lambda i: (group(i), 0, 0))

        gains = jnp.concatenate([lane_gain(q_norm_g[l]), lane_gain(k_norm_g[l]),
                                 jnp.zeros((6, LANES), F32)], axis=0)
        if l == 0:
            streams = (x.reshape(t_lat, d), ctx.reshape(t_ctx, d))
            stream_specs = [
                pl.BlockSpec((tm, d), lambda i: (jnp.minimum(i, n_lat_tiles - 1), 0)),
                pl.BlockSpec((tm, d), lambda i: (jnp.maximum(i - n_lat_tiles, 0), 0))]
        else:
            streams, stream_specs = (xs,), [tile_spec]
        xs, qkv = pl.pallas_call(
            functools.partial(_pre_kernel, n_lat_tiles if l == 0 else None),
            grid=(n_lat_tiles + n_ctx_tiles,),
            in_specs=stream_specs + [
                mod_spec,
                _resident((d, D_FF), l), _resident((d, D_FF), l), _resident((D_FF, d), l),
                _resident((d, IN_WIDTH), l),
                pl.BlockSpec((4, tm, LANES), lambda i: (0, rope_block(i), 0)),
                _resident((8, LANES)),
            ],
            out_specs=[tile_spec, pl.BlockSpec((tm, IN_WIDTH), lambda i: (i, 0))],
            out_shape=[jax.ShapeDtypeStruct((t_all, d), F32),
                       jax.ShapeDtypeStruct((t_all, IN_WIDTH), BF16)],
            scratch_shapes=[pltpu.VMEM((tm, D_FF), BF16)],
            compiler_params=_params(("arbitrary",)),
            name=f"pre_l{l}",
        )(*streams, mods[l], *ffn1, w_in_b, rope, gains)

        ctx_row = t_lat // n_ctx
        sink_perm = sink_logit[l].astype(F32)[jnp.array(GQA_HEAD_ORDER)]
        lamv = jnp.pad(jnp.stack([lam_q1[l], lam_k1[l], lam_q2[l], lam_k2[l]]).astype(F32),
                       ((0, 4), (0, LANES - D_SUB)))
        subln = lane_gain(subln_g[l])
        subln_t = jnp.broadcast_to(subln[0][:, None], (LANES, DIFF_TILE))
        smem = pl.BlockSpec(memory_space=pltpu.SMEM)

        def lat(cols, col):
            return pl.BlockSpec((seq, cols), lambda b, *_: (b, col))

        def ctxb(cols, col):
            return pl.BlockSpec((n_ctx, cols), lambda b, *_: (ctx_row + b, col))

        def v_ext_scratch(n_blocks):
            lead = () if n_blocks == 1 else (n_blocks,)
            return [pltpu.VMEM(lead + (seq, 2 * LANES), BF16),
                    pltpu.VMEM(lead + (n_ctx, 2 * LANES), BF16)]

        y_a = pl.pallas_call(
            functools.partial(_attn_a_kernel, seq),
            grid=(batch,),
            in_specs=[smem, lat(2 * LANES, COL_AQ // 2), lat(LANES, COL_AK), lat(LANES, COL_AV),
                      ctxb(LANES, COL_AK), ctxb(LANES, COL_AV)],
            out_specs=pl.BlockSpec((seq, GROUP_WIDTH), lambda b: (b, 0)),
            out_shape=jax.ShapeDtypeStruct((t_lat, GROUP_WIDTH), BF16),
            scratch_shapes=(v_ext_scratch(1)
                            + _loop_scratch(2 * BLOCK, BLOCK + 2 * WINDOW + n_ctx,
                                            with_maxima=True)),
            compiler_params=_params(("arbitrary",)),
            name=f"attn_a_l{l}",
        )(sink_perm, qkv, qkv, qkv, qkv, qkv)

        def v_ext_t_scratch(n_blocks):
            lead = () if n_blocks == 1 else (n_blocks,)
            return [pltpu.VMEM(lead + (LANES + EXT_ROWS, seq), BF16),
                    pltpu.VMEM(lead + (LANES + EXT_ROWS, n_ctx), BF16)]

        def dense_mixer(body, name, tq, n_groups, q_col, k_cols, k_col, v_col, params=(),
                        ext_blocks=0):
            blocks_per_seq = seq // (2 * tq)
            n_blocks = batch * blocks_per_seq

            def cur(j):
                return jnp.minimum(j, n_blocks - 1)

            def prev(j):
                return jnp.maximum(j - 1, 0)

            def sample_of(block):
                return lambda j: block(j) // blocks_per_seq

            specs = [pl.BlockSpec((2 * tq, 2 * LANES), lambda j: (prev(j), q_col)),
                     pl.BlockSpec((2 * tq, 2 * LANES), lambda j: (cur(j), q_col))]
            for rows, row0 in ((seq, 0), (n_ctx, ctx_row)):
                for block in (prev, cur):
                    specs.append(pl.BlockSpec(
                        (rows, k_cols), lambda j, b=sample_of(block), r=row0: (r + b(j), k_col)))
            for rows, row0 in ((seq, 0), (n_ctx, ctx_row)):
                specs.append(pl.BlockSpec(
                    (rows, k_cols), lambda j, b=sample_of(prev), r=row0: (r + b(j), v_col)))
            return pl.pallas_call(
                functools.partial(body, blocks_per_seq),
                grid=(n_blocks + 1,),
                in_specs=[_resident(p.shape) for p in params] + specs,
                out_specs=pl.BlockSpec((2 * tq, GROUP_WIDTH), lambda j: (prev(j), 0)),
                out_shape=jax.ShapeDtypeStruct((t_lat, GROUP_WIDTH), BF16),
                scratch_shapes=(v_ext_t_scratch(ext_blocks)
                                + _stage_scratch_t(2 * tq, seq + n_ctx, n_groups=n_groups)),
                compiler_params=_params(("arbitrary",)),
                name=name,
            )(*params, *([qkv] * 8))

        y_b = dense_mixer(_attn_b_kernel, f"attn_b_l{l}", GQA_TILE, 2,
                          COL_BQ // 2, LANES, COL_BK, COL_BV, ext_blocks=1)
        y_d = dense_mixer(functools.partial(_attn_d_kernel, lam_init), f"attn_d_l{l}", DIFF_TILE, 4,
                          COL_DQ // 2, 2 * LANES, COL_DK // 2, COL_DV // 2, params=(lamv, subln_t),
                          ext_blocks=2)

        bias = _neighbourhood_bias(rel_pos_bias[l])
        y_c = pl.pallas_call(
            functools.partial(_attn_c_kernel, grid_rows),
            grid=(batch,),
            in_specs=[lat(2 * LANES, COL_CQ // 2),
                      lat(2 * LANES, COL_CK // 2), lat(2 * LANES, COL_CV // 2),
                      ctxb(2 * LANES, COL_CK // 2), ctxb(2 * LANES, COL_CV // 2),
                      _resident(bias.shape)],
            out_specs=pl.BlockSpec((seq, GROUP_WIDTH), lambda b: (b, 0)),
            out_shape=jax.ShapeDtypeStruct((t_lat, GROUP_WIDTH), BF16),
            scratch_shapes=(v_ext_scratch(2)
                            + _loop_scratch(2 * GRID_W, NA_KR * GRID_W + n_ctx)),
            compiler_params=_params(("arbitrary",)),
            name=f"attn_c_l{l}",
        )(qkv, qkv, qkv, qkv, qkv, bias)

        ffn2_specs = [_resident((d, D_FF), l), _resident((d, D_FF), l), _resident((D_FF, d), l)]
        fng = final_norm_g.astype(F32)[None, :]
        y_spec = pl.BlockSpec((tm, GROUP_WIDTH), lambda i: (jnp.minimum(i, n_lat_tiles - 1), 0))
        post_scratch = [pltpu.VMEM((tm, d), BF16), pltpu.VMEM((tm, D_FF), BF16)]

        if not last:
            y_ctx = pl.pallas_call(
                functools.partial(_attn_ctx_kernel, lam_init),
                grid=(batch,),
                in_specs=[smem, _resident((8, LANES)), _resident((1, LANES)),
                          ctxb(2 * LANES, COL_AQ // 2), ctxb(LANES, COL_AK), ctxb(LANES, COL_AV),
                          ctxb(2 * LANES, COL_BQ // 2), ctxb(LANES, COL_BK), ctxb(LANES, COL_BV),
                          ctxb(2 * LANES, COL_CQ // 2), ctxb(2 * LANES, COL_CK // 2),
                          ctxb(2 * LANES, COL_CV // 2),
                          ctxb(2 * LANES, COL_DQ // 2), ctxb(2 * LANES, COL_DK // 2),
                          ctxb(2 * LANES, COL_DV // 2)],
                out_specs=pl.BlockSpec((n_ctx, d), lambda b: (b, 0)),
                out_shape=jax.ShapeDtypeStruct((t_ctx, d), BF16),
                scratch_shapes=_stage_scratch(2 * n_ctx, n_ctx, n_groups=1),
                compiler_params=_params(("arbitrary",)),
                name=f"attn_ctx_l{l}",
            )(sink_perm, lamv, subln, *([qkv] * 12))

            xs = pl.pallas_call(
                functools.partial(_post_kernel, n_lat_tiles, False),
                grid=(n_lat_tiles + n_ctx_tiles,),
                in_specs=[tile_spec, mod_spec, y_spec, y_spec, y_spec, y_spec,
                          pl.BlockSpec((tm, d), lambda i: (jnp.maximum(i - n_lat_tiles, 0), 0)),
                          _resident((d, d), l)] + ffn2_specs + [_resident((1, d))],
                out_specs=tile_spec,
                out_shape=jax.ShapeDtypeStruct((t_all, d), F32),
                scratch_shapes=post_scratch,
                compiler_params=_params(("arbitrary",)),
                name=f"post_l{l}",
            )(xs, mods[l], y_a, y_b, y_c, y_d, y_ctx, w_out_b, *ffn2, fng)
        else:
            xs = pl.pallas_call(
                functools.partial(_post_kernel, None, True),
                grid=(n_lat_tiles,),
                in_specs=[tile_spec, mod_spec, y_spec, y_spec, y_spec, y_spec,
                          _resident((d, d), l)] + ffn2_specs + [_resident((1, d))],
                out_specs=tile_spec,
                out_shape=jax.ShapeDtypeStruct((t_lat, d), F32),
                scratch_shapes=post_scratch,
                compiler_params=_params(("arbitrary",)),
                name=f"post_l{l}",
            )(xs, mods[l], y_a, y_b, y_c, y_d, w_out_b, *ffn2, fng)

    return xs.reshape(batch, seq, d)
```

```python
import functools
import math

import numpy as np
import jax
import jax.numpy as jnp
from jax import lax
from jax.experimental import pallas as pl
from jax.experimental.pallas import tpu as pltpu

D_MODEL = 1024
GRID_W = 64
HEAD_DIM = 64
N_GROUPS = 4
GROUP_HEADS = D_MODEL // (N_GROUPS * HEAD_DIM)
GROUP_WIDTH = GROUP_HEADS * HEAD_DIM
D_SUB = HEAD_DIM // 2
WINDOW = 128
BLOCK = 128
NA_KR = 8
NA_KC = 16
D_FF = 2816
ROPE_BASE = 10000.0
NORM_EPS = 1e-6
N_MOD = 9
NEG_INF = -1e30
IN_WIDTH = 2560

LANES = 128
MXU_N = 256
TOKEN_TILE = 512
GQA_TILE = 256
DIFF_TILE = 128
KEY_CHUNK = 256
VMEM_LIMIT = 56 * 1024 * 1024
LOG2E = 1.4426950408889634

BF16 = jnp.bfloat16
F32 = jnp.float32

COL_AQ, COL_AK, COL_AV = 0, 2, 3
COL_BQ, COL_BK, COL_BV = 4, 6, 7
COL_CQ, COL_CK, COL_CV = 8, 10, 12
COL_DQ, COL_DK, COL_DV = 14, 16, 18

GQA_HEAD_ORDER = (0, 2, 1, 3)


def _dot(a, b):
    return jnp.dot(a, b, preferred_element_type=F32)


def _dot_nt(a, b):
    return lax.dot_general(a, b, (((1,), (1,)), ((), ())), preferred_element_type=F32)


def _params(semantics):
    return pltpu.CompilerParams(dimension_semantics=semantics, vmem_limit_bytes=VMEM_LIMIT)


def _resident(shape, layer=None):
    nd = len(shape)
    if layer is None:
        return pl.BlockSpec(shape, lambda *_: (0,) * nd, pipeline_mode=pl.Buffered(1))
    return pl.BlockSpec((None,) + tuple(shape), lambda *_: (layer,) + (0,) * nd,
                        pipeline_mode=pl.Buffered(1))


def _swap_middle_heads(lo, hi):
    lane = lax.broadcasted_iota(jnp.int32, (1, LANES), 1)
    low = lane < HEAD_DIM
    return (jnp.where(low, lo, pltpu.roll(hi, HEAD_DIM, 1)),
            jnp.where(low, pltpu.roll(lo, HEAD_DIM, 1), hi))


CAST_BLOCK_BYTES = 6 * 1024 * 1024


def _cast_kernel(swap_cols, w_ref, o_ref):
    o_ref[...] = w_ref[...].astype(BF16)
    for c0 in swap_cols:
        lo, hi = _swap_middle_heads(w_ref[:, c0:c0 + LANES], w_ref[:, c0 + LANES:c0 + 2 * LANES])
        o_ref[:, c0:c0 + LANES] = lo.astype(BF16)
        o_ref[:, c0 + LANES:c0 + 2 * LANES] = hi.astype(BF16)


def _to_bf16(w, swap_cols=()):
    depth, r, c = w.shape
    tr = max(t for t in range(16, r + 1, 16) if r % t == 0 and t * c * 4 <= CAST_BLOCK_BYTES)
    spec = pl.BlockSpec((None, tr, c), lambda l, i: (l, i, 0))
    return pl.pallas_call(
        functools.partial(_cast_kernel, tuple(swap_cols)),
        grid=(depth, r // tr),
        in_specs=[spec],
        out_specs=spec,
        out_shape=jax.ShapeDtypeStruct(w.shape, BF16),
        compiler_params=_params(("arbitrary", "arbitrary")),
        name="cast_bf16",
    )(w)


def _mods_kernel(c_ref, w_ref, b_ref, o_ref):
    c = c_ref[...]
    a = c * (1.0 / (1.0 + jnp.exp(-c)))
    o_ref[...] = _dot(a.astype(BF16), w_ref[...].astype(BF16)) + b_ref[...]


def _mods(cc, w_ada, b_ada):
    depth, d, n = w_ada.shape
    rows = cc.shape[0]
    tn = 1152
    return pl.pallas_call(
        _mods_kernel,
        grid=(depth, n // tn),
        in_specs=[
            pl.BlockSpec((rows, d), lambda l, j: (0, 0)),
            pl.BlockSpec((None, d, tn), lambda l, j: (l, 0, j)),
            pl.BlockSpec((None, 1, tn), lambda l, j: (l, 0, j)),
        ],
        out_specs=pl.BlockSpec((None, rows, tn), lambda l, j: (l, 0, j)),
        out_shape=jax.ShapeDtypeStruct((depth, rows, n), F32),
        compiler_params=_params(("arbitrary", "arbitrary")),
        name="adaln_mods",
    )(cc, w_ada, b_ada.reshape(depth, 1, n))


def _modulated(x, mod_ref, k):
    shift = mod_ref[0, k:k + 1, :]
    scale = mod_ref[0, k + 1:k + 2, :]
    ms = jnp.mean(x * x, axis=-1, keepdims=True)
    return (x * lax.rsqrt(ms + NORM_EPS)) * (1.0 + scale) + shift


def _swiglu(h, wg_ref, wu_ref, wd_ref, act_ref):
    hb = h.astype(BF16)
    for j in range(D_FF // MXU_N):
        cols = slice(j * MXU_N, (j + 1) * MXU_N)
        g = _dot(hb, wg_ref[:, cols])
        u = _dot(hb, wu_ref[:, cols])
        act_ref[:, cols] = ((g * (1.0 / (1.0 + jnp.exp(-g)))) * u).astype(BF16)
    return _dot(act_ref[...], wd_ref[...])


def _rope(v, cos, sin_signed, quarter):
    lane = lax.broadcasted_iota(jnp.int32, (1, LANES), 1)
    first = (lane & (2 * quarter - 1)) < quarter
    rot = jnp.where(first, pltpu.roll(v, LANES - quarter, 1), pltpu.roll(v, quarter, 1))
    return v * cos + rot * sin_signed


def _head_rms(v, gain):
    lane = lax.broadcasted_iota(jnp.int32, (1, LANES), 1)
    lo = lane < HEAD_DIM
    sq = v * v
    ms_lo = jnp.sum(jnp.where(lo, sq, 0.0), axis=-1, keepdims=True) * (1.0 / HEAD_DIM)
    ms_hi = jnp.sum(jnp.where(lo, 0.0, sq), axis=-1, keepdims=True) * (1.0 / HEAD_DIM)
    rs = jnp.where(lo, lax.rsqrt(ms_lo + NORM_EPS), lax.rsqrt(ms_hi + NORM_EPS))
    return v * rs * gain


_QK_SCALE = HEAD_DIM ** -0.5 * LOG2E
_SUB_SCALE = D_SUB ** -0.5 * LOG2E
_PROJ_BLOCKS = (
    (16, -1, _QK_SCALE), (16, -1, _QK_SCALE), (16, -1, 1.0), (0, -1, 1.0),
    (16, 0, _QK_SCALE), (16, 0, _QK_SCALE), (16, 1, 1.0), (0, -1, 1.0),
    (0, -1, _QK_SCALE), (0, -1, _QK_SCALE), (0, -1, 1.0), (0, -1, 1.0),
    (0, -1, 1.0), (0, -1, 1.0),
    (8, -1, _SUB_SCALE), (8, -1, _SUB_SCALE), (8, -1, 1.0), (8, -1, 1.0),
    (0, -1, 1.0), (0, -1, 1.0),
)


def _project(h, w_ref, rope_ref, gain_ref, o_ref):
    hb = h.astype(BF16)
    groups = sorted(range(IN_WIDTH // MXU_N),
                    key=lambda j: -sum((q > 0) + 2 * (g >= 0) for q, g, _ in _PROJ_BLOCKS[2 * j:2 * j + 2]))
    for j in groups:
        r = _dot(hb, w_ref[:, j * MXU_N:(j + 1) * MXU_N])
        for half in range(MXU_N // LANES):
            blk = j * (MXU_N // LANES) + half
            quarter, gain_idx, scale = _PROJ_BLOCKS[blk]
            v = r[:, half * LANES:(half + 1) * LANES]
            if gain_idx >= 0:
                v = _head_rms(v, gain_ref[gain_idx:gain_idx + 1, :])
            if quarter:
                t = 0 if quarter == 16 else 2
                v = _rope(v, rope_ref[t], rope_ref[t + 1], quarter)
            if scale != 1.0:
                v = v * scale
            o_ref[:, blk * LANES:(blk + 1) * LANES] = v.astype(BF16)


def _pre_kernel(n_lat_tiles, *refs):
    if n_lat_tiles is None:
        (x_ref, mod_ref, wg_ref, wu_ref, wd_ref, win_ref, rope_ref, gain_ref,
         xo_ref, qkv_ref, act_ref) = refs
        x = x_ref[...]
    else:
        (x_ref, c_ref, mod_ref, wg_ref, wu_ref, wd_ref, win_ref, rope_ref, gain_ref,
         xo_ref, qkv_ref, act_ref) = refs
        x = jnp.where(pl.program_id(0) < n_lat_tiles, x_ref[...], c_ref[...])
    y = _swiglu(_modulated(x, mod_ref, 0), wg_ref, wu_ref, wd_ref, act_ref)
    x = x + (0.5 * mod_ref[0, 2:3, :]) * y
    xo_ref[...] = x
    _project(_modulated(x, mod_ref, 3), win_ref, rope_ref, gain_ref, qkv_ref)


def _post_kernel(n_lat_tiles, final_norm, *refs):
    if n_lat_tiles is None:
        (x_ref, mod_ref, ya_ref, yb_ref, yc_ref, yd_ref, wout_ref, wg_ref, wu_ref, wd_ref,
         fng_ref, xo_ref, y_scr, act_ref) = refs
        y_scr[...] = jnp.concatenate([ya_ref[...], yb_ref[...], yc_ref[...], yd_ref[...]], axis=1)
    else:
        (x_ref, mod_ref, ya_ref, yb_ref, yc_ref, yd_ref, yctx_ref, wout_ref, wg_ref, wu_ref,
         wd_ref, fng_ref, xo_ref, y_scr, act_ref) = refs
        is_latent = pl.program_id(0) < n_lat_tiles

        @pl.when(is_latent)
        def _():
            y_scr[...] = jnp.concatenate(
                [ya_ref[...], yb_ref[...], yc_ref[...], yd_ref[...]], axis=1)

        @pl.when(jnp.logical_not(is_latent))
        def _():
            y_scr[...] = yctx_ref[...]

    x = x_ref[...]
    x = x + mod_ref[0, 5:6, :] * _dot(y_scr[...], wout_ref[...])
    y = _swiglu(_modulated(x, mod_ref, 6), wg_ref, wu_ref, wd_ref, act_ref)
    x = x + (0.5 * mod_ref[0, 8:9, :]) * y
    if final_norm:
        ms = jnp.mean(x * x, axis=-1, keepdims=True)
        x = (x * lax.rsqrt(ms + NORM_EPS)) * fng_ref[...]
    xo_ref[...] = x


def _lane_masks(n, dtype):
    lane = lax.broadcasted_iota(jnp.int32, (1, LANES), 1)
    w = LANES // n
    return [jnp.where((lane >= k * w) & (lane < (k + 1) * w), 1.0, 0.0).astype(dtype)
            for k in range(n)]


def _stack_masked(q, masks):
    return jnp.concatenate([q * m for m in masks], axis=0)


def _fill_ext(ext_ref, v):
    ext_ref[:, :LANES] = v
    ext_ref[:, LANES:] = jnp.ones(v.shape, v.dtype)


def _by_parity(refs):
    half = len(refs) // 2
    return (tuple(refs[:half]), tuple(refs[half:]))


def _zero_at_first_step(refs):
    @pl.when(pl.program_id(0) == 0)
    def _():
        for ref in refs:
            ref[...] = jnp.zeros(ref.shape, ref.dtype)


ROW_BLOCK = 32


def _scores(lhs, k_list, s_ref, score_fn=None):
    off = 0
    for idx, k in enumerate(k_list):
        s = _dot_nt(lhs, k)
        if idx == 0 and score_fn is not None:
            s = score_fn(s)
        s_ref[:, off:off + k.shape[0]] = s
        off += k.shape[0]


def _numerators(s_ref, p_ref, floor=None):
    maxima = []
    for r in range(0, s_ref.shape[0], ROW_BLOCK):
        rows = slice(r, r + ROW_BLOCK)
        m = jnp.max(s_ref[rows, :], axis=-1, keepdims=True)
        if floor is not None:
            m = jnp.maximum(m, floor[rows])
        p_ref[rows, :] = jnp.exp2(s_ref[rows, :] - m).astype(BF16)
        maxima.append(m)
    return jnp.concatenate(maxima, axis=0)


def _weighted_values(p_ref, v_list):
    o = None
    off = 0
    for v in v_list:
        part = _dot(p_ref[:, off:off + v.shape[0]], v)
        o = part if o is None else o + part
        off += v.shape[0]
    return o


def _key_chunks(lat_ref, ctx_ref, cols=slice(None)):
    n_lat = lat_ref.shape[0]
    return ([lat_ref[r:r + KEY_CHUNK, cols] for r in range(0, n_lat, KEY_CHUNK)]
            + [ctx_ref[:, cols]])


def _scores_stage(lhs, k_chunks, s_ref, m_ref, score_fn=None, floor=None):
    running = None
    off = 0
    for idx, k in enumerate(k_chunks):
        s = _dot_nt(lhs, k)
        if idx == 0 and score_fn is not None:
            s = score_fn(s)
        n = k.shape[0]
        s_ref[:, off:off + n] = s
        for c in range(0, n, LANES):
            tile = s[:, c:c + LANES]
            running = tile if running is None else jnp.maximum(running, tile)
        off += n
    m = jnp.max(running, axis=-1, keepdims=True)
    if floor is not None:
        m = jnp.maximum(m, floor)
    m_ref[...] = jnp.broadcast_to(m, m_ref.shape)


def _values_stage(s_ref, m_ref, v_chunks):
    m = m_ref[...]
    acc = None
    off = 0
    for v in v_chunks:
        n = v.shape[0]
        p = jnp.concatenate([jnp.exp2(s_ref[:, off + c:off + c + LANES] - m)
                             for c in range(0, n, LANES)], axis=1).astype(BF16)
        part = _dot(p, v)
        acc = part if acc is None else acc + part
        off += n
    return acc


def _pair_select(o, n_q):
    lane = lax.broadcasted_iota(jnp.int32, (1, LANES), 1)
    return jnp.where(lane < HEAD_DIM, o[:n_q], o[n_q:2 * n_q])


def _sub_ln(y, gain, post_scale):
    return _head_rms(y, gain) * post_scale


def _lam_value(lamv_ref, lam_init):
    t1 = jnp.sum(lamv_ref[0:1, :] * lamv_ref[1:2, :], axis=-1, keepdims=True)
    t2 = jnp.sum(lamv_ref[2:3, :] * lamv_ref[3:4, :], axis=-1, keepdims=True)
    return jnp.exp(t1) - jnp.exp(t2) + lam_init


def _sink_rows(sink_ref, pair, n_q):
    row = lax.broadcasted_iota(jnp.int32, (2 * n_q, 1), 0)
    return jnp.where(row < n_q, sink_ref[2 * pair], sink_ref[2 * pair + 1]) * LOG2E


def _diff_combine(o, n_q, lam):
    r = o[:, :LANES] / o[:, LANES:]
    return r[:n_q] - lam * r[n_q:]


def _gqa_output(groups):
    return jnp.concatenate(_swap_middle_heads(*groups), axis=1).astype(BF16)


def _bias_kernel(rel_ref, o_ref):
    h = pl.program_id(0)
    n_dr, n_dc = 2 * NA_KR - 1, 2 * NA_KC - 1
    q_col = lax.broadcasted_iota(jnp.int32, (GRID_W, LANES), 0)
    lane = lax.broadcasted_iota(jnp.int32, (GRID_W, LANES), 1)
    k_col = lane & (GRID_W - 1)
    dc = k_col - q_col + (NA_KC - 1)
    c_start = jnp.clip(q_col - NA_KC // 2, 0, GRID_W - NA_KC)
    in_window = (k_col >= c_start) & (k_col < c_start + NA_KC)
    tiles = []
    for dr in range(n_dr):
        t = jnp.zeros((GRID_W, LANES), F32)
        for d in range(n_dc):
            t = jnp.where(dc == d, rel_ref[(h * n_dr + dr) * n_dc + d], t)
        tiles.append(jnp.where(in_window, t * LOG2E, NEG_INF))
    low = lane < GRID_W
    for off in range(NA_KR):
        for c in range(NA_KR * GRID_W // LANES):
            dr = 2 * c - off + NA_KR - 1
            o_ref[0, off, :, c * LANES:(c + 1) * LANES] = jnp.where(low, tiles[dr], tiles[dr + 1])


def _attn_a_kernel(seq, sink_ref, q_ref, k_ref, v_ref, kc_ref, vc_ref, o_ref,
                   vl_ext, vc_ext, *bufs):
    n_q = BLOCK
    span = BLOCK + 2 * WINDOW
    n_blocks = seq // n_q
    masks = _lane_masks(2, BF16)
    s_refs, p_refs, m_refs = (_by_parity(bufs[i:i + 4]) for i in (0, 4, 8))
    _fill_ext_t(vl_ext, v_ref[...])
    _fill_ext_t(vc_ext, vc_ref[...])

    def window(n):
        start = min(max(n * n_q - WINDOW, 0), seq - span)
        return n * n_q, start

    def sink_row(pair):
        col = lax.broadcasted_iota(jnp.int32, (1, 2 * n_q), 1)
        return jnp.where(col < n_q, sink_ref[2 * pair], sink_ref[2 * pair + 1]) * LOG2E

    for t in range(n_blocks + 2):
        new, old = t % 2, 1 - t % 2
        if 0 <= t - 2:
            q0, start = window(t - 2)
            vt_loc = vl_ext[:, start:start + span]
            groups = []
            for pair in range(2):
                p_ref = p_refs[new][pair]
                o_t = _dot(vt_loc, p_ref[:span, :]) + _dot(vc_ext[...], p_ref[span:, :])
                den = o_t[LANES:LANES + 1, :] + jnp.exp2(sink_row(pair) - m_refs[new][pair][0:1, :])
                inv = 1.0 / den
                lo_t = o_t[:HEAD_DIM, :n_q] * inv[:, :n_q]
                hi_t = o_t[HEAD_DIM:LANES, n_q:] * inv[:, n_q:]
                groups.append(jnp.concatenate([lo_t, hi_t], axis=0).T)
            o_ref[q0:q0 + n_q, :] = _gqa_output(groups)
        if 0 <= t - 1 < n_blocks:
            for pair in range(2):
                s_ref, p_ref = s_refs[old][pair], p_refs[old][pair]
                m = jnp.maximum(jnp.max(s_ref[...], axis=0, keepdims=True), sink_row(pair))
                p_ref[...] = jnp.exp2(s_ref[...] - m).astype(BF16)
                m_refs[old][pair][...] = jnp.broadcast_to(m, (8, 2 * n_q))
        if t < n_blocks:
            q0, start = window(t)
            q = q_ref[q0:q0 + n_q, :]
            k_loc = k_ref[start:start + span, :]
            k_pos = start + lax.broadcasted_iota(jnp.int32, (span, 1), 0)
            q_pos = q0 + (lax.broadcasted_iota(jnp.int32, (1, 2 * n_q), 1) & (n_q - 1))
            valid = jnp.abs(k_pos - q_pos) <= WINDOW
            for pair in range(2):
                lhs = _stack_masked(q[:, pair * LANES:(pair + 1) * LANES], masks)
                s_ref = s_refs[new][pair]
                s_ref[:span, :] = jnp.where(valid, _dot_nt(k_loc, lhs), NEG_INF)
                s_ref[span:, :] = _dot_nt(kc_ref[...], lhs)


EXT_ROWS = 16


def _fill_ext_t(ext_ref, v):
    row = lax.broadcasted_iota(jnp.int32, (LANES, LANES), 0)
    col = lax.broadcasted_iota(jnp.int32, (LANES, LANES), 1)
    eye = jnp.where(row == col, 1.0, 0.0).astype(BF16)
    ext_ref[:LANES, :] = _dot_nt(eye, v).astype(BF16)
    ext_ref[LANES:, :] = jnp.ones((EXT_ROWS, v.shape[0]), BF16)


def _value_chunks_t(lat_ref, ctx_ref):
    n_lat = lat_ref.shape[1]
    return [lat_ref[:, r:r + KEY_CHUNK] for r in range(0, n_lat, KEY_CHUNK)] + [ctx_ref[...]]


def _scores_stage_t(lhs, k_chunks, s_ref, m_ref):
    running = None
    off = 0
    for k in k_chunks:
        s = _dot_nt(k, lhs)
        n = k.shape[0]
        s_ref[off:off + n, :] = s
        top = jnp.max(s, axis=0, keepdims=True)
        running = top if running is None else jnp.maximum(running, top)
        off += n
        yield
    m_ref[...] = jnp.broadcast_to(running, m_ref.shape)


def _values_stage_t(s_ref, m_ref, vt_chunks, result):
    m = m_ref[0:1, :]
    acc = None
    off = 0
    for vt in vt_chunks:
        n = vt.shape[1]
        p = jnp.exp2(s_ref[off:off + n, :] - m).astype(BF16)
        part = _dot(vt, p)
        acc = part if acc is None else acc + part
        off += n
        yield
    result.append(acc)


def _interleave(*stages):
    stages = list(stages)
    while stages:
        stages = [stage for stage in stages if next(stage, stages) is not stages]


def _dense_slots(n_q, qp_ref, qn_ref, kp_ref, kn_ref, kcp_ref, kcn_ref):
    return ((0, qp_ref, slice(n_q, 2 * n_q), kp_ref, kcp_ref),
            (1, qn_ref, slice(0, n_q), kn_ref, kcn_ref))


def _refill_ext_per_sample(blocks_per_seq, fill):
    pl.when(lax.rem(jnp.maximum(pl.program_id(0) - 1, 0), blocks_per_seq) == 0)(fill)


def _attn_b_kernel(blocks_per_seq, qp_ref, qn_ref, kp_ref, kn_ref, kcp_ref, kcn_ref, v_ref,
                   vc_ref, o_ref, vl_ext, vc_ext, *bufs):
    n_q = o_ref.shape[0] // 2
    s_refs, m_refs = _by_parity(bufs[0:4]), _by_parity(bufs[4:8])
    _zero_at_first_step(bufs)

    def fill():
        _fill_ext_t(vl_ext, v_ref[...])
        _fill_ext_t(vc_ext, vc_ref[...])

    _refill_ext_per_sample(blocks_per_seq, fill)
    masks = _lane_masks(2, BF16)
    for parity, q_ref, q_rows, k_ref, kc_ref in _dense_slots(n_q, qp_ref, qn_ref, kp_ref, kn_ref,
                                                            kcp_ref, kcn_ref):
        done, todo = parity, 1 - parity
        q = q_ref[q_rows, :]
        results, stages = [], []
        for pair in range(2):
            results.append([])
            stages.append(_values_stage_t(s_refs[done][pair], m_refs[done][pair],
                                          _value_chunks_t(vl_ext, vc_ext), results[pair]))
            lhs = _stack_masked(q[:, pair * LANES:(pair + 1) * LANES], masks)
            stages.append(_scores_stage_t(lhs, _key_chunks(k_ref, kc_ref),
                                          s_refs[todo][pair], m_refs[todo][pair]))
        _interleave(*stages)
        groups = []
        for (o_t,) in results:
            inv = 1.0 / o_t[LANES:LANES + 1, :]
            lo_t = o_t[:HEAD_DIM, :n_q] * inv[:, :n_q]
            hi_t = o_t[HEAD_DIM:LANES, n_q:] * inv[:, n_q:]
            groups.append(jnp.concatenate([lo_t, hi_t], axis=0).T)
        o_ref[parity * n_q:(parity + 1) * n_q, :] = _gqa_output(groups)


def _attn_d_kernel(lam_init, blocks_per_seq, lamv_ref, gain_ref, qp_ref, qn_ref, kp_ref, kn_ref,
                   kcp_ref, kcn_ref, v_ref, vc_ref, o_ref, vl_ext, vc_ext, *bufs):
    n_q = o_ref.shape[0] // 2
    s_refs, m_refs = _by_parity(bufs[0:8]), _by_parity(bufs[8:16])
    _zero_at_first_step(bufs)

    def fill():
        for jb in range(2):
            cols = slice(jb * LANES, (jb + 1) * LANES)
            _fill_ext_t(vl_ext.at[jb], v_ref[:, cols])
            _fill_ext_t(vc_ext.at[jb], vc_ref[:, cols])

    _refill_ext_per_sample(blocks_per_seq, fill)
    lam = _lam_value(lamv_ref, lam_init)
    masks = _lane_masks(4, BF16)
    for parity, q_ref, q_rows, k_ref, kc_ref in _dense_slots(n_q, qp_ref, qn_ref, kp_ref, kn_ref,
                                                            kcp_ref, kcn_ref):
        done, todo = parity, 1 - parity
        q = q_ref[q_rows, :]
        results, stages = [], []
        for jb in range(2):
            cols = slice(jb * LANES, (jb + 1) * LANES)
            for head in range(2):
                g = 2 * jb + head
                results.append([])
                stages.append(_values_stage_t(s_refs[done][g], m_refs[done][g],
                                              _value_chunks_t(vl_ext.at[jb], vc_ext.at[jb]),
                                              results[g]))
                lhs = _stack_masked(q[:, cols], masks[2 * head:2 * head + 2])
                stages.append(_scores_stage_t(lhs, _key_chunks(k_ref, kc_ref, cols),
                                              s_refs[todo][g], m_refs[todo][g]))
        _interleave(*stages)
        outs = []
        for jb in range(2):
            ys_t = []
            for head in range(2):
                (o_t,) = results[2 * jb + head]
                dims = slice(head * HEAD_DIM, (head + 1) * HEAD_DIM)
                r = o_t[dims, :] * (1.0 / o_t[LANES:LANES + 1, :])
                y_t = r[:, :n_q] - lam * r[:, n_q:]
                ms = jnp.mean(y_t * y_t, axis=0, keepdims=True)
                ys_t.append(y_t * lax.rsqrt(ms + NORM_EPS) * gain_ref[dims, :])
            outs.append(jnp.concatenate(ys_t, axis=0).T * (1.0 - lam_init))
        o_ref[parity * n_q:(parity + 1) * n_q, :] = jnp.concatenate(outs, axis=1).astype(BF16)


def _attn_c_kernel(n_grid_rows, q_ref, k_ref, v_ref, kc_ref, vc_ref, bias_ref,
                   o_ref, vl_ext, vc_ext, *bufs):
    n_q = GRID_W
    n_loc = NA_KR * GRID_W
    masks = _lane_masks(2, BF16)
    s_refs, p_refs = _by_parity(bufs[0:4]), _by_parity(bufs[4:8])
    for jb in range(2):
        cols = slice(jb * LANES, (jb + 1) * LANES)
        _fill_ext(vl_ext.at[jb], v_ref[:, cols])
        _fill_ext(vc_ext.at[jb], vc_ref[:, cols])

    def geometry(r):
        r_start = min(max(r - NA_KR // 2, 0), n_grid_rows - NA_KR)
        return slice(r * n_q, (r + 1) * n_q), slice(r_start * GRID_W, r_start * GRID_W + n_loc), \
            r - r_start

    for t in range(n_grid_rows + 2):
        new, old = t % 2, 1 - t % 2
        if 0 <= t - 2:
            q_rows, k_rows, _ = geometry(t - 2)
            outs = []
            for jb in range(2):
                o = _weighted_values(p_refs[new][jb], [vl_ext[jb, k_rows, :], vc_ext[jb]])
                outs.append(_pair_select(o[:, :LANES] / o[:, LANES:], n_q))
            o_ref[q_rows, :] = jnp.concatenate(outs, axis=1).astype(BF16)
        if 0 <= t - 1 < n_grid_rows:
            for jb in range(2):
                _numerators(s_refs[old][jb], p_refs[old][jb])
        if t < n_grid_rows:
            q_rows, k_rows, off = geometry(t)
            q = q_ref[q_rows, :]
            for jb in range(2):
                cols = slice(jb * LANES, (jb + 1) * LANES)
                lhs = _stack_masked(q[:, cols], masks)
                bias = jnp.concatenate([bias_ref[2 * jb, off], bias_ref[2 * jb + 1, off]], axis=0)
                _scores(lhs, [k_ref[k_rows, cols], kc_ref[:, cols]], s_refs[new][jb],
                        score_fn=lambda s: s + bias)


def _attn_ctx_kernel(lam_init, sink_ref, lamv_ref, gain_ref, aq_ref, ak_ref, av_ref, bq_ref, bk_ref,
                     bv_ref, cq_ref, ck_ref, cv_ref, dq_ref, dk_ref, dv_ref, o_ref,
                     s0, s1, m0, m1):
    n_q = aq_ref.shape[0]
    m2 = _lane_masks(2, BF16)
    m4 = _lane_masks(4, BF16)
    lane = lax.broadcasted_iota(jnp.int32, (1, LANES), 1)
    sets = ((s0, m0), (s1, m1))

    def ext(v):
        return jnp.concatenate([v, jnp.ones(v.shape, v.dtype)], axis=1)

    def attend(lhs, k, v_ext, s_ref, m_ref, floor=None):
        _scores_stage(lhs, [k], s_ref, m_ref, floor=floor)
        return _values_stage(s_ref, m_ref, [v_ext])

    def out_cols(group, pair):
        c0 = group * GROUP_WIDTH + pair * LANES
        return slice(c0, c0 + LANES)

    for group, (q_ref_, k_ref_, v_ref_) in enumerate(((aq_ref, ak_ref, av_ref),
                                                      (bq_ref, bk_ref, bv_ref))):
        q = q_ref_[...]
        v_ext = ext(v_ref_[...])
        groups = []
        for pair, (s_ref, m_ref) in enumerate(sets):
            lhs = _stack_masked(q[:, pair * LANES:(pair + 1) * LANES], m2)
            sink = _sink_rows(sink_ref, pair, n_q) if group == 0 else None
            o = attend(lhs, k_ref_[...], v_ext, s_ref, m_ref, floor=sink)
            den = o[:, LANES:]
            if group == 0:
                den = den + jnp.exp2(sink - m_ref[...])
            groups.append(_pair_select(o[:, :LANES] / den, n_q))
        o_ref[:, group * GROUP_WIDTH:(group + 1) * GROUP_WIDTH] = _gqa_output(groups)

    q = cq_ref[...]
    for jb, (s_ref, m_ref) in enumerate(sets):
        cols = slice(jb * LANES, (jb + 1) * LANES)
        o = attend(_stack_masked(q[:, cols], m2), ck_ref[:, cols], ext(cv_ref[:, cols]),
                   s_ref, m_ref)
        o_ref[:, out_cols(2, jb)] = _pair_select(o[:, :LANES] / o[:, LANES:], n_q).astype(BF16)

    lam = _lam_value(lamv_ref, lam_init)
    q = dq_ref[...]
    for jb in range(2):
        cols = slice(jb * LANES, (jb + 1) * LANES)
        v_ext = ext(dv_ref[:, cols])
        ys = []
        for head, (s_ref, m_ref) in enumerate(sets):
            lhs = _stack_masked(q[:, cols], m4[2 * head:2 * head + 2])
            ys.append(_diff_combine(attend(lhs, dk_ref[:, cols], v_ext, s_ref, m_ref), n_q, lam))
        y = jnp.where(lane < HEAD_DIM, ys[0], ys[1])
        o_ref[:, out_cols(3, jb)] = _sub_ln(y, gain_ref[...], 1.0 - lam_init).astype(BF16)


def _rope_tables(seq, n_extra):
    t = np.arange(seq)
    row = (t // GRID_W).astype(np.float64)
    col = (t % GRID_W).astype(np.float64)
    tables = []
    for dim in (HEAD_DIM, D_SUB):
        half = dim // 2
        freqs = ROPE_BASE ** (-np.arange(0, half, 2, dtype=np.float64) / half)
        ang_r = row[:, None] * freqs[None, :]
        ang_c = col[:, None] * freqs[None, :]
        ang = np.concatenate([ang_r, ang_r, ang_c, ang_c], axis=-1)
        quarter = dim // 4
        sign = np.where((np.arange(dim) % (2 * quarter)) < quarter, -1.0, 1.0)
        cos = np.tile(np.cos(ang), (1, LANES // dim))
        sin = np.tile(np.sin(ang) * sign[None, :], (1, LANES // dim))
        cos = np.concatenate([cos, np.ones((n_extra, LANES))], axis=0)
        sin = np.concatenate([sin, np.zeros((n_extra, LANES))], axis=0)
        tables += [cos, sin]
    return jnp.asarray(np.stack(tables), dtype=F32)


def _neighbourhood_bias(rel_bias):
    heads = rel_bias.shape[0]
    return pl.pallas_call(
        _bias_kernel,
        grid=(heads,),
        in_specs=[pl.BlockSpec(memory_space=pltpu.SMEM)],
        out_specs=pl.BlockSpec((1, NA_KR, GRID_W, NA_KR * GRID_W), lambda h: (h, 0, 0, 0)),
        out_shape=jax.ShapeDtypeStruct((heads, NA_KR, GRID_W, NA_KR * GRID_W), F32),
        compiler_params=_params(("arbitrary",)),
        name="neighbourhood_bias",
    )(rel_bias.astype(F32).reshape(-1))


def _stage_scratch(n_rows, n_keys, n_groups=2):
    return ([pltpu.VMEM((n_rows, n_keys), F32)] * (2 * n_groups)
            + [pltpu.VMEM((n_rows, LANES), F32)] * (2 * n_groups))


def _stage_scratch_t(n_rows, n_keys, n_groups=2):
    return ([pltpu.VMEM((n_keys, n_rows), F32)] * (2 * n_groups)
            + [pltpu.VMEM((8, n_rows), F32)] * (2 * n_groups))


def _loop_scratch(n_rows, n_keys, with_maxima=False):
    maxima = [pltpu.VMEM((n_rows, LANES), F32)] * 4 if with_maxima else []
    return ([pltpu.VMEM((n_rows, n_keys), F32)] * 4 + [pltpu.VMEM((n_rows, n_keys), BF16)] * 4
            + maxima)


def kernel(x, c, ctx, c_ctx, w_ada, b_ada, w_ffn1_gate, w_ffn1_up, w_ffn1_down, w_in, w_out,
           sink_logit, q_norm_g, k_norm_g, rel_pos_bias, lam_q1, lam_k1, lam_q2, lam_k2, subln_g,
           w_ffn2_gate, w_ffn2_up, w_ffn2_down, final_norm_g):
    batch, seq, d = x.shape
    n_ctx = ctx.shape[1]
    depth = w_ada.shape[0]
    assert d == D_MODEL and seq % TOKEN_TILE == 0 and (batch * n_ctx) % TOKEN_TILE == 0
    assert seq % GRID_W == 0 and w_in.shape[-1] == IN_WIDTH and seq % KEY_CHUNK == 0
    assert seq % (2 * GQA_TILE) == 0 and seq % (2 * DIFF_TILE) == 0
    assert seq % (2 * BLOCK) == 0 and seq % (2 * GRID_W) == 0
    t_lat, t_ctx = batch * seq, batch * n_ctx
    t_all = t_lat + t_ctx
    tm = TOKEN_TILE
    n_lat_tiles, n_ctx_tiles = t_lat // tm, t_ctx // tm
    tiles_per_seq = seq // tm
    grid_rows = seq // GRID_W
    assert grid_rows >= NA_KR and rel_pos_bias.shape[1:] == (GROUP_HEADS, 2 * NA_KR - 1, 2 * NA_KC - 1)

    cc = jnp.concatenate([c, c_ctx[None, :], jnp.zeros((16 - batch - 1, d), F32)], axis=0)
    mods = _mods(cc, w_ada, b_ada).reshape(depth, 16, N_MOD, d)

    def group(i):
        return jnp.minimum(i // tiles_per_seq, batch)

    rope = _rope_tables(seq, tm)

    def rope_block(i):
        return jnp.where(i < n_lat_tiles, i % tiles_per_seq, tiles_per_seq)

    lane_gain = lambda g: jnp.tile(g.astype(F32), LANES // HEAD_DIM)[None, :]

    tile_spec = pl.BlockSpec((tm, d), lambda i: (i, 0))

    ffn1 = tuple(_to_bf16(w) for w in (w_ffn1_gate, w_ffn1_up, w_ffn1_down))
    ffn2 = tuple(_to_bf16(w) for w in (w_ffn2_gate, w_ffn2_up, w_ffn2_down))
    w_in_b = _to_bf16(w_in, swap_cols=(COL_AQ * LANES, COL_BQ * LANES))
    w_out_b = _to_bf16(w_out)

    for l in range(depth):
        last = l == depth - 1
        lam_init = 0.8 - 0.6 * math.exp(-0.3 * l)
        mod_spec = pl.BlockSpec((1, N_MOD, d), lambda i: (group(i), 0, 0))

        gains = jnp.concatenate([lane_gain(q_norm_g[l]), lane_gain(k_norm_g[l]),
                                 jnp.zeros((6, LANES), F32)], axis=0)
        if l == 0:
            streams = (x.reshape(t_lat, d), ctx.reshape(t_ctx, d))
            stream_specs = [
                pl.BlockSpec((tm, d), lambda i: (jnp.minimum(i, n_lat_tiles - 1), 0)),
                pl.BlockSpec((tm, d), lambda i: (jnp.maximum(i - n_lat_tiles, 0), 0))]
        else:
            streams, stream_specs = (xs,), [tile_spec]
        xs, qkv = pl.pallas_call(
            functools.partial(_pre_kernel, n_lat_tiles if l == 0 else None),
            grid=(n_lat_tiles + n_ctx_tiles,),
            in_specs=stream_specs + [
                mod_spec,
                _resident((d, D_FF), l), _resident((d, D_FF), l), _resident((D_FF, d), l),
                _resident((d, IN_WIDTH), l),
                pl.BlockSpec((4, tm, LANES), lambda i: (0, rope_block(i), 0)),
                _resident((8, LANES)),
            ],
            out_specs=[tile_spec, pl.BlockSpec((tm, IN_WIDTH), lambda i: (i, 0))],
            out_shape=[jax.ShapeDtypeStruct((t_all, d), F32),
                       jax.ShapeDtypeStruct((t_all, IN_WIDTH), BF16)],
            scratch_shapes=[pltpu.VMEM((tm, D_FF), BF16)],
            compiler_params=_params(("arbitrary",)),
            name=f"pre_l{l}",
        )(*streams, mods[l], *ffn1, w_in_b, rope, gains)

        ctx_row = t_lat // n_ctx
        sink_perm = sink_logit[l].astype(F32)[jnp.array(GQA_HEAD_ORDER)]
        lamv = jnp.pad(jnp.stack([lam_q1[l], lam_k1[l], lam_q2[l], lam_k2[l]]).astype(F32),
                       ((0, 4), (0, LANES - D_SUB)))
        subln = lane_gain(subln_g[l])
        subln_t = jnp.broadcast_to(subln[0][:, None], (LANES, DIFF_TILE))
        smem = pl.BlockSpec(memory_space=pltpu.SMEM)

        def lat(cols, col):
            return pl.BlockSpec((seq, cols), lambda b, *_: (b, col))

        def ctxb(cols, col):
            return pl.BlockSpec((n_ctx, cols), lambda b, *_: (ctx_row + b, col))

        def v_ext_scratch(n_blocks):
            lead = () if n_blocks == 1 else (n_blocks,)
            return [pltpu.VMEM(lead + (seq, 2 * LANES), BF16),
                    pltpu.VMEM(lead + (n_ctx, 2 * LANES), BF16)]

        y_a = pl.pallas_call(
            functools.partial(_attn_a_kernel, seq),
            grid=(batch,),
            in_specs=[smem, lat(2 * LANES, COL_AQ // 2), lat(LANES, COL_AK), lat(LANES, COL_AV),
                      ctxb(LANES, COL_AK), ctxb(LANES, COL_AV)],
            out_specs=pl.BlockSpec((seq, GROUP_WIDTH), lambda b: (b, 0)),
            out_shape=jax.ShapeDtypeStruct((t_lat, GROUP_WIDTH), BF16),
            scratch_shapes=(
                [pltpu.VMEM((LANES + EXT_ROWS, seq), BF16),
                 pltpu.VMEM((LANES + EXT_ROWS, n_ctx), BF16)]
                + [pltpu.VMEM((BLOCK + 2 * WINDOW + n_ctx, 2 * BLOCK), F32)] * 4
                + [pltpu.VMEM((BLOCK + 2 * WINDOW + n_ctx, 2 * BLOCK), BF16)] * 4
                + [pltpu.VMEM((8, 2 * BLOCK), F32)] * 4),
            compiler_params=_params(("arbitrary",)),
            name=f"attn_a_l{l}",
        )(sink_perm, qkv, qkv, qkv, qkv, qkv)

        def v_ext_t_scratch(n_blocks):
            lead = () if n_blocks == 1 else (n_blocks,)
            return [pltpu.VMEM(lead + (LANES + EXT_ROWS, seq), BF16),
                    pltpu.VMEM(lead + (LANES + EXT_ROWS, n_ctx), BF16)]

        def dense_mixer(body, name, tq, n_groups, q_col, k_cols, k_col, v_col, params=(),
                        ext_blocks=0):
            blocks_per_seq = seq // (2 * tq)
            n_blocks = batch * blocks_per_seq

            def cur(j):
                return jnp.minimum(j, n_blocks - 1)

            def prev(j):
                return jnp.maximum(j - 1, 0)

            def sample_of(block):
                return lambda j: block(j) // blocks_per_seq

            specs = [pl.BlockSpec((2 * tq, 2 * LANES), lambda j: (prev(j), q_col)),
                     pl.BlockSpec((2 * tq, 2 * LANES), lambda j: (cur(j), q_col))]
            for rows, row0 in ((seq, 0), (n_ctx, ctx_row)):
                for block in (prev, cur):
                    specs.append(pl.BlockSpec(
                        (rows, k_cols), lambda j, b=sample_of(block), r=row0: (r + b(j), k_col)))
            for rows, row0 in ((seq, 0), (n_ctx, ctx_row)):
                specs.append(pl.BlockSpec(
                    (rows, k_cols), lambda j, b=sample_of(prev), r=row0: (r + b(j), v_col)))
            return pl.pallas_call(
                functools.partial(body, blocks_per_seq),
                grid=(n_blocks + 1,),
                in_specs=[_resident(p.shape) for p in params] + specs,
                out_specs=pl.BlockSpec((2 * tq, GROUP_WIDTH), lambda j: (prev(j), 0)),
                out_shape=jax.ShapeDtypeStruct((t_lat, GROUP_WIDTH), BF16),
                scratch_shapes=(v_ext_t_scratch(ext_blocks)
                                + _stage_scratch_t(2 * tq, seq + n_ctx, n_groups=n_groups)),
                compiler_params=_params(("arbitrary",)),
                name=name,
            )(*params, *([qkv] * 8))

        y_b = dense_mixer(_attn_b_kernel, f"attn_b_l{l}", GQA_TILE, 2,
                          COL_BQ // 2, LANES, COL_BK, COL_BV, ext_blocks=1)
        y_d = dense_mixer(functools.partial(_attn_d_kernel, lam_init), f"attn_d_l{l}", DIFF_TILE, 4,
                          COL_DQ // 2, 2 * LANES, COL_DK // 2, COL_DV // 2, params=(lamv, subln_t),
                          ext_blocks=2)

        bias = _neighbourhood_bias(rel_pos_bias[l])
        y_c = pl.pallas_call(
            functools.partial(_attn_c_kernel, grid_rows),
            grid=(batch,),
            in_specs=[lat(2 * LANES, COL_CQ // 2),
                      lat(2 * LANES, COL_CK // 2), lat(2 * LANES, COL_CV // 2),
                      ctxb(2 * LANES, COL_CK // 2), ctxb(2 * LANES, COL_CV // 2),
                      _resident(bias.shape)],
            out_specs=pl.BlockSpec((seq, GROUP_WIDTH), lambda b: (b, 0)),
            out_shape=jax.ShapeDtypeStruct((t_lat, GROUP_WIDTH), BF16),
            scratch_shapes=(v_ext_scratch(2)
                            + _loop_scratch(2 * GRID_W, NA_KR * GRID_W + n_ctx)),
            compiler_params=_params(("arbitrary",)),
            name=f"attn_c_l{l}",
        )(qkv, qkv, qkv, qkv, qkv, bias)

        ffn2_specs = [_resident((d, D_FF), l), _resident((d, D_FF), l), _resident((D_FF, d), l)]
        fng = final_norm_g.astype(F32)[None, :]
        y_spec = pl.BlockSpec((tm, GROUP_WIDTH), lambda i: (jnp.minimum(i, n_lat_tiles - 1), 0))
        post_scratch = [pltpu.VMEM((tm, d), BF16), pltpu.VMEM((tm, D_FF), BF16)]

        if not last:
            y_ctx = pl.pallas_call(
                functools.partial(_attn_ctx_kernel, lam_init),
                grid=(batch,),
                in_specs=[smem, _resident((8, LANES)), _resident((1, LANES)),
                          ctxb(2 * LANES, COL_AQ // 2), ctxb(LANES, COL_AK), ctxb(LANES, COL_AV),
                          ctxb(2 * LANES, COL_BQ // 2), ctxb(LANES, COL_BK), ctxb(LANES, COL_BV),
                          ctxb(2 * LANES, COL_CQ // 2), ctxb(2 * LANES, COL_CK // 2),
                          ctxb(2 * LANES, COL_CV // 2),
                          ctxb(2 * LANES, COL_DQ // 2), ctxb(2 * LANES, COL_DK // 2),
                          ctxb(2 * LANES, COL_DV // 2)],
                out_specs=pl.BlockSpec((n_ctx, d), lambda b: (b, 0)),
                out_shape=jax.ShapeDtypeStruct((t_ctx, d), BF16),
                scratch_shapes=_stage_scratch(2 * n_ctx, n_ctx, n_groups=1),
                compiler_params=_params(("arbitrary",)),
                name=f"attn_ctx_l{l}",
            )(sink_perm, lamv, subln, *([qkv] * 12))

            xs = pl.pallas_call(
                functools.partial(_post_kernel, n_lat_tiles, False),
                grid=(n_lat_tiles + n_ctx_tiles,),
                in_specs=[tile_spec, mod_spec, y_spec, y_spec, y_spec, y_spec,
                          pl.BlockSpec((tm, d), lambda i: (jnp.maximum(i - n_lat_tiles, 0), 0)),
                          _resident((d, d), l)] + ffn2_specs + [_resident((1, d))],
                out_specs=tile_spec,
                out_shape=jax.ShapeDtypeStruct((t_all, d), F32),
                scratch_shapes=post_scratch,
                compiler_params=_params(("arbitrary",)),
                name=f"post_l{l}",
            )(xs, mods[l], y_a, y_b, y_c, y_d, y_ctx, w_out_b, *ffn2, fng)
        else:
            xs = pl.pallas_call(
                functools.partial(_post_kernel, None, True),
                grid=(n_lat_tiles,),
                in_specs=[tile_spec, mod_spec, y_spec, y_spec, y_spec, y_spec,
                          _resident((d, d), l)] + ffn2_specs + [_resident((1, d))],
                out_specs=tile_spec,
                out_shape=jax.ShapeDtypeStruct((t_lat, d), F32),
                scratch_shapes=post_scratch,
                compiler_params=_params(("arbitrary",)),
                name=f"post_l{l}",
            )(xs, mods[l], y_a, y_b, y_c, y_d, w_out_b, *ffn2, fng)

    return xs.reshape(batch, seq, d)
```
